```python
import math
import jax, jax.numpy as jnp
from jax import lax
import numpy as np

D_MODEL = 1024
BATCH = 32
SEQ = 2048
DEPTH = 4

GRID_W = 64
N_MIXERS = 4
GROUP_WIDTH = D_MODEL // N_MIXERS
HEAD_DIM = 64
Q_BLOCK = 128
EPS = 1e-6

NA_HEADS = GROUP_WIDTH // HEAD_DIM
NA_WIN_ROWS = 8
NA_WIN_COLS = 16

DIFF_HEADS = GROUP_WIDTH // HEAD_DIM
DIFF_QK_DIM = HEAD_DIM // 2
DIFF_V_DIM = HEAD_DIM

GQA_HEADS = GROUP_WIDTH // HEAD_DIM
GQA_KV_HEADS = GQA_HEADS // 2
ROPE_THETA = 10000.0

MLA_HEADS = GROUP_WIDTH // HEAD_DIM
MLA_Q_LORA = GROUP_WIDTH
MLA_KV_LORA = GROUP_WIDTH // 2
MLA_NOPE_DIM = HEAD_DIM
MLA_ROPE_DIM = HEAD_DIM // 2
MLA_V_DIM = HEAD_DIM

D_FF = -(-8 * D_MODEL // (3 * 256)) * 256

IN_SIZES = (
    NA_HEADS * HEAD_DIM, NA_HEADS * HEAD_DIM, NA_HEADS * HEAD_DIM,
    DIFF_HEADS * 2 * DIFF_QK_DIM, DIFF_HEADS * 2 * DIFF_QK_DIM, DIFF_HEADS * DIFF_V_DIM,
    GQA_HEADS * HEAD_DIM, GQA_KV_HEADS * HEAD_DIM, GQA_KV_HEADS * HEAD_DIM,
    MLA_Q_LORA, MLA_KV_LORA, MLA_ROPE_DIM,
)
IN_WIDTH = sum(IN_SIZES)
MIX_WIDTH = NA_HEADS * HEAD_DIM + DIFF_HEADS * DIFF_V_DIM + GQA_HEADS * HEAD_DIM + MLA_HEADS * MLA_V_DIM

kernel_name = "hybrid_parallel_group_encoder"


def rms_norm(x, w):
    xf = x.astype(jnp.float32)
    y = xf * lax.rsqrt(jnp.mean(xf * xf, axis=-1, keepdims=True) + EPS)
    return (y * w.astype(jnp.float32)).astype(x.dtype)


def rope(x, pos):
    d = x.shape[-1]
    inv = ROPE_THETA ** (-jnp.arange(0, d, 2, dtype=jnp.float32) / d)
    ang = pos.astype(jnp.float32)[:, None] * inv[None, :]
    ang = jnp.concatenate([ang, ang], axis=-1)
    xf = x.astype(jnp.float32)
    x1, x2 = jnp.split(xf, 2, axis=-1)
    rot = jnp.concatenate([-x2, x1], axis=-1)
    return (xf * jnp.cos(ang) + rot * jnp.sin(ang)).astype(x.dtype)


def axial_rope(x, row, col):
    half = x.shape[-1] // 2
    return jnp.concatenate([rope(x[..., :half], row), rope(x[..., half:], col)], axis=-1)


def split_heads(t, n):
    b, s, _ = t.shape
    return t.reshape(b, s, n, -1).transpose(0, 2, 1, 3)


def merge_heads(t):
    b, h, s, d = t.shape
    return t.transpose(0, 2, 1, 3).reshape(b, s, h * d)


def to_blocks(t):
    b, h, s, d = t.shape
    return t.reshape(b, h, s // Q_BLOCK, Q_BLOCK, d).transpose(2, 0, 1, 3, 4)


def from_blocks(t):
    nb, b, h, qb, d = t.shape
    return t.transpose(1, 2, 0, 3, 4).reshape(b, h, nb * qb, d)


def neighbourhood_attention(q, k, v, rel_bias):
    b, h, s, d = q.shape
    rows = s // GRID_W
    wr = min(NA_WIN_ROWS, rows)
    wc = NA_WIN_COLS
    q5 = q.reshape(b, h, rows, GRID_W, d)
    k5 = k.reshape(b, h, rows, GRID_W, d)
    v5 = v.reshape(b, h, rows, GRID_W, d)
    r = jnp.arange(rows)
    row_start = jnp.clip(r - wr // 2, 0, rows - wr)
    c = jnp.arange(GRID_W)
    col_idx = jnp.clip(c - wc // 2, 0, GRID_W - wc)[:, None] + jnp.arange(wc)[None, :]
    dc_idx = col_idx - c[:, None] + (NA_WIN_COLS - 1)
    scale = d ** -0.5

    def row_block(args):
        q_r, rs, r_i = args
        k_band = lax.dynamic_slice_in_dim(k5, rs, wr, axis=2)
        v_band = lax.dynamic_slice_in_dim(v5, rs, wr, axis=2)
        k_win = k_band[:, :, :, col_idx, :]
        v_win = v_band[:, :, :, col_idx, :]
        dr_idx = rs + jnp.arange(wr) - r_i + (NA_WIN_ROWS - 1)
        bias = rel_bias[:, dr_idx[None, :, None], dc_idx[:, None, :]]
        sc = jnp.einsum('bhcd,bhrcwd->bhcrw', q_r, k_win).astype(jnp.float32) * scale + bias.astype(jnp.float32)
        p = jax.nn.softmax(sc.reshape(b, h, GRID_W, wr * wc), axis=-1).reshape(b, h, GRID_W, wr, wc)
        return jnp.einsum('bhcrw,bhrcwd->bhcd', p.astype(v.dtype), v_win)

    out = lax.map(row_block, (q5.transpose(2, 0, 1, 3, 4), row_start, r))
    return out.transpose(1, 2, 0, 3, 4).reshape(b, h, s, d)


def diff_attention(q1, q2, k1, k2, v, lam, slopes):
    s = k1.shape[2]
    pos = jnp.arange(s)
    scale = DIFF_QK_DIM ** -0.5

    def block(args):
        qb1, qb2, qp = args
        dist = jnp.abs(qp[:, None] - pos[None, :]).astype(jnp.float32)
        alibi = -slopes[:, None, None] * dist
        s1 = jnp.einsum('bhqd,bhkd->bhqk', qb1, k1).astype(jnp.float32) * scale + alibi
        s2 = jnp.einsum('bhqd,bhkd->bhqk', qb2, k2).astype(jnp.float32) * scale + alibi
        a = jax.nn.softmax(s1, axis=-1) - lam * jax.nn.softmax(s2, axis=-1)
        return jnp.einsum('bhqk,bhkd->bhqd', a.astype(v.dtype), v)

    out = lax.map(block, (to_blocks(q1), to_blocks(q2), pos.reshape(-1, Q_BLOCK)))
    return from_blocks(out)


def gqa_attention(q, k, v):
    b, h, s, d = q.shape
    kvh = k.shape[1]
    g = h // kvh
    scale = d ** -0.5

    def block(qb):
        qg = qb.reshape(b, kvh, g, Q_BLOCK, d)
        sc = jnp.einsum('bkgqd,bksd->bkgqs', qg, k).astype(jnp.float32) * scale
        o = jnp.einsum('bkgqs,bksd->bkgqd', jax.nn.softmax(sc, axis=-1).astype(v.dtype), v)
        return o.reshape(b, h, Q_BLOCK, d)

    return from_blocks(lax.map(block, to_blocks(q)))


def mla_attention(q_nope, q_rope, k_nope, k_rope, v):
    scale = (MLA_NOPE_DIM + MLA_ROPE_DIM) ** -0.5

    def block(args):
        qn, qr = args
        sc = (jnp.einsum('bhqd,bhkd->bhqk', qn, k_nope)
              + jnp.einsum('bhqd,bkd->bhqk', qr, k_rope)).astype(jnp.float32) * scale
        return jnp.einsum('bhqk,bhkd->bhqd', jax.nn.softmax(sc, axis=-1).astype(v.dtype), v)

    return from_blocks(lax.map(block, (to_blocks(q_nope), to_blocks(q_rope))))


def setup_inputs(seed: int = 0) -> dict:
    key = jax.random.key(seed)
    ks = jax.random.split(key, 21)

    def normal(k, shape, scale):
        return scale * jax.random.normal(k, shape, jnp.float32)

    def gain(k, shape):
        return 1.0 + normal(k, shape, 0.05)

    return {
        "x": normal(ks[0], (BATCH, SEQ, D_MODEL), 1.0),
        "pre_mix_norm": gain(ks[1], (DEPTH, D_MODEL)),
        "w_in": normal(ks[2], (DEPTH, D_MODEL, IN_WIDTH), D_MODEL ** -0.5),
        "na_rel_bias": normal(ks[3], (DEPTH, NA_HEADS, 2 * NA_WIN_ROWS - 1, 2 * NA_WIN_COLS - 1), 0.1),
        "diff_lambda_q1": normal(ks[4], (DEPTH, DIFF_QK_DIM), 0.1),
        "diff_lambda_k1": normal(ks[5], (DEPTH, DIFF_QK_DIM), 0.1),
        "diff_lambda_q2": normal(ks[6], (DEPTH, DIFF_QK_DIM), 0.1),
        "diff_lambda_k2": normal(ks[7], (DEPTH, DIFF_QK_DIM), 0.1),
        "diff_subln": gain(ks[8], (DEPTH, DIFF_V_DIM)),
        "gqa_q_norm": gain(ks[9], (DEPTH, HEAD_DIM)),
        "gqa_k_norm": gain(ks[10], (DEPTH, HEAD_DIM)),
        "mla_q_norm": gain(ks[11], (DEPTH, MLA_Q_LORA)),
        "mla_kv_norm": gain(ks[12], (DEPTH, MLA_KV_LORA)),
        "mla_w_uq": normal(ks[13], (DEPTH, MLA_Q_LORA, MLA_HEADS * (MLA_NOPE_DIM + MLA_ROPE_DIM)), MLA_Q_LORA ** -0.5),
        "mla_w_ukv": normal(ks[14], (DEPTH, MLA_KV_LORA, MLA_HEADS * (MLA_NOPE_DIM + MLA_V_DIM)), MLA_KV_LORA ** -0.5),
        "w_o": normal(ks[15], (DEPTH, MIX_WIDTH, D_MODEL), MIX_WIDTH ** -0.5),
        "post_mix_norm": gain(ks[16], (DEPTH, D_MODEL)),
        "pre_ffn_norm": gain(ks[17], (DEPTH, D_MODEL)),
        "ffn_w_gate_up": normal(ks[18], (DEPTH, D_MODEL, 2 * D_FF), D_MODEL ** -0.5),
        "ffn_w_down": normal(ks[19], (DEPTH, D_FF, D_MODEL), D_FF ** -0.5),
        "post_ffn_norm": gain(ks[20], (DEPTH, D_MODEL)),
    }


def reference(x, pre_mix_norm, w_in, na_rel_bias, diff_lambda_q1, diff_lambda_k1, diff_lambda_q2,
              diff_lambda_k2, diff_subln, gqa_q_norm, gqa_k_norm, mla_q_norm, mla_kv_norm, mla_w_uq,
              mla_w_ukv, w_o, post_mix_norm, pre_ffn_norm, ffn_w_gate_up, ffn_w_down, post_ffn_norm):
    b, s, _ = x.shape
    pos = jnp.arange(s)
    grid_row = pos // GRID_W
    grid_col = pos % GRID_W
    offsets = [int(o) for o in np.cumsum(IN_SIZES)[:-1]]
    alibi_slopes = jnp.asarray([2.0 ** (-8.0 * (i + 1) / DIFF_HEADS) for i in range(DIFF_HEADS)], dtype=jnp.float32)

    for l in range(DEPTH):
        h = rms_norm(x, pre_mix_norm[l])
        proj = jnp.einsum('bsd,de->bse', h, w_in[l])
        (na_q, na_k, na_v, df_q, df_k, df_v, gq_q, gq_k, gq_v,
         ml_cq, ml_ckv, ml_kr) = jnp.split(proj, offsets, axis=-1)

        a_out = neighbourhood_attention(split_heads(na_q, NA_HEADS), split_heads(na_k, NA_HEADS),
                                        split_heads(na_v, NA_HEADS), na_rel_bias[l])

        lambda_init = 0.8 - 0.6 * math.exp(-0.3 * l)
        dq = df_q.reshape(b, s, DIFF_HEADS, 2, DIFF_QK_DIM).transpose(3, 0, 2, 1, 4)
        dk = df_k.reshape(b, s, DIFF_HEADS, 2, DIFF_QK_DIM).transpose(3, 0, 2, 1, 4)
        lam = (jnp.exp(jnp.sum(diff_lambda_q1[l].astype(jnp.float32) * diff_lambda_k1[l].astype(jnp.float32)))
               - jnp.exp(jnp.sum(diff_lambda_q2[l].astype(jnp.float32) * diff_lambda_k2[l].astype(jnp.float32)))
               + lambda_init)
        b_out = diff_attention(dq[0], dq[1], dk[0], dk[1], split_heads(df_v, DIFF_HEADS), lam, alibi_slopes)
        b_out = rms_norm(b_out, diff_subln[l]) * (1.0 - lambda_init)

        cq = axial_rope(rms_norm(split_heads(gq_q, GQA_HEADS), gqa_q_norm[l]), grid_row, grid_col)
        ck = axial_rope(rms_norm(split_heads(gq_k, GQA_KV_HEADS), gqa_k_norm[l]), grid_row, grid_col)
        c_out = gqa_attention(cq, ck, split_heads(gq_v, GQA_KV_HEADS))

        c_q = rms_norm(ml_cq, mla_q_norm[l])
        q_full = split_heads(jnp.einsum('bsr,re->bse', c_q, mla_w_uq[l]), MLA_HEADS)
        q_nope, q_rope = q_full[..., :MLA_NOPE_DIM], rope(q_full[..., MLA_NOPE_DIM:], pos)
        c_kv = rms_norm(ml_ckv, mla_kv_norm[l])
        kv = split_heads(jnp.einsum('bsr,re->bse', c_kv, mla_w_ukv[l]), MLA_HEADS)
        k_nope, mla_v = kv[..., :MLA_NOPE_DIM], kv[..., MLA_NOPE_DIM:]
        k_rope = rope(ml_kr, pos)
        d_out = mla_attention(q_nope, q_rope, k_nope, k_rope, mla_v)

        mix = jnp.concatenate([merge_heads(a_out), merge_heads(b_out),
                               merge_heads(c_out), merge_heads(d_out)], axis=-1)
        mix = jnp.einsum('bse,ed->bsd', mix, w_o[l])
        x = x + rms_norm(mix, post_mix_norm[l])

        h = rms_norm(x, pre_ffn_norm[l])
        gate, up = jnp.split(jnp.einsum('bsd,df->bsf', h, ffn_w_gate_up[l]), 2, axis=-1)
        f = jnp.einsum('bsf,fd->bsd', jax.nn.silu(gate) * up, ffn_w_down[l])
        x = x + rms_norm(f, post_ffn_norm[l])
    return x
```

```python
import functools
import math

import numpy as np
import jax
import jax.numpy as jnp
from jax import lax
from jax.experimental import pallas as pl
from jax.experimental.pallas import tpu as pltpu

F32 = jnp.float32
BF16 = jnp.bfloat16

D_MODEL = 1024
SEQ = 2048
GRID_W = 64
GRID_ROWS = SEQ // GRID_W
HEAD_DIM = 64
N_HEADS = 4
GROUP_WIDTH = 256
EPS = 1e-6
ROPE_THETA = 10000.0

NA_WIN_ROWS = 8
NA_WIN_COLS = 16
DIFF_QK_DIM = 32
MLA_NOPE_DIM = 64
MLA_ROPE_DIM = 32
MLA_KV_LORA = 128
D_FF = 2816

O_GQ_Q, O_GQ_K, O_GQ_V, O_ML_CQ, O_ML_CKV, O_ML_KR = 1536, 1792, 1920, 2048, 2304, 2432

PROJ_WIDTH = 3072
B256_NA_Q, B256_NA_K, B256_NA_V, B256_DF_Q, B256_DF_K, B256_DF_V, B256_GQ_Q, B256_GQ_QR, B256_ML_CQ = range(9)
B128_GQ_K, B128_GQ_KR, B128_GQ_V, B128_ML_CKV, B128_ML_KR, B128_ML_KRR = range(18, 24)

VMEM_LIMIT_BYTES = 56 * 1024 * 1024

TOKEN_TILE = 1024
FF_CHUNK = 256
Q_TILE = 256
NA_GROUP_ROWS = 4
NA_BAND_ROWS = 12
NEG_BIG = -1e30


def _rot_perm(width, group):
    j = np.arange(width)
    jj = j % group
    half = group // 2
    src = (j // group) * group + (jj + half) % group
    sign = np.where(jj < half, -1.0, 1.0).astype(np.float32)
    return src, sign


def _compiler_params():
    return pltpu.CompilerParams(dimension_semantics=("arbitrary",), vmem_limit_bytes=VMEM_LIMIT_BYTES)


def _const_spec(shape):
    zeros = (0,) * len(shape)
    return pl.BlockSpec(shape, lambda i: zeros, pipeline_mode=pl.Buffered(1))


def _rms(x):
    return lax.rsqrt(jnp.mean(x * x, axis=-1, keepdims=True) + EPS)


def _dot(a, b):
    return jnp.dot(a, b, preferred_element_type=F32)


def _dot_nt(a, b):
    return lax.dot_general(a, b, (((1,), (1,)), ((), ())), preferred_element_type=F32)


def _inproj_kernel(x_ref, g_ref, w_ref, o_ref):
    x = x_ref[...]
    h = (x * _rms(x) * g_ref[...]).astype(BF16)
    o_ref[...] = _dot(h, w_ref[...])


def _inproj(x2d, gain, w):
    t = x2d.shape[0]
    return pl.pallas_call(
        _inproj_kernel,
        out_shape=jax.ShapeDtypeStruct((t, PROJ_WIDTH), F32),
        grid=(t // TOKEN_TILE,),
        in_specs=[
            pl.BlockSpec((TOKEN_TILE, D_MODEL), lambda i: (i, 0)),
            _const_spec((1, D_MODEL)),
            _const_spec((D_MODEL, PROJ_WIDTH)),
        ],
        out_specs=pl.BlockSpec((TOKEN_TILE, PROJ_WIDTH), lambda i: (i, 0)),
        compiler_params=_compiler_params(),
        name="inproj",
    )(x2d, gain, w)


def _softmax_unnormalised(s):
    m = jnp.max(s, axis=-1, keepdims=True)
    p = jnp.exp(s - m)
    return p, jnp.sum(p, axis=-1, keepdims=True)


def _proj_spec(width, block):
    return pl.BlockSpec((1, SEQ, width), lambda b: (b, 0, block))


def _na_kernel(q_ref, k_ref, v_ref, tab_ref, o_ref):
    n_groups = GRID_ROWS // NA_GROUP_ROWS
    q_rows = NA_GROUP_ROWS * GRID_W
    band = NA_BAND_ROWS * GRID_W
    scale = HEAD_DIM ** -0.5

    def body(gi, carry):
        band_row = jnp.clip(NA_GROUP_ROWS * gi - NA_WIN_ROWS // 2, 0, GRID_ROWS - NA_BAND_ROWS)
        b0 = pl.multiple_of(band_row * GRID_W, GRID_W)
        q0 = pl.multiple_of(gi * q_rows, q_rows)
        kind = jnp.where(gi == 0, 0, jnp.where(gi == n_groups - 1, 2, 1))
        q = q_ref[0, pl.ds(q0, q_rows), :] * scale
        kb = k_ref[0, pl.ds(b0, band), :].astype(BF16)
        vb = v_ref[0, pl.ds(b0, band), :].astype(BF16)
        outs = []
        for h in range(N_HEADS):
            lanes = slice(h * HEAD_DIM, (h + 1) * HEAD_DIM)
            s = _dot_nt(q[:, lanes].astype(BF16), kb[:, lanes]) + tab_ref[kind, h]
            p, l = _softmax_unnormalised(s)
            outs.append(_dot(p.astype(BF16), vb[:, lanes]) / l)
        o_ref[0, pl.ds(q0, q_rows), :] = jnp.concatenate(outs, axis=-1).astype(o_ref.dtype)
        return carry

    lax.fori_loop(0, n_groups, body, 0)


def _na_tables(rel_bias):
    a = np.arange(NA_GROUP_ROWS)[:, None, None, None]
    c = np.arange(GRID_W)[None, :, None, None]
    i = np.arange(NA_BAND_ROWS)[None, None, :, None]
    kc = np.arange(GRID_W)[None, None, None, :]
    cs = np.clip(c - NA_WIN_COLS // 2, 0, GRID_W - NA_WIN_COLS)
    col_ok = (kc >= cs) & (kc < cs + NA_WIN_COLS)
    dc = np.clip(kc - c + NA_WIN_COLS - 1, 0, 2 * NA_WIN_COLS - 2)
    idxs, masks = [], []
    last_r0 = GRID_ROWS - NA_GROUP_ROWS
    for r0, band_row in ((0, 0), (NA_GROUP_ROWS, 0), (last_r0, GRID_ROWS - NA_BAND_ROWS)):
        r = r0 + a
        rs = np.clip(r - NA_WIN_ROWS // 2, 0, GRID_ROWS - NA_WIN_ROWS)
        key_row = band_row + i
        row_ok = (key_row >= rs) & (key_row < rs + NA_WIN_ROWS)
        dr = np.clip(key_row - r + NA_WIN_ROWS - 1, 0, 2 * NA_WIN_ROWS - 2)
        shape = (NA_GROUP_ROWS, GRID_W, NA_BAND_ROWS, GRID_W)
        idxs.append(np.broadcast_to(dr * (2 * NA_WIN_COLS - 1) + dc, shape).reshape(-1))
        masks.append(np.broadcast_to(row_ok & col_ok, shape).reshape(-1))
    idx = np.stack(idxs).astype(np.int32)
    mask = np.stack(masks)
    flat = rel_bias.reshape(N_HEADS, -1).astype(F32)
    vals = jnp.take(flat, jnp.asarray(idx), axis=1)
    tab = jnp.where(jnp.asarray(mask)[None], vals, NEG_BIG)
    q_rows, band = NA_GROUP_ROWS * GRID_W, NA_BAND_ROWS * GRID_W
    return tab.transpose(1, 0, 2).reshape(3, N_HEADS, q_rows, band)


def _na_attention(proj, tables):
    b = proj.shape[0]
    return pl.pallas_call(
        _na_kernel,
        out_shape=jax.ShapeDtypeStruct((b, SEQ, GROUP_WIDTH), BF16),
        grid=(b,),
        in_specs=[
            _proj_spec(256, B256_NA_Q), _proj_spec(256, B256_NA_K), _proj_spec(256, B256_NA_V),
            _const_spec(tables.shape),
        ],
        out_specs=pl.BlockSpec((1, SEQ, GROUP_WIDTH), lambda i: (i, 0, 0)),
        compiler_params=_compiler_params(),
        name="na_attention",
    )(proj, proj, proj, tables)


def _diff_kernel(lambda_init, q_ref, k_ref, v_ref, lq1_ref, lk1_ref, lq2_ref, lk2_ref, subln_ref, o_ref,
                 qlo_scr, qhi_scr, k_scr, v_scr):
    scale = DIFF_QK_DIM ** -0.5
    lam = (jnp.exp(jnp.sum(lq1_ref[...] * lk1_ref[...], axis=-1, keepdims=True))
           - jnp.exp(jnp.sum(lq2_ref[...] * lk2_ref[...], axis=-1, keepdims=True)) + lambda_init)

    def prep(ci, carry):
        r0 = pl.multiple_of(ci * Q_TILE, Q_TILE)
        rows = pl.ds(r0, Q_TILE)
        q = q_ref[0, rows, :]
        first_map = (lax.broadcasted_iota(jnp.int32, q.shape, 1) % (2 * DIFF_QK_DIM)) < DIFF_QK_DIM
        qlo_scr[rows, :] = jnp.where(first_map, q, 0.0).astype(BF16)
        qhi_scr[rows, :] = jnp.where(first_map, 0.0, q).astype(BF16)
        k_scr[rows, :] = k_ref[0, rows, :].astype(BF16)
        v_scr[rows, :] = v_ref[0, rows, :].astype(BF16)
        return carry

    lax.fori_loop(0, SEQ // Q_TILE, prep, 0)

    def body(qi, carry):
        r0 = pl.multiple_of(qi * Q_TILE, Q_TILE)
        rows = pl.ds(r0, Q_TILE)
        qpos = r0 + lax.broadcasted_iota(jnp.int32, (Q_TILE, SEQ), 0)
        kpos = lax.broadcasted_iota(jnp.int32, (Q_TILE, SEQ), 1)
        dist = jnp.abs(qpos - kpos).astype(F32)
        qlo = qlo_scr[rows, :]
        qhi = qhi_scr[rows, :]
        outs = []
        for h in range(N_HEADS):
            lanes = slice(h * HEAD_DIM, (h + 1) * HEAD_DIM)
            slope = 2.0 ** (-8.0 * (h + 1) / N_HEADS)
            alibi = -slope * dist
            kh = k_scr[:, lanes]
            p1, l1 = _softmax_unnormalised(_dot_nt(qlo[:, lanes], kh) * scale + alibi)
            p2, l2 = _softmax_unnormalised(_dot_nt(qhi[:, lanes], kh) * scale + alibi)
            a = p1 * (1.0 / l1) - p2 * (lam / l2)
            o = _dot(a.astype(BF16), v_scr[:, lanes])
            outs.append(o * _rms(o) * subln_ref[...] * (1.0 - lambda_init))
        o_ref[0, rows, :] = jnp.concatenate(outs, axis=-1).astype(o_ref.dtype)
        return carry

    lax.fori_loop(0, SEQ // Q_TILE, body, 0)


def _diff_attention(proj, lq1, lk1, lq2, lk2, subln, lambda_init):
    b = proj.shape[0]
    vec = _const_spec((1, DIFF_QK_DIM))
    return pl.pallas_call(
        functools.partial(_diff_kernel, lambda_init),
        out_shape=jax.ShapeDtypeStruct((b, SEQ, GROUP_WIDTH), BF16),
        grid=(b,),
        in_specs=[
            _proj_spec(256, B256_DF_Q), _proj_spec(256, B256_DF_K), _proj_spec(256, B256_DF_V),
            vec, vec, vec, vec, _const_spec((1, HEAD_DIM)),
        ],
        out_specs=pl.BlockSpec((1, SEQ, GROUP_WIDTH), lambda i: (i, 0, 0)),
        scratch_shapes=[pltpu.VMEM((SEQ, GROUP_WIDTH), BF16)] * 4,
        compiler_params=_compiler_params(),
        name="diff_attention",
    )(proj, proj, proj, lq1, lk1, lq2, lk2, subln)


def _gqa_kernel(q_ref, qr_ref, k_ref, kr_ref, v_ref, gq_ref, gqp_ref, gk_ref, gkp_ref, cos_ref, sin_ref, o_ref,
                q_scr, k_scr, v_scr):
    scale = HEAD_DIM ** -0.5
    kv_heads = N_HEADS // 2

    def normed_rotary(x, xr, g, gp, cos, sin):
        r = _rms(x)
        return (x * r * g) * cos + (xr * r * gp) * sin

    def prep(ci, carry):
        r0 = pl.multiple_of(ci * Q_TILE, Q_TILE)
        rows = pl.ds(r0, Q_TILE)
        cos, sin = cos_ref[rows, :], sin_ref[rows, :]
        q, qr = q_ref[0, rows, :], qr_ref[0, rows, :]
        k, kr = k_ref[0, rows, :], kr_ref[0, rows, :]
        qs = []
        for h in range(N_HEADS):
            lanes = slice(h * HEAD_DIM, (h + 1) * HEAD_DIM)
            qs.append(normed_rotary(q[:, lanes], qr[:, lanes], gq_ref[...], gqp_ref[...], cos, sin) * scale)
        ks = []
        for h in range(kv_heads):
            lanes = slice(h * HEAD_DIM, (h + 1) * HEAD_DIM)
            ks.append(normed_rotary(k[:, lanes], kr[:, lanes], gk_ref[...], gkp_ref[...], cos, sin))
        q_scr[rows, :] = jnp.concatenate(qs, axis=-1).astype(BF16)
        k_scr[rows, :] = jnp.concatenate(ks, axis=-1).astype(BF16)
        v_scr[rows, :] = v_ref[0, rows, :].astype(BF16)
        return carry

    lax.fori_loop(0, SEQ // Q_TILE, prep, 0)

    def body(qi, carry):
        r0 = pl.multiple_of(qi * Q_TILE, Q_TILE)
        rows = pl.ds(r0, Q_TILE)
        q = q_scr[rows, :]
        outs = []
        for h in range(N_HEADS):
            kv = h // (N_HEADS // kv_heads)
            lanes = slice(h * HEAD_DIM, (h + 1) * HEAD_DIM)
            kv_lanes = slice(kv * HEAD_DIM, (kv + 1) * HEAD_DIM)
            p, l = _softmax_unnormalised(_dot_nt(q[:, lanes], k_scr[:, kv_lanes]))
            outs.append(_dot(p.astype(BF16), v_scr[:, kv_lanes]) / l)
        o_ref[0, rows, :] = jnp.concatenate(outs, axis=-1).astype(o_ref.dtype)
        return carry

    lax.fori_loop(0, SEQ // Q_TILE, body, 0)


def _gqa_attention(proj, gq, gqp, gk, gkp, cos, sin):
    b = proj.shape[0]
    vec = _const_spec((1, HEAD_DIM))
    tab = _const_spec((SEQ, HEAD_DIM))
    return pl.pallas_call(
        _gqa_kernel,
        out_shape=jax.ShapeDtypeStruct((b, SEQ, GROUP_WIDTH), BF16),
        grid=(b,),
        in_specs=[
            _proj_spec(256, B256_GQ_Q), _proj_spec(256, B256_GQ_QR),
            _proj_spec(128, B128_GQ_K), _proj_spec(128, B128_GQ_KR), _proj_spec(128, B128_GQ_V),
            vec, vec, vec, vec, tab, tab,
        ],
        out_specs=pl.BlockSpec((1, SEQ, GROUP_WIDTH), lambda i: (i, 0, 0)),
        scratch_shapes=[pltpu.VMEM((SEQ, GROUP_WIDTH), BF16), pltpu.VMEM((SEQ, 128), BF16),
                        pltpu.VMEM((SEQ, 128), BF16)],
        compiler_params=_compiler_params(),
        name="gqa_attention",
    )(proj, proj, proj, proj, proj, gq, gqp, gk, gkp, cos, sin)


def _mla_kernel(cq_ref, ckv_ref, kr_ref, krr_ref, gq_ref, gkv_ref, wuq_ref, wukv_ref, cos_ref, sin_ref, o_ref,
                q_scr, k_scr, v_scr):
    scale = (MLA_NOPE_DIM + MLA_ROPE_DIM) ** -0.5
    rot0 = N_HEADS * 128
    v0 = N_HEADS * 128

    def prep(ci, carry):
        r0 = pl.multiple_of(ci * Q_TILE, Q_TILE)
        rows = pl.ds(r0, Q_TILE)
        cos, sin = cos_ref[rows, :], sin_ref[rows, :]
        cq = cq_ref[0, rows, :]
        cqn = (cq * _rms(cq) * gq_ref[...]).astype(BF16)
        ckv = ckv_ref[0, rows, :]
        ckvn = (ckv * _rms(ckv) * gkv_ref[...]).astype(BF16)
        k_rope = kr_ref[0, rows, :] * cos + krr_ref[0, rows, :] * sin
        for h in range(N_HEADS):
            cols = slice(h * 128, (h + 1) * 128)
            rot_cols = slice(rot0 + h * 128, rot0 + (h + 1) * 128)
            qh = _dot(cqn, wuq_ref[:, cols]) * cos + _dot(cqn, wuq_ref[:, rot_cols]) * sin
            q_scr[h, rows, :] = qh.astype(BF16)
            k_scr[h, rows, :] = (_dot(ckvn, wukv_ref[:, cols]) + k_rope).astype(BF16)
        v_scr[rows, :] = _dot(ckvn, wukv_ref[:, v0:v0 + GROUP_WIDTH]).astype(BF16)
        return carry

    lax.fori_loop(0, SEQ // Q_TILE, prep, 0)

    def body(qi, carry):
        r0 = pl.multiple_of(qi * Q_TILE, Q_TILE)
        rows = pl.ds(r0, Q_TILE)
        outs = []
        for h in range(N_HEADS):
            lanes = slice(h * HEAD_DIM, (h + 1) * HEAD_DIM)
            p, l = _softmax_unnormalised(_dot_nt(q_scr[h, rows, :], k_scr[h]) * scale)
            outs.append(_dot(p.astype(BF16), v_scr[:, lanes]) / l)
        o_ref[0, rows, :] = jnp.concatenate(outs, axis=-1).astype(o_ref.dtype)
        return carry

    lax.fori_loop(0, SEQ // Q_TILE, body, 0)


def _mla_attention(proj, gq, gkv, wuq, wukv, cos, sin):
    b = proj.shape[0]
    tab = _const_spec((SEQ, 128))
    return pl.pallas_call(
        _mla_kernel,
        out_shape=jax.ShapeDtypeStruct((b, SEQ, GROUP_WIDTH), BF16),
        grid=(b,),
        in_specs=[
            _proj_spec(256, B256_ML_CQ), _proj_spec(128, B128_ML_CKV),
            _proj_spec(128, B128_ML_KR), _proj_spec(128, B128_ML_KRR),
            _const_spec((1, GROUP_WIDTH)), _const_spec((1, MLA_KV_LORA)),
            _const_spec(wuq.shape), _const_spec(wukv.shape), tab, tab,
        ],
        out_specs=pl.BlockSpec((1, SEQ, GROUP_WIDTH), lambda i: (i, 0, 0)),
        scratch_shapes=[pltpu.VMEM((N_HEADS, SEQ, 128), BF16), pltpu.VMEM((N_HEADS, SEQ, 128), BF16),
                        pltpu.VMEM((SEQ, GROUP_WIDTH), BF16)],
        compiler_params=_compiler_params(),
        name="mla_attention",
    )(proj, proj, proj, proj, gq, gkv, wuq, wukv, cos, sin)


def _post_kernel(x_ref, a_ref, b_ref, c_ref, d_ref, wo_ref, g_mix_ref, g_pre_ref, wg_ref, wu_ref, wd_ref,
                 g_ffn_ref, o_ref, acc_ref):
    mix = (_dot(a_ref[...], wo_ref[0]) + _dot(b_ref[...], wo_ref[1])
           + _dot(c_ref[...], wo_ref[2]) + _dot(d_ref[...], wo_ref[3]))
    x = x_ref[...] + mix * _rms(mix) * g_mix_ref[...]
    h = (x * _rms(x) * g_pre_ref[...]).astype(BF16)
    acc_ref[...] = jnp.zeros_like(acc_ref)

    def body(ci, carry):
        gate = _dot(h, wg_ref[ci])
        up = _dot(h, wu_ref[ci])
        act = (gate * jax.nn.sigmoid(gate) * up).astype(BF16)
        acc_ref[...] += _dot(act, wd_ref[ci])
        return carry

    lax.fori_loop(0, D_FF // FF_CHUNK, body, 0)
    f = acc_ref[...]
    o_ref[...] = x + f * _rms(f) * g_ffn_ref[...]


def _post(x2d, a, b, c, d, wo, g_mix, g_pre, wg, wu, wd, g_ffn):
    t = x2d.shape[0]
    tok = lambda w: pl.BlockSpec((TOKEN_TILE, w), lambda i: (i, 0))
    vec = _const_spec((1, D_MODEL))
    return pl.pallas_call(
        _post_kernel,
        out_shape=jax.ShapeDtypeStruct((t, D_MODEL), F32),
        grid=(t // TOKEN_TILE,),
        in_specs=[
            tok(D_MODEL), tok(GROUP_WIDTH), tok(GROUP_WIDTH), tok(GROUP_WIDTH), tok(GROUP_WIDTH),
            _const_spec(wo.shape), vec, vec, _const_spec(wg.shape), _const_spec(wu.shape), _const_spec(wd.shape),
            vec,
        ],
        out_specs=tok(D_MODEL),
        scratch_shapes=[pltpu.VMEM((TOKEN_TILE, D_MODEL), F32)],
        compiler_params=_compiler_params(),
        name="outproj_swiglu",
    )(x2d, a, b, c, d, wo, g_mix, g_pre, wg, wu, wd, g_ffn)


def _layout_w_in(w):
    src64, sign64 = _rot_perm(HEAD_DIM, HEAD_DIM // 2)
    src_q = np.concatenate([h * HEAD_DIM + src64 for h in range(N_HEADS)])
    src_k = src_q[: 2 * HEAD_DIM]
    src32, sign32 = _rot_perm(MLA_ROPE_DIM, MLA_ROPE_DIM)
    gq_q = w[:, O_GQ_Q:O_GQ_K]
    gq_k = w[:, O_GQ_K:O_GQ_V]
    k_rope = w[:, O_ML_KR:O_ML_KR + MLA_ROPE_DIM]
    pad = lambda m: jnp.pad(m, ((0, 0), (MLA_NOPE_DIM, 128 - MLA_NOPE_DIM - MLA_ROPE_DIM)))
    cols = [
        w[:, :O_GQ_K],
        gq_q[:, src_q] * np.tile(sign64, N_HEADS),
        w[:, O_ML_CQ:O_ML_CKV],
        gq_k, gq_k[:, src_k] * np.tile(sign64, 2),
        w[:, O_GQ_V:O_ML_CQ],
        w[:, O_ML_CKV:O_ML_KR],
        pad(k_rope), pad(k_rope[:, src32] * sign32),
    ]
    return jnp.concatenate(cols, axis=1).astype(BF16)


def _layout_mla_weights(w_uq, w_ukv):
    src32, sign32 = _rot_perm(MLA_ROPE_DIM, MLA_ROPE_DIM)
    wq = w_uq.reshape(GROUP_WIDTH, N_HEADS, MLA_NOPE_DIM + MLA_ROPE_DIM)
    rope = wq[:, :, MLA_NOPE_DIM:]
    tail = 128 - MLA_NOPE_DIM - MLA_ROPE_DIM
    q_main = jnp.pad(wq, ((0, 0), (0, 0), (0, tail)))
    q_rot = jnp.pad(rope[:, :, src32] * sign32, ((0, 0), (0, 0), (MLA_NOPE_DIM, tail)))
    wuq = jnp.concatenate([q_main.reshape(GROUP_WIDTH, -1), q_rot.reshape(GROUP_WIDTH, -1)], axis=1)
    wkv = w_ukv.reshape(MLA_KV_LORA, N_HEADS, MLA_NOPE_DIM + HEAD_DIM)
    k_nope = jnp.pad(wkv[:, :, :MLA_NOPE_DIM], ((0, 0), (0, 0), (0, 128 - MLA_NOPE_DIM)))
    vals = wkv[:, :, MLA_NOPE_DIM:]
    wukv = jnp.concatenate([k_nope.reshape(MLA_KV_LORA, -1), vals.reshape(MLA_KV_LORA, -1)], axis=1)
    return wuq.astype(BF16), wukv.astype(BF16)


def _rotary_tables():
    pos = jnp.arange(SEQ)
    half = HEAD_DIM // 2
    inv = ROPE_THETA ** (-jnp.arange(0, half, 2, dtype=F32) / half)

    def angles(p):
        ang = p.astype(F32)[:, None] * inv[None, :]
        return jnp.concatenate([ang, ang], axis=-1)

    axial = jnp.concatenate([angles(pos // GRID_W), angles(pos % GRID_W)], axis=-1)
    seq = angles(pos)
    tail = 128 - MLA_NOPE_DIM - MLA_ROPE_DIM
    mla_cos = jnp.concatenate([jnp.ones((SEQ, MLA_NOPE_DIM), F32), jnp.cos(seq), jnp.zeros((SEQ, tail), F32)], -1)
    mla_sin = jnp.pad(jnp.sin(seq), ((0, 0), (MLA_NOPE_DIM, tail)))
    return jnp.cos(axial), jnp.sin(axial), mla_cos, mla_sin


def kernel(x, pre_mix_norm, w_in, na_rel_bias, diff_lambda_q1, diff_lambda_k1, diff_lambda_q2, diff_lambda_k2,
           diff_subln, gqa_q_norm, gqa_k_norm, mla_q_norm, mla_kv_norm, mla_w_uq, mla_w_ukv, w_o, post_mix_norm,
           pre_ffn_norm, ffn_w_gate_up, ffn_w_down, post_ffn_norm):
    b, s, d = x.shape
    assert (s, d) == (SEQ, D_MODEL)
    depth = w_in.shape[0]
    src64, _ = _rot_perm(HEAD_DIM, HEAD_DIM // 2)
    ax_cos, ax_sin, mla_cos, mla_sin = _rotary_tables()
    row = lambda v: v.reshape(1, -1).astype(F32)
    n_chunks = D_FF // FF_CHUNK

    x2d = x.reshape(b * s, d)
    for l in range(depth):
        lambda_init = 0.8 - 0.6 * math.exp(-0.3 * l)
        proj = _inproj(x2d, row(pre_mix_norm[l]), _layout_w_in(w_in[l])).reshape(b, s, PROJ_WIDTH)

        a_out = _na_attention(proj, _na_tables(na_rel_bias[l]))
        b_out = _diff_attention(proj, row(diff_lambda_q1[l]), row(diff_lambda_k1[l]), row(diff_lambda_q2[l]),
                                row(diff_lambda_k2[l]), row(diff_subln[l]), lambda_init)
        c_out = _gqa_attention(proj, row(gqa_q_norm[l]), row(gqa_q_norm[l][src64]), row(gqa_k_norm[l]),
                               row(gqa_k_norm[l][src64]), ax_cos, ax_sin)
        wuq, wukv = _layout_mla_weights(mla_w_uq[l], mla_w_ukv[l])
        d_out = _mla_attention(proj, row(mla_q_norm[l]), row(mla_kv_norm[l]), wuq, wukv, mla_cos, mla_sin)

        gate_up = ffn_w_gate_up[l].astype(BF16)
        wg = gate_up[:, :D_FF].reshape(d, n_chunks, FF_CHUNK).transpose(1, 0, 2)
        wu = gate_up[:, D_FF:].reshape(d, n_chunks, FF_CHUNK).transpose(1, 0, 2)
        wd = ffn_w_down[l].astype(BF16).reshape(n_chunks, FF_CHUNK, d)
        wo = w_o[l].astype(BF16).reshape(N_HEADS, GROUP_WIDTH, d)
        flat = lambda t: t.reshape(b * s, GROUP_WIDTH)
        x2d = _post(x2d, flat(a_out), flat(b_out), flat(c_out), flat(d_out), wo, row(post_mix_norm[l]),
                    row(pre_ffn_norm[l]), wg, wu, wd, row(post_ffn_norm[l]))
    return x2d.reshape(b, s, d)
```

```python
import functools
import math

import numpy as np
import jax
import jax.numpy as jnp
from jax import lax
from jax.experimental import pallas as pl
from jax.experimental.pallas import tpu as pltpu

F32 = jnp.float32
BF16 = jnp.bfloat16

D_MODEL = 1024
SEQ = 2048
GRID_W = 64
GRID_ROWS = SEQ // GRID_W
HEAD_DIM = 64
N_HEADS = 4
GROUP_WIDTH = 256
EPS = 1e-6
ROPE_THETA = 10000.0

NA_WIN_ROWS = 8
NA_WIN_COLS = 16
DIFF_QK_DIM = 32
MLA_NOPE_DIM = 64
MLA_ROPE_DIM = 32
MLA_KV_LORA = 128
D_FF = 2816

O_GQ_Q, O_GQ_K, O_GQ_V, O_ML_CQ, O_ML_CKV, O_ML_KR = 1536, 1792, 1920, 2048, 2304, 2432

PROJ_WIDTH = 3072
B256_NA_Q, B256_NA_K, B256_NA_V, B256_DF_Q, B256_DF_K, B256_DF_V, B256_GQ_Q, B256_GQ_QR, B256_ML_CQ = range(9)
B128_GQ_K, B128_GQ_KR, B128_GQ_V, B128_ML_CKV, B128_ML_KR, B128_ML_KRR = range(18, 24)

VMEM_LIMIT_BYTES = 56 * 1024 * 1024

TOKEN_TILE = 1024
FF_CHUNK = 256
Q_TILE = 256
NA_GROUP_ROWS = 4
NA_BAND_ROWS = 12
NEG_BIG = -1e30
LOG2E = math.log2(math.e)


def _rot_perm(width, group):
    j = np.arange(width)
    jj = j % group
    half = group // 2
    src = (j // group) * group + (jj + half) % group
    sign = np.where(jj < half, -1.0, 1.0).astype(np.float32)
    return src, sign


def _compiler_params():
    return pltpu.CompilerParams(dimension_semantics=("arbitrary",), vmem_limit_bytes=VMEM_LIMIT_BYTES)


def _const_spec(shape):
    zeros = (0,) * len(shape)
    return pl.BlockSpec(shape, lambda i: zeros, pipeline_mode=pl.Buffered(1))


def _rms(x):
    return lax.rsqrt(jnp.mean(x * x, axis=-1, keepdims=True) + EPS)


def _dot(a, b):
    return jnp.dot(a, b, preferred_element_type=F32)


def _dot_nt(a, b):
    return lax.dot_general(a, b, (((1,), (1,)), ((), ())), preferred_element_type=F32)


def _inproj_kernel(x_ref, g_ref, w_ref, o_ref):
    x = x_ref[...]
    h = (x * _rms(x) * g_ref[...]).astype(BF16)
    o_ref[...] = _dot(h, w_ref[...])


def _inproj(x2d, gain, w):
    t = x2d.shape[0]
    return pl.pallas_call(
        _inproj_kernel,
        out_shape=jax.ShapeDtypeStruct((t, PROJ_WIDTH), F32),
        grid=(t // TOKEN_TILE,),
        in_specs=[
            pl.BlockSpec((TOKEN_TILE, D_MODEL), lambda i: (i, 0)),
            _const_spec((1, D_MODEL)),
            _const_spec((D_MODEL, PROJ_WIDTH)),
        ],
        out_specs=pl.BlockSpec((TOKEN_TILE, PROJ_WIDTH), lambda i: (i, 0)),
        compiler_params=_compiler_params(),
        name="inproj",
    )(x2d, gain, w)


def _softmax_unnormalised(s):
    m = jnp.max(s, axis=-1, keepdims=True)
    p = jnp.exp(s - m)
    return p, jnp.sum(p, axis=-1, keepdims=True)


def _segment_mean_square(x, seg):
    w = x.shape[-1]
    same = (lax.broadcasted_iota(jnp.int32, (w, w), 0) // seg) == (lax.broadcasted_iota(jnp.int32, (w, w), 1) // seg)
    ones = jnp.where(same, 1.0, 0.0).astype(BF16)
    sq = x * x
    hi = sq.astype(BF16)
    lo = (sq - hi.astype(F32)).astype(BF16)
    return (_dot(hi, ones) + _dot(lo, ones)) * (1.0 / seg)


def _ones_row_block(width):
    return jnp.where(lax.broadcasted_iota(jnp.int32, (16, width), 0) == 0, 1.0, 0.0).astype(BF16)


def _softmax_pv_transposed(sts, vts):
    m = functools.reduce(jnp.maximum, sts)
    m = jnp.max(m, axis=0, keepdims=True)
    acc = None
    for st, vt in zip(sts, vts):
        part = _dot(vt, jnp.exp2(st - m).astype(BF16))
        acc = part if acc is None else acc + part
    return acc[:HEAD_DIM] / acc[HEAD_DIM:HEAD_DIM + 1]


def _pipelined_softmax_pv(n_tiles, n_heads, n_chunks, score_chunk, value_chunk, s_scr, o_ref):
    assert n_heads % 2 == 0
    rows = s_scr.shape[1] // n_chunks

    def scores_into(slot, qi, h, c, m8):
        st = score_chunk(qi, h, c)
        s_scr[slot, c * rows:(c + 1) * rows, :] = st
        cm = jnp.max(st.reshape(-1, 8, st.shape[-1]), axis=0)
        return cm if m8 is None else jnp.maximum(m8, cm)

    m8 = None
    for c in range(n_chunks):
        m8 = scores_into(0, 0, 0, c, m8)

    def body(qi, m):
        next_qi = jnp.minimum(qi + 1, n_tiles - 1)
        outs = []
        for h in range(n_heads):
            slot = h % 2
            nq, nh = (qi, h + 1) if h + 1 < n_heads else (next_qi, 0)
            acc, m8 = None, None
            for c in range(n_chunks):
                m8 = scores_into(1 - slot, nq, nh, c, m8)
                p = jnp.exp2(s_scr[slot, c * rows:(c + 1) * rows, :] - m).astype(BF16)
                part = _dot(value_chunk(h, c), p)
                acc = part if acc is None else acc + part
            outs.append(acc[:HEAD_DIM] / acc[HEAD_DIM:HEAD_DIM + 1])
            m = jnp.max(m8, axis=0, keepdims=True)
        r0 = pl.multiple_of(qi * Q_TILE, Q_TILE)
        o_ref[0, pl.ds(r0, Q_TILE), :] = jnp.concatenate(outs, axis=0).T.astype(o_ref.dtype)
        return m

    lax.fori_loop(0, n_tiles, body, jnp.max(m8, axis=0, keepdims=True))


def _proj_spec(width, block):
    return pl.BlockSpec((1, SEQ, width), lambda b: (b, 0, block))


def _na_kernel(q_ref, k_ref, v_ref, tab_ref, o_ref):
    n_groups = GRID_ROWS // NA_GROUP_ROWS
    q_rows = NA_GROUP_ROWS * GRID_W
    band = NA_BAND_ROWS * GRID_W
    scale = HEAD_DIM ** -0.5

    def body(gi, carry):
        band_row = jnp.clip(NA_GROUP_ROWS * gi - NA_WIN_ROWS // 2, 0, GRID_ROWS - NA_BAND_ROWS)
        b0 = pl.multiple_of(band_row * GRID_W, GRID_W)
        q0 = pl.multiple_of(gi * q_rows, q_rows)
        kind = jnp.where(gi == 0, 0, jnp.where(gi == n_groups - 1, 2, 1))
        q = q_ref[0, pl.ds(q0, q_rows), :] * scale
        kb = k_ref[0, pl.ds(b0, band), :].astype(BF16)
        vb = v_ref[0, pl.ds(b0, band), :].astype(BF16)
        outs = []
        for h in range(N_HEADS):
            lanes = slice(h * HEAD_DIM, (h + 1) * HEAD_DIM)
            s = _dot_nt(q[:, lanes].astype(BF16), kb[:, lanes]) + tab_ref[kind, h]
            p, l = _softmax_unnormalised(s)
            outs.append(_dot(p.astype(BF16), vb[:, lanes]) / l)
        o_ref[0, pl.ds(q0, q_rows), :] = jnp.concatenate(outs, axis=-1).astype(o_ref.dtype)
        return carry

    lax.fori_loop(0, n_groups, body, 0)


def _na_tables(rel_bias):
    a = np.arange(NA_GROUP_ROWS)[:, None, None, None]
    c = np.arange(GRID_W)[None, :, None, None]
    i = np.arange(NA_BAND_ROWS)[None, None, :, None]
    kc = np.arange(GRID_W)[None, None, None, :]
    cs = np.clip(c - NA_WIN_COLS // 2, 0, GRID_W - NA_WIN_COLS)
    col_ok = (kc >= cs) & (kc < cs + NA_WIN_COLS)
    dc = kc - c + NA_WIN_COLS - 1
    col_sel = (dc[..., None] == np.arange(2 * NA_WIN_COLS - 1)) & col_ok[..., None]
    row_sels = []
    last_r0 = GRID_ROWS - NA_GROUP_ROWS
    for r0, band_row in ((0, 0), (NA_GROUP_ROWS, 0), (last_r0, GRID_ROWS - NA_BAND_ROWS)):
        r = r0 + a
        rs = np.clip(r - NA_WIN_ROWS // 2, 0, GRID_ROWS - NA_WIN_ROWS)
        key_row = band_row + i
        row_ok = (key_row >= rs) & (key_row < rs + NA_WIN_ROWS)
        dr = key_row - r + NA_WIN_ROWS - 1
        row_sels.append((dr[..., None] == np.arange(2 * NA_WIN_ROWS - 1)) & row_ok[..., None])
    row_sel = np.stack(row_sels)[:, :, 0, :, 0, :].astype(np.float32)
    col_sel = col_sel[0, :, 0, :, :].astype(np.float32)
    vals = jnp.einsum("taiu,huv,ckv->thacik", row_sel, rel_bias.astype(F32), col_sel,
                      precision=lax.Precision.HIGHEST)
    inside = np.einsum("taiu,ckv->tacik", row_sel, col_sel) > 0
    tab = jnp.where(inside[:, None], vals, NEG_BIG)
    q_rows, band = NA_GROUP_ROWS * GRID_W, NA_BAND_ROWS * GRID_W
    return tab.reshape(3, N_HEADS, q_rows, band)


def _na_attention(proj, tables):
    b = proj.shape[0]
    return pl.pallas_call(
        _na_kernel,
        out_shape=jax.ShapeDtypeStruct((b, SEQ, GROUP_WIDTH), BF16),
        grid=(b,),
        in_specs=[
            _proj_spec(256, B256_NA_Q), _proj_spec(256, B256_NA_K), _proj_spec(256, B256_NA_V),
            _const_spec(tables.shape),
        ],
        out_specs=pl.BlockSpec((1, SEQ, GROUP_WIDTH), lambda i: (i, 0, 0)),
        compiler_params=_compiler_params(),
        name="na_attention",
    )(proj, proj, proj, tables)


def _diff_kernel(lambda_init, q_ref, k_ref, v_ref, lq1_ref, lk1_ref, lq2_ref, lk2_ref, subln_ref, o_ref,
                 qlo_scr, qhi_scr, k_scr, v_scr):
    scale = DIFF_QK_DIM ** -0.5
    lam = (jnp.exp(jnp.sum(lq1_ref[...] * lk1_ref[...], axis=-1, keepdims=True))
           - jnp.exp(jnp.sum(lq2_ref[...] * lk2_ref[...], axis=-1, keepdims=True)) + lambda_init)

    def prep(ci, carry):
        r0 = pl.multiple_of(ci * Q_TILE, Q_TILE)
        rows = pl.ds(r0, Q_TILE)
        q = q_ref[0, rows, :]
        first_map = (lax.broadcasted_iota(jnp.int32, q.shape, 1) % (2 * DIFF_QK_DIM)) < DIFF_QK_DIM
        qlo_scr[rows, :] = jnp.where(first_map, q, 0.0).astype(BF16)
        qhi_scr[rows, :] = jnp.where(first_map, 0.0, q).astype(BF16)
        k_scr[rows, :] = k_ref[0, rows, :].astype(BF16)
        v_scr[rows, :] = v_ref[0, rows, :].astype(BF16)
        return carry

    lax.fori_loop(0, SEQ // Q_TILE, prep, 0)

    def body(qi, carry):
        r0 = pl.multiple_of(qi * Q_TILE, Q_TILE)
        rows = pl.ds(r0, Q_TILE)
        qpos = r0 + lax.broadcasted_iota(jnp.int32, (Q_TILE, SEQ), 0)
        kpos = lax.broadcasted_iota(jnp.int32, (Q_TILE, SEQ), 1)
        dist = jnp.abs(qpos - kpos).astype(F32)
        qlo = qlo_scr[rows, :]
        qhi = qhi_scr[rows, :]
        outs = []
        for h in range(N_HEADS):
            lanes = slice(h * HEAD_DIM, (h + 1) * HEAD_DIM)
            slope = 2.0 ** (-8.0 * (h + 1) / N_HEADS)
            alibi = -slope * dist
            kh = k_scr[:, lanes]
            p1, l1 = _softmax_unnormalised(_dot_nt(qlo[:, lanes], kh) * scale + alibi)
            p2, l2 = _softmax_unnormalised(_dot_nt(qhi[:, lanes], kh) * scale + alibi)
            a = p1 * (1.0 / l1) - p2 * (lam / l2)
            o = _dot(a.astype(BF16), v_scr[:, lanes])
            outs.append(o * _rms(o) * subln_ref[...] * (1.0 - lambda_init))
        o_ref[0, rows, :] = jnp.concatenate(outs, axis=-1).astype(o_ref.dtype)
        return carry

    lax.fori_loop(0, SEQ // Q_TILE, body, 0)


def _diff_attention(proj, lq1, lk1, lq2, lk2, subln, lambda_init):
    b = proj.shape[0]
    vec = _const_spec((1, DIFF_QK_DIM))
    return pl.pallas_call(
        functools.partial(_diff_kernel, lambda_init),
        out_shape=jax.ShapeDtypeStruct((b, SEQ, GROUP_WIDTH), BF16),
        grid=(b,),
        in_specs=[
            _proj_spec(256, B256_DF_Q), _proj_spec(256, B256_DF_K), _proj_spec(256, B256_DF_V),
            vec, vec, vec, vec, _const_spec((1, HEAD_DIM)),
        ],
        out_specs=pl.BlockSpec((1, SEQ, GROUP_WIDTH), lambda i: (i, 0, 0)),
        scratch_shapes=[pltpu.VMEM((SEQ, GROUP_WIDTH), BF16)] * 4,
        compiler_params=_compiler_params(),
        name="diff_attention",
    )(proj, proj, proj, lq1, lk1, lq2, lk2, subln)


def _gqa_kernel(q_ref, qr_ref, k_ref, kr_ref, v_ref, gq_ref, gqp_ref, gk_ref, gkp_ref, cos_ref, sin_ref, o_ref,
                qt_scr, k_scr, vt_scr, s_scr):
    q_scale = HEAD_DIM ** -0.5 * LOG2E
    kv_heads = N_HEADS // 2
    n_tiles = SEQ // Q_TILE
    pair = 2 * HEAD_DIM

    def normed_rotary(x, xr, g, gp, cos, sin):
        r = lax.rsqrt(_segment_mean_square(x, HEAD_DIM) + EPS)
        return (x * r * g) * cos + (xr * r * gp) * sin

    def prep(ci, carry):
        r0 = pl.multiple_of(ci * Q_TILE, Q_TILE)
        rows = pl.ds(r0, Q_TILE)
        cos, sin = cos_ref[rows, :], sin_ref[rows, :]
        zeros = jnp.zeros((HEAD_DIM, Q_TILE), BF16)
        for half in range(2):
            lanes = slice(half * pair, (half + 1) * pair)
            y = normed_rotary(q_ref[0, rows, lanes], qr_ref[0, rows, lanes], gq_ref[...], gqp_ref[...], cos, sin)
            yt = (y * q_scale).T.astype(BF16)
            for j in range(2):
                h = 2 * half + j
                kv = h // (N_HEADS // kv_heads)
                for part in range(kv_heads):
                    block = yt[j * HEAD_DIM:(j + 1) * HEAD_DIM] if part == kv else zeros
                    qt_scr[ci, h * pair + part * HEAD_DIM:h * pair + (part + 1) * HEAD_DIM, :] = block
        k_scr[rows, :] = normed_rotary(k_ref[0, rows, :], kr_ref[0, rows, :], gk_ref[...], gkp_ref[...], cos,
                                       sin).astype(BF16)
        vt = v_ref[0, rows, :].T.astype(BF16)
        for kv in range(kv_heads):
            vt_scr[kv, ci, 0:HEAD_DIM, :] = vt[kv * HEAD_DIM:(kv + 1) * HEAD_DIM]
            vt_scr[kv, ci, HEAD_DIM:, :] = _ones_row_block(Q_TILE)
        return carry

    lax.fori_loop(0, n_tiles, prep, 0)

    def score_chunk(qi, h, c):
        return _dot(k_scr[c * Q_TILE:(c + 1) * Q_TILE, :], qt_scr[qi, h * pair:(h + 1) * pair, :])

    def value_chunk(h, c):
        return vt_scr[h // (N_HEADS // kv_heads), c]

    _pipelined_softmax_pv(n_tiles, N_HEADS, n_tiles, score_chunk, value_chunk, s_scr, o_ref)


def _gqa_attention(proj, gq, gqp, gk, gkp, cos, sin):
    b = proj.shape[0]
    vec = _const_spec((1, 2 * HEAD_DIM))
    tab = _const_spec((SEQ, 2 * HEAD_DIM))
    return pl.pallas_call(
        _gqa_kernel,
        out_shape=jax.ShapeDtypeStruct((b, SEQ, GROUP_WIDTH), BF16),
        grid=(b,),
        in_specs=[
            _proj_spec(256, B256_GQ_Q), _proj_spec(256, B256_GQ_QR),
            _proj_spec(128, B128_GQ_K), _proj_spec(128, B128_GQ_KR), _proj_spec(128, B128_GQ_V),
            vec, vec, vec, vec, tab, tab,
        ],
        out_specs=pl.BlockSpec((1, SEQ, GROUP_WIDTH), lambda i: (i, 0, 0)),
        scratch_shapes=[pltpu.VMEM((SEQ // Q_TILE, N_HEADS * 2 * HEAD_DIM, Q_TILE), BF16),
                        pltpu.VMEM((SEQ, 2 * HEAD_DIM), BF16),
                        pltpu.VMEM((N_HEADS // 2, SEQ // Q_TILE, HEAD_DIM + 16, Q_TILE), BF16),
                        pltpu.VMEM((2, SEQ, Q_TILE), F32)],
        compiler_params=_compiler_params(),
        name="gqa_attention",
    )(proj, proj, proj, proj, proj, gq, gqp, gk, gkp, cos, sin)


def _mla_kernel(cq_ref, ckv_ref, kr_ref, krr_ref, gq_ref, gkv_ref, wuq_ref, wukv_ref, cos_ref, sin_ref, o_ref,
                qt_scr, k_scr, vt_scr, s_scr):
    q_scale = (MLA_NOPE_DIM + MLA_ROPE_DIM) ** -0.5 * LOG2E
    n_tiles = SEQ // Q_TILE
    rot0 = N_HEADS * 128
    v0 = N_HEADS * 128

    def prep(ci, carry):
        r0 = pl.multiple_of(ci * Q_TILE, Q_TILE)
        rows = pl.ds(r0, Q_TILE)
        cos, sin = cos_ref[rows, :], sin_ref[rows, :]
        cq = cq_ref[0, rows, :]
        cqn = (cq * _rms(cq) * gq_ref[...]).astype(BF16)
        ckv = ckv_ref[0, rows, :]
        ckvn = (ckv * _rms(ckv) * gkv_ref[...]).astype(BF16)
        k_rope = kr_ref[0, rows, :] * cos + krr_ref[0, rows, :] * sin
        for h in range(N_HEADS):
            cols = slice(h * 128, (h + 1) * 128)
            rot_cols = slice(rot0 + h * 128, rot0 + (h + 1) * 128)
            qh = _dot(cqn, wuq_ref[:, cols]) * cos + _dot(cqn, wuq_ref[:, rot_cols]) * sin
            qt_scr[ci, h * 128:(h + 1) * 128, :] = (qh * q_scale).T.astype(BF16)
            k_scr[h, rows, :] = (_dot(ckvn, wukv_ref[:, cols]) + k_rope).astype(BF16)
        vt = _dot(ckvn, wukv_ref[:, v0:v0 + GROUP_WIDTH]).T.astype(BF16)
        for h in range(N_HEADS):
            vt_scr[h, ci, 0:HEAD_DIM, :] = vt[h * HEAD_DIM:(h + 1) * HEAD_DIM]
            vt_scr[h, ci, HEAD_DIM:, :] = _ones_row_block(Q_TILE)
        return carry

    lax.fori_loop(0, n_tiles, prep, 0)

    def score_chunk(qi, h, c):
        return _dot(k_scr[h, c * Q_TILE:(c + 1) * Q_TILE, :], qt_scr[qi, h * 128:(h + 1) * 128, :])

    def value_chunk(h, c):
        return vt_scr[h, c]

    _pipelined_softmax_pv(n_tiles, N_HEADS, n_tiles, score_chunk, value_chunk, s_scr, o_ref)


def _mla_attention(proj, gq, gkv, wuq, wukv, cos, sin):
    b = proj.shape[0]
    tab = _const_spec((SEQ, 128))
    return pl.pallas_call(
        _mla_kernel,
        out_shape=jax.ShapeDtypeStruct((b, SEQ, GROUP_WIDTH), BF16),
        grid=(b,),
        in_specs=[
            _proj_spec(256, B256_ML_CQ), _proj_spec(128, B128_ML_CKV),
            _proj_spec(128, B128_ML_KR), _proj_spec(128, B128_ML_KRR),
            _const_spec((1, GROUP_WIDTH)), _const_spec((1, MLA_KV_LORA)),
            _const_spec(wuq.shape), _const_spec(wukv.shape), tab, tab,
        ],
        out_specs=pl.BlockSpec((1, SEQ, GROUP_WIDTH), lambda i: (i, 0, 0)),
        scratch_shapes=[pltpu.VMEM((SEQ // Q_TILE, N_HEADS * 128, Q_TILE), BF16),
                        pltpu.VMEM((N_HEADS, SEQ, 128), BF16),
                        pltpu.VMEM((N_HEADS, SEQ // Q_TILE, HEAD_DIM + 16, Q_TILE), BF16),
                        pltpu.VMEM((2, SEQ, Q_TILE), F32)],
        compiler_params=_compiler_params(),
        name="mla_attention",
    )(proj, proj, proj, proj, gq, gkv, wuq, wukv, cos, sin)


def _post_kernel(x_ref, a_ref, b_ref, c_ref, d_ref, wo_ref, g_mix_ref, g_pre_ref, wg_ref, wu_ref, wd_ref,
                 g_ffn_ref, o_ref, acc_ref):
    mix = (_dot(a_ref[...], wo_ref[0]) + _dot(b_ref[...], wo_ref[1])
           + _dot(c_ref[...], wo_ref[2]) + _dot(d_ref[...], wo_ref[3]))
    x = x_ref[...] + mix * _rms(mix) * g_mix_ref[...]
    h = (x * _rms(x) * g_pre_ref[...]).astype(BF16)
    acc_ref[...] = jnp.zeros_like(acc_ref)

    def body(ci, carry):
        gate = _dot(h, wg_ref[ci])
        up = _dot(h, wu_ref[ci])
        act = (gate * jax.nn.sigmoid(gate) * up).astype(BF16)
        acc_ref[...] += _dot(act, wd_ref[ci])
        return carry

    lax.fori_loop(0, D_FF // FF_CHUNK, body, 0)
    f = acc_ref[...]
    o_ref[...] = x + f * _rms(f) * g_ffn_ref[...]


def _post(x2d, a, b, c, d, wo, g_mix, g_pre, wg, wu, wd, g_ffn):
    t = x2d.shape[0]
    tok = lambda w: pl.BlockSpec((TOKEN_TILE, w), lambda i: (i, 0))
    vec = _const_spec((1, D_MODEL))
    return pl.pallas_call(
        _post_kernel,
        out_shape=jax.ShapeDtypeStruct((t, D_MODEL), F32),
        grid=(t // TOKEN_TILE,),
        in_specs=[
            tok(D_MODEL), tok(GROUP_WIDTH), tok(GROUP_WIDTH), tok(GROUP_WIDTH), tok(GROUP_WIDTH),
            _const_spec(wo.shape), vec, vec, _const_spec(wg.shape), _const_spec(wu.shape), _const_spec(wd.shape),
            vec,
        ],
        out_specs=tok(D_MODEL),
        scratch_shapes=[pltpu.VMEM((TOKEN_TILE, D_MODEL), F32)],
        compiler_params=_compiler_params(),
        name="outproj_swiglu",
    )(x2d, a, b, c, d, wo, g_mix, g_pre, wg, wu, wd, g_ffn)


def _layout_w_in(w):
    src64, sign64 = _rot_perm(HEAD_DIM, HEAD_DIM // 2)
    src_q = np.concatenate([h * HEAD_DIM + src64 for h in range(N_HEADS)])
    src_k = src_q[: 2 * HEAD_DIM]
    src32, sign32 = _rot_perm(MLA_ROPE_DIM, MLA_ROPE_DIM)
    gq_q = w[:, O_GQ_Q:O_GQ_K]
    gq_k = w[:, O_GQ_K:O_GQ_V]
    k_rope = w[:, O_ML_KR:O_ML_KR + MLA_ROPE_DIM]
    pad = lambda m: jnp.pad(m, ((0, 0), (MLA_NOPE_DIM, 128 - MLA_NOPE_DIM - MLA_ROPE_DIM)))
    cols = [
        w[:, :O_GQ_K],
        gq_q[:, src_q] * np.tile(sign64, N_HEADS),
        w[:, O_ML_CQ:O_ML_CKV],
        gq_k, gq_k[:, src_k] * np.tile(sign64, 2),
        w[:, O_GQ_V:O_ML_CQ],
        w[:, O_ML_CKV:O_ML_KR],
        pad(k_rope), pad(k_rope[:, src32] * sign32),
    ]
    return jnp.concatenate(cols, axis=1).astype(BF16)


def _layout_mla_weights(w_uq, w_ukv):
    src32, sign32 = _rot_perm(MLA_ROPE_DIM, MLA_ROPE_DIM)
    wq = w_uq.reshape(GROUP_WIDTH, N_HEADS, MLA_NOPE_DIM + MLA_ROPE_DIM)
    rope = wq[:, :, MLA_NOPE_DIM:]
    tail = 128 - MLA_NOPE_DIM - MLA_ROPE_DIM
    q_main = jnp.pad(wq, ((0, 0), (0, 0), (0, tail)))
    q_rot = jnp.pad(rope[:, :, src32] * sign32, ((0, 0), (0, 0), (MLA_NOPE_DIM, tail)))
    wuq = jnp.concatenate([q_main.reshape(GROUP_WIDTH, -1), q_rot.reshape(GROUP_WIDTH, -1)], axis=1)
    wkv = w_ukv.reshape(MLA_KV_LORA, N_HEADS, MLA_NOPE_DIM + HEAD_DIM)
    k_nope = jnp.pad(wkv[:, :, :MLA_NOPE_DIM], ((0, 0), (0, 0), (0, 128 - MLA_NOPE_DIM)))
    vals = wkv[:, :, MLA_NOPE_DIM:]
    wukv = jnp.concatenate([k_nope.reshape(MLA_KV_LORA, -1), vals.reshape(MLA_KV_LORA, -1)], axis=1)
    return wuq.astype(BF16), wukv.astype(BF16)


def _rotary_tables():
    pos = jnp.arange(SEQ)
    half = HEAD_DIM // 2
    inv = ROPE_THETA ** (-jnp.arange(0, half, 2, dtype=F32) / half)

    def angles(p):
        ang = p.astype(F32)[:, None] * inv[None, :]
        return jnp.concatenate([ang, ang], axis=-1)

    axial = jnp.concatenate([angles(pos // GRID_W), angles(pos % GRID_W)], axis=-1)
    seq = angles(pos)
    tail = 128 - MLA_NOPE_DIM - MLA_ROPE_DIM
    mla_cos = jnp.concatenate([jnp.ones((SEQ, MLA_NOPE_DIM), F32), jnp.cos(seq), jnp.zeros((SEQ, tail), F32)], -1)
    mla_sin = jnp.pad(jnp.sin(seq), ((0, 0), (MLA_NOPE_DIM, tail)))
    pair = lambda t: jnp.tile(t, (1, 2))
    return pair(jnp.cos(axial)), pair(jnp.sin(axial)), mla_cos, mla_sin


def kernel(x, pre_mix_norm, w_in, na_rel_bias, diff_lambda_q1, diff_lambda_k1, diff_lambda_q2, diff_lambda_k2,
           diff_subln, gqa_q_norm, gqa_k_norm, mla_q_norm, mla_kv_norm, mla_w_uq, mla_w_ukv, w_o, post_mix_norm,
           pre_ffn_norm, ffn_w_gate_up, ffn_w_down, post_ffn_norm):
    b, s, d = x.shape
    assert (s, d) == (SEQ, D_MODEL)
    depth = w_in.shape[0]
    src64, _ = _rot_perm(HEAD_DIM, HEAD_DIM // 2)
    ax_cos, ax_sin, mla_cos, mla_sin = _rotary_tables()
    row = lambda v: v.reshape(1, -1).astype(F32)
    n_chunks = D_FF // FF_CHUNK

    x2d = x.reshape(b * s, d)
    for l in range(depth):
        lambda_init = 0.8 - 0.6 * math.exp(-0.3 * l)
        proj = _inproj(x2d, row(pre_mix_norm[l]), _layout_w_in(w_in[l])).reshape(b, s, PROJ_WIDTH)

        a_out = _na_attention(proj, _na_tables(na_rel_bias[l]))
        b_out = _diff_attention(proj, row(diff_lambda_q1[l]), row(diff_lambda_k1[l]), row(diff_lambda_q2[l]),
                                row(diff_lambda_k2[l]), row(diff_subln[l]), lambda_init)
        pair_row = lambda v: row(jnp.tile(v, 2))
        c_out = _gqa_attention(proj, pair_row(gqa_q_norm[l]), pair_row(gqa_q_norm[l][src64]),
                               pair_row(gqa_k_norm[l]), pair_row(gqa_k_norm[l][src64]), ax_cos, ax_sin)
        wuq, wukv = _layout_mla_weights(mla_w_uq[l], mla_w_ukv[l])
        d_out = _mla_attention(proj, row(mla_q_norm[l]), row(mla_kv_norm[l]), wuq, wukv, mla_cos, mla_sin)

        gate_up = ffn_w_gate_up[l].astype(BF16)
        wg = gate_up[:, :D_FF].reshape(d, n_chunks, FF_CHUNK).transpose(1, 0, 2)
        wu = gate_up[:, D_FF:].reshape(d, n_chunks, FF_CHUNK).transpose(1, 0, 2)
        wd = ffn_w_down[l].astype(BF16).reshape(n_chunks, FF_CHUNK, d)
        wo = w_o[l].astype(BF16).reshape(N_HEADS, GROUP_WIDTH, d)
        flat = lambda t: t.reshape(b * s, GROUP_WIDTH)
        x2d = _post(x2d, flat(a_out), flat(b_out), flat(c_out), flat(d_out), wo, row(post_mix_norm[l]),
                    row(pre_ffn_norm[l]), wg, wu, wd, row(post_ffn_norm[l]))
    return x2d.reshape(b, s, d)
```

```python
import functools
import math

import numpy as np
import jax
import jax.numpy as jnp
from jax import lax
from jax.experimental import pallas as pl
from jax.experimental.pallas import tpu as pltpu

F32 = jnp.float32
BF16 = jnp.bfloat16

D_MODEL = 1024
SEQ = 2048
GRID_W = 64
GRID_ROWS = SEQ // GRID_W
HEAD_DIM = 64
N_HEADS = 4
GROUP_WIDTH = 256
EPS = 1e-6
ROPE_THETA = 10000.0

NA_WIN_ROWS = 8
NA_WIN_COLS = 16
DIFF_QK_DIM = 32
POS_SPLIT = 64
MLA_NOPE_DIM = 64
MLA_ROPE_DIM = 32
MLA_KV_LORA = 128
D_FF = 2816

O_GQ_Q, O_GQ_K, O_GQ_V, O_ML_CQ, O_ML_CKV, O_ML_KR = 1536, 1792, 1920, 2048, 2304, 2432

PROJ_WIDTH = 3072
B256_NA_Q, B256_NA_K, B256_NA_V, B256_DF_Q, B256_DF_K, B256_DF_V, B256_GQ_Q, B256_GQ_QR, B256_ML_CQ = range(9)
B128_GQ_K, B128_GQ_KR, B128_GQ_V, B128_ML_CKV, B128_ML_KR, B128_ML_KRR = range(18, 24)

VMEM_LIMIT_BYTES = 56 * 1024 * 1024

TOKEN_TILE = 1024
FF_CHUNK = 256
Q_TILE = 256
NA_GROUP_ROWS = 4
NA_BAND_ROWS = 12
NEG_BIG = -1e30
LOG2E = math.log2(math.e)


def _rot_perm(width, group):
    j = np.arange(width)
    jj = j % group
    half = group // 2
    src = (j // group) * group + (jj + half) % group
    sign = np.where(jj < half, -1.0, 1.0).astype(np.float32)
    return src, sign


def _compiler_params():
    return pltpu.CompilerParams(dimension_semantics=("arbitrary",), vmem_limit_bytes=VMEM_LIMIT_BYTES)


def _const_spec(shape):
    zeros = (0,) * len(shape)
    return pl.BlockSpec(shape, lambda i: zeros, pipeline_mode=pl.Buffered(1))


def _rms(x):
    return lax.rsqrt(jnp.mean(x * x, axis=-1, keepdims=True) + EPS)


def _dot(a, b):
    return jnp.dot(a, b, preferred_element_type=F32)


def _dot_nt(a, b):
    return lax.dot_general(a, b, (((1,), (1,)), ((), ())), preferred_element_type=F32)


def _inproj_kernel(x_ref, g_ref, w_ref, o_ref):
    x = x_ref[...]
    h = (x * _rms(x) * g_ref[...]).astype(BF16)
    o_ref[...] = _dot(h, w_ref[...])


def _inproj(x2d, gain, w):
    t = x2d.shape[0]
    return pl.pallas_call(
        _inproj_kernel,
        out_shape=jax.ShapeDtypeStruct((t, PROJ_WIDTH), F32),
        grid=(t // TOKEN_TILE,),
        in_specs=[
            pl.BlockSpec((TOKEN_TILE, D_MODEL), lambda i: (i, 0)),
            _const_spec((1, D_MODEL)),
            _const_spec((D_MODEL, PROJ_WIDTH)),
        ],
        out_specs=pl.BlockSpec((TOKEN_TILE, PROJ_WIDTH), lambda i: (i, 0)),
        compiler_params=_compiler_params(),
        name="inproj",
    )(x2d, gain, w)


def _softmax_unnormalised(s):
    m = jnp.max(s, axis=-1, keepdims=True)
    p = jnp.exp(s - m)
    return p, jnp.sum(p, axis=-1, keepdims=True)


def _segment_mean_square(x, seg):
    w = x.shape[-1]
    same = (lax.broadcasted_iota(jnp.int32, (w, w), 0) // seg) == (lax.broadcasted_iota(jnp.int32, (w, w), 1) // seg)
    ones = jnp.where(same, 1.0, 0.0).astype(BF16)
    sq = x * x
    hi = sq.astype(BF16)
    lo = (sq - hi.astype(F32)).astype(BF16)
    return (_dot(hi, ones) + _dot(lo, ones)) * (1.0 / seg)


def _ones_row_block(width):
    return jnp.where(lax.broadcasted_iota(jnp.int32, (16, width), 0) == 0, 1.0, 0.0).astype(BF16)


def _softmax_pv_transposed(sts, vts):
    m = functools.reduce(jnp.maximum, sts)
    m = jnp.max(m, axis=0, keepdims=True)
    acc = None
    for st, vt in zip(sts, vts):
        part = _dot(vt, jnp.exp2(st - m).astype(BF16))
        acc = part if acc is None else acc + part
    return acc[:HEAD_DIM] / acc[HEAD_DIM:HEAD_DIM + 1]


def _normalised_heads(accs):
    return jnp.concatenate([acc[:HEAD_DIM] / acc[HEAD_DIM:HEAD_DIM + 1] for acc in accs], axis=0)


def _pipelined_softmax_pv(n_tiles, n_heads, n_chunks, score_chunk, value_chunk, combine, s_scr, o_ref,
                          exp_fn=jnp.exp2):
    assert n_heads % 2 == 0
    rows = s_scr.shape[1] // n_chunks

    def scores_into(slot, qi, h, c, m8):
        st = score_chunk(qi, h, c)
        s_scr[slot, c * rows:(c + 1) * rows, :] = st
        cm = jnp.max(st.reshape(-1, 8, st.shape[-1]), axis=0)
        return cm if m8 is None else jnp.maximum(m8, cm)

    m8 = None
    for c in range(n_chunks):
        m8 = scores_into(0, 0, 0, c, m8)

    def body(qi, m):
        next_qi = jnp.minimum(qi + 1, n_tiles - 1)
        accs = []
        for h in range(n_heads):
            slot = h % 2
            nq, nh = (qi, h + 1) if h + 1 < n_heads else (next_qi, 0)
            acc, m8 = None, None
            for c in range(n_chunks):
                m8 = scores_into(1 - slot, nq, nh, c, m8)
                p = exp_fn(s_scr[slot, c * rows:(c + 1) * rows, :] - m).astype(BF16)
                part = _dot(value_chunk(qi, h, c), p)
                acc = part if acc is None else acc + part
            accs.append(acc)
            m = jnp.max(m8, axis=0, keepdims=True)
        r0 = pl.multiple_of(qi * Q_TILE, Q_TILE)
        o_ref[0, pl.ds(r0, Q_TILE), :] = combine(accs).T.astype(o_ref.dtype)
        return m

    lax.fori_loop(0, n_tiles, body, jnp.max(m8, axis=0, keepdims=True))


def _proj_spec(width, block):
    return pl.BlockSpec((1, SEQ, width), lambda b: (b, 0, block))


def _na_kernel(q_ref, k_ref, v_ref, tab_ref, o_ref):
    n_groups = GRID_ROWS // NA_GROUP_ROWS
    q_rows = NA_GROUP_ROWS * GRID_W
    band = NA_BAND_ROWS * GRID_W
    scale = HEAD_DIM ** -0.5

    def body(gi, carry):
        band_row = jnp.clip(NA_GROUP_ROWS * gi - NA_WIN_ROWS // 2, 0, GRID_ROWS - NA_BAND_ROWS)
        b0 = pl.multiple_of(band_row * GRID_W, GRID_W)
        q0 = pl.multiple_of(gi * q_rows, q_rows)
        kind = jnp.where(gi == 0, 0, jnp.where(gi == n_groups - 1, 2, 1))
        q = q_ref[0, pl.ds(q0, q_rows), :] * scale
        kb = k_ref[0, pl.ds(b0, band), :].astype(BF16)
        vb = v_ref[0, pl.ds(b0, band), :].astype(BF16)
        outs = []
        for h in range(N_HEADS):
            lanes = slice(h * HEAD_DIM, (h + 1) * HEAD_DIM)
            s = _dot_nt(q[:, lanes].astype(BF16), kb[:, lanes]) + tab_ref[kind, h]
            p, l = _softmax_unnormalised(s)
            outs.append(_dot(p.astype(BF16), vb[:, lanes]) / l)
        o_ref[0, pl.ds(q0, q_rows), :] = jnp.concatenate(outs, axis=-1).astype(o_ref.dtype)
        return carry

    lax.fori_loop(0, n_groups, body, 0)


def _na_tables(rel_bias):
    a = np.arange(NA_GROUP_ROWS)[:, None, None, None]
    c = np.arange(GRID_W)[None, :, None, None]
    i = np.arange(NA_BAND_ROWS)[None, None, :, None]
    kc = np.arange(GRID_W)[None, None, None, :]
    cs = np.clip(c - NA_WIN_COLS // 2, 0, GRID_W - NA_WIN_COLS)
    col_ok = (kc >= cs) & (kc < cs + NA_WIN_COLS)
    dc = kc - c + NA_WIN_COLS - 1
    col_sel = (dc[..., None] == np.arange(2 * NA_WIN_COLS - 1)) & col_ok[..., None]
    row_sels = []
    last_r0 = GRID_ROWS - NA_GROUP_ROWS
    for r0, band_row in ((0, 0), (NA_GROUP_ROWS, 0), (last_r0, GRID_ROWS - NA_BAND_ROWS)):
        r = r0 + a
        rs = np.clip(r - NA_WIN_ROWS // 2, 0, GRID_ROWS - NA_WIN_ROWS)
        key_row = band_row + i
        row_ok = (key_row >= rs) & (key_row < rs + NA_WIN_ROWS)
        dr = key_row - r + NA_WIN_ROWS - 1
        row_sels.append((dr[..., None] == np.arange(2 * NA_WIN_ROWS - 1)) & row_ok[..., None])
    row_sel = np.stack(row_sels)[:, :, 0, :, 0, :].astype(np.float32)
    col_sel = col_sel[0, :, 0, :, :].astype(np.float32)
    vals = jnp.einsum("taiu,huv,ckv->thacik", row_sel, rel_bias.astype(F32), col_sel,
                      precision=lax.Precision.HIGHEST)
    inside = np.einsum("taiu,ckv->tacik", row_sel, col_sel) > 0
    tab = jnp.where(inside[:, None], vals, NEG_BIG)
    q_rows, band = NA_GROUP_ROWS * GRID_W, NA_BAND_ROWS * GRID_W
    return tab.reshape(3, N_HEADS, q_rows, band)


def _na_attention(proj, tables):
    b = proj.shape[0]
    return pl.pallas_call(
        _na_kernel,
        out_shape=jax.ShapeDtypeStruct((b, SEQ, GROUP_WIDTH), BF16),
        grid=(b,),
        in_specs=[
            _proj_spec(256, B256_NA_Q), _proj_spec(256, B256_NA_K), _proj_spec(256, B256_NA_V),
            _const_spec(tables.shape),
        ],
        out_specs=pl.BlockSpec((1, SEQ, GROUP_WIDTH), lambda i: (i, 0, 0)),
        compiler_params=_compiler_params(),
        name="na_attention",
    )(proj, proj, proj, tables)


def _alibi_slope(h):
    return 2.0 ** (-8.0 * (h + 1) / N_HEADS)


def _diff_kernel(lambda_init, q_ref, k_ref, v_ref, lq1_ref, lk1_ref, lq2_ref, lk2_ref, subln_ref, diag_ref, o_ref,
                 qt_scr, qf_scr, k_scr, vt_scr, s_scr):
    scale = DIFF_QK_DIM ** -0.5
    n_tiles = SEQ // Q_TILE
    pair = 2 * HEAD_DIM
    n_feat = 16
    lam = (jnp.exp(jnp.sum(lq1_ref[...] * lk1_ref[...], axis=-1, keepdims=True))
           - jnp.exp(jnp.sum(lq2_ref[...] * lk2_ref[...], axis=-1, keepdims=True)) + lambda_init)

    def prep(ci, carry):
        r0 = pl.multiple_of(ci * Q_TILE, Q_TILE)
        rows = pl.ds(r0, Q_TILE)
        qt = (q_ref[0, rows, :] * scale).T
        row = lax.broadcasted_iota(jnp.int32, (pair, Q_TILE), 0)
        feat_row = lax.broadcasted_iota(jnp.int32, (n_feat, Q_TILE), 0)
        i = r0 + lax.broadcasted_iota(jnp.int32, (n_feat, Q_TILE), 1)
        i_lo = i % POS_SPLIT
        i_hi = i - i_lo
        for h in range(N_HEADS):
            group = qt[(h // 2) * pair:(h // 2 + 1) * pair]
            for mp in range(2):
                lo = (h % 2) * HEAD_DIM + mp * DIFF_QK_DIM
                keep = (row >= lo) & (row < lo + DIFF_QK_DIM)
                qt_scr[ci, 2 * h + mp] = jnp.where(keep, group, 0.0).astype(BF16)
            slope = _alibi_slope(h)
            feat = jnp.where(feat_row < 2, slope,
                             jnp.where(feat_row == 2, -slope * i_hi.astype(F32),
                                       jnp.where(feat_row == 3, -slope * i_lo.astype(F32), 0.0)))
            qf_scr[ci, h, 0:n_feat, :] = feat.astype(BF16)
            qf_scr[ci, h, n_feat:, :] = (-feat).astype(BF16)
        lane = lax.broadcasted_iota(jnp.int32, (Q_TILE, pair), 1)
        j = r0 + lax.broadcasted_iota(jnp.int32, (Q_TILE, pair), 0)
        j_lo = j % POS_SPLIT
        k_feat = jnp.where(lane == 0, (j - j_lo).astype(F32),
                           jnp.where(lane == 1, j_lo.astype(F32), jnp.where(lane < 4, 1.0, 0.0))).astype(BF16)
        for p in range(N_HEADS // 2):
            k_scr[p, rows, 0:pair] = k_ref[0, rows, p * pair:(p + 1) * pair].astype(BF16)
            k_scr[p, rows, pair:] = k_feat
        vt = v_ref[0, rows, :].T.astype(BF16)
        for h in range(N_HEADS):
            vt_scr[h, ci, 0:HEAD_DIM, :] = vt[h * HEAD_DIM:(h + 1) * HEAD_DIM]
            vt_scr[h, ci, HEAD_DIM:, :] = _ones_row_block(Q_TILE)
        return carry

    lax.fori_loop(0, n_tiles, prep, 0)

    def key_chunk(qi, c):
        wrapped = qi + c >= n_tiles
        return jnp.where(wrapped, qi + c - n_tiles, qi + c), wrapped

    def score_chunk(qi, item, c):
        h = item // 2
        kc, left = key_chunk(qi, c)
        form = 0 if c == 0 else pl.multiple_of(jnp.where(left, 0, n_feat), n_feat)
        feat = qf_scr[qi, h, pl.ds(form, n_feat), :]
        rhs = jnp.concatenate([qt_scr[qi, item], feat, jnp.zeros((pair - n_feat, Q_TILE), BF16)], axis=0)
        st = _dot(k_scr[h // 2, pl.ds(pl.multiple_of(kc * Q_TILE, Q_TILE), Q_TILE), :], rhs)
        return st + diag_ref[h] if c == 0 else st

    def value_chunk(qi, item, c):
        return vt_scr[item // 2, key_chunk(qi, c)[0]]

    def combine(accs):
        outs = []
        for h in range(N_HEADS):
            a1, a2 = accs[2 * h], accs[2 * h + 1]
            o = a1[:HEAD_DIM] / a1[HEAD_DIM:HEAD_DIM + 1] - lam * (a2[:HEAD_DIM] / a2[HEAD_DIM:HEAD_DIM + 1])
            r = lax.rsqrt(jnp.mean(o * o, axis=0, keepdims=True) + EPS)
            outs.append(o * r * subln_ref[...] * (1.0 - lambda_init))
        return jnp.concatenate(outs, axis=0)

    _pipelined_softmax_pv(n_tiles, 2 * N_HEADS, n_tiles, score_chunk, value_chunk, combine, s_scr, o_ref,
                          exp_fn=jnp.exp)


def _diff_attention(proj, lq1, lk1, lq2, lk2, subln, lambda_init):
    b = proj.shape[0]
    vec = _const_spec((1, DIFF_QK_DIM))
    n_tiles = SEQ // Q_TILE
    rel = np.arange(Q_TILE)
    over = -2.0 * np.maximum(rel[:, None] - rel[None, :], 0).astype(np.float32)
    diag = jnp.asarray(np.stack([_alibi_slope(h) * over for h in range(N_HEADS)]))
    subln_cols = jnp.broadcast_to(subln.reshape(HEAD_DIM, 1), (HEAD_DIM, Q_TILE))
    return pl.pallas_call(
        functools.partial(_diff_kernel, lambda_init),
        out_shape=jax.ShapeDtypeStruct((b, SEQ, GROUP_WIDTH), BF16),
        grid=(b,),
        in_specs=[
            _proj_spec(256, B256_DF_Q), _proj_spec(256, B256_DF_K), _proj_spec(256, B256_DF_V),
            vec, vec, vec, vec, _const_spec((HEAD_DIM, Q_TILE)), _const_spec(diag.shape),
        ],
        out_specs=pl.BlockSpec((1, SEQ, GROUP_WIDTH), lambda i: (i, 0, 0)),
        scratch_shapes=[pltpu.VMEM((n_tiles, 2 * N_HEADS, 2 * HEAD_DIM, Q_TILE), BF16),
                        pltpu.VMEM((n_tiles, N_HEADS, 32, Q_TILE), BF16),
                        pltpu.VMEM((N_HEADS // 2, SEQ, 4 * HEAD_DIM), BF16),
                        pltpu.VMEM((N_HEADS, n_tiles, HEAD_DIM + 16, Q_TILE), BF16),
                        pltpu.VMEM((2, SEQ, Q_TILE), F32)],
        compiler_params=_compiler_params(),
        name="diff_attention",
    )(proj, proj, proj, lq1, lk1, lq2, lk2, subln_cols, diag)


def _gqa_kernel(q_ref, qr_ref, k_ref, kr_ref, v_ref, gq_ref, gqp_ref, gk_ref, gkp_ref, cos_ref, sin_ref, o_ref,
                qt_scr, k_scr, vt_scr, s_scr):
    q_scale = HEAD_DIM ** -0.5 * LOG2E
    kv_heads = N_HEADS // 2
    n_tiles = SEQ // Q_TILE
    pair = 2 * HEAD_DIM

    def normed_rotary(x, xr, g, gp, cos, sin):
        r = lax.rsqrt(_segment_mean_square(x, HEAD_DIM) + EPS)
        return (x * r * g) * cos + (xr * r * gp) * sin

    def prep(ci, carry):
        r0 = pl.multiple_of(ci * Q_TILE, Q_TILE)
        rows = pl.ds(r0, Q_TILE)
        cos, sin = cos_ref[rows, :], sin_ref[rows, :]
        zeros = jnp.zeros((HEAD_DIM, Q_TILE), BF16)
        for half in range(2):
            lanes = slice(half * pair, (half + 1) * pair)
            y = normed_rotary(q_ref[0, rows, lanes], qr_ref[0, rows, lanes], gq_ref[...], gqp_ref[...], cos, sin)
            yt = (y * q_scale).T.astype(BF16)
            for j in range(2):
                h = 2 * half + j
                kv = h // (N_HEADS // kv_heads)
                for part in range(kv_heads):
                    block = yt[j * HEAD_DIM:(j + 1) * HEAD_DIM] if part == kv else zeros
                    qt_scr[ci, h * pair + part * HEAD_DIM:h * pair + (part + 1) * HEAD_DIM, :] = block
        k_scr[rows, :] = normed_rotary(k_ref[0, rows, :], kr_ref[0, rows, :], gk_ref[...], gkp_ref[...], cos,
                                       sin).astype(BF16)
        vt = v_ref[0, rows, :].T.astype(BF16)
        for kv in range(kv_heads):
            vt_scr[kv, ci, 0:HEAD_DIM, :] = vt[kv * HEAD_DIM:(kv + 1) * HEAD_DIM]
            vt_scr[kv, ci, HEAD_DIM:, :] = _ones_row_block(Q_TILE)
        return carry

    lax.fori_loop(0, n_tiles, prep, 0)

    def score_chunk(qi, h, c):
        return _dot(k_scr[c * Q_TILE:(c + 1) * Q_TILE, :], qt_scr[qi, h * pair:(h + 1) * pair, :])

    def value_chunk(qi, h, c):
        return vt_scr[h // (N_HEADS // kv_heads), c]

    _pipelined_softmax_pv(n_tiles, N_HEADS, n_tiles, score_chunk, value_chunk, _normalised_heads, s_scr, o_ref)


def _gqa_attention(proj, gq, gqp, gk, gkp, cos, sin):
    b = proj.shape[0]
    vec = _const_spec((1, 2 * HEAD_DIM))
    tab = _const_spec((SEQ, 2 * HEAD_DIM))
    return pl.pallas_call(
        _gqa_kernel,
        out_shape=jax.ShapeDtypeStruct((b, SEQ, GROUP_WIDTH), BF16),
        grid=(b,),
        in_specs=[
            _proj_spec(256, B256_GQ_Q), _proj_spec(256, B256_GQ_QR),
            _proj_spec(128, B128_GQ_K), _proj_spec(128, B128_GQ_KR), _proj_spec(128, B128_GQ_V),
            vec, vec, vec, vec, tab, tab,
        ],
        out_specs=pl.BlockSpec((1, SEQ, GROUP_WIDTH), lambda i: (i, 0, 0)),
        scratch_shapes=[pltpu.VMEM((SEQ // Q_TILE, N_HEADS * 2 * HEAD_DIM, Q_TILE), BF16),
                        pltpu.VMEM((SEQ, 2 * HEAD_DIM), BF16),
                        pltpu.VMEM((N_HEADS // 2, SEQ // Q_TILE, HEAD_DIM + 16, Q_TILE), BF16),
                        pltpu.VMEM((2, SEQ, Q_TILE), F32)],
        compiler_params=_compiler_params(),
        name="gqa_attention",
    )(proj, proj, proj, proj, proj, gq, gqp, gk, gkp, cos, sin)


def _mla_kernel(cq_ref, ckv_ref, kr_ref, krr_ref, gq_ref, gkv_ref, wuq_ref, wukv_ref, cos_ref, sin_ref, o_ref,
                qt_scr, k_scr, vt_scr, s_scr):
    q_scale = (MLA_NOPE_DIM + MLA_ROPE_DIM) ** -0.5 * LOG2E
    n_tiles = SEQ // Q_TILE
    rot0 = N_HEADS * 128
    v0 = N_HEADS * 128

    def prep(ci, carry):
        r0 = pl.multiple_of(ci * Q_TILE, Q_TILE)
        rows = pl.ds(r0, Q_TILE)
        cos, sin = cos_ref[rows, :], sin_ref[rows, :]
        cq = cq_ref[0, rows, :]
        cqn = (cq * _rms(cq) * gq_ref[...]).astype(BF16)
        ckv = ckv_ref[0, rows, :]
        ckvn = (ckv * _rms(ckv) * gkv_ref[...]).astype(BF16)
        k_rope = kr_ref[0, rows, :] * cos + krr_ref[0, rows, :] * sin
        for h in range(N_HEADS):
            cols = slice(h * 128, (h + 1) * 128)
            rot_cols = slice(rot0 + h * 128, rot0 + (h + 1) * 128)
            qh = _dot(cqn, wuq_ref[:, cols]) * cos + _dot(cqn, wuq_ref[:, rot_cols]) * sin
            qt_scr[ci, h * 128:(h + 1) * 128, :] = (qh * q_scale).T.astype(BF16)
            k_scr[h, rows, :] = (_dot(ckvn, wukv_ref[:, cols]) + k_rope).astype(BF16)
        vt = _dot(ckvn, wukv_ref[:, v0:v0 + GROUP_WIDTH]).T.astype(BF16)
        for h in range(N_HEADS):
            vt_scr[h, ci, 0:HEAD_DIM, :] = vt[h * HEAD_DIM:(h + 1) * HEAD_DIM]
            vt_scr[h, ci, HEAD_DIM:, :] = _ones_row_block(Q_TILE)
        return carry

    lax.fori_loop(0, n_tiles, prep, 0)

    def score_chunk(qi, h, c):
        return _dot(k_scr[h, c * Q_TILE:(c + 1) * Q_TILE, :], qt_scr[qi, h * 128:(h + 1) * 128, :])

    def value_chunk(qi, h, c):
        return vt_scr[h, c]

    _pipelined_softmax_pv(n_tiles, N_HEADS, n_tiles, score_chunk, value_chunk, _normalised_heads, s_scr, o_ref)


def _mla_attention(proj, gq, gkv, wuq, wukv, cos, sin):
    b = proj.shape[0]
    tab = _const_spec((SEQ, 128))
    return pl.pallas_call(
        _mla_kernel,
        out_shape=jax.ShapeDtypeStruct((b, SEQ, GROUP_WIDTH), BF16),
        grid=(b,),
        in_specs=[
            _proj_spec(256, B256_ML_CQ), _proj_spec(128, B128_ML_CKV),
            _proj_spec(128, B128_ML_KR), _proj_spec(128, B128_ML_KRR),
            _const_spec((1, GROUP_WIDTH)), _const_spec((1, MLA_KV_LORA)),
            _const_spec(wuq.shape), _const_spec(wukv.shape), tab, tab,
        ],
        out_specs=pl.BlockSpec((1, SEQ, GROUP_WIDTH), lambda i: (i, 0, 0)),
        scratch_shapes=[pltpu.VMEM((SEQ // Q_TILE, N_HEADS * 128, Q_TILE), BF16),
                        pltpu.VMEM((N_HEADS, SEQ, 128), BF16),
                        pltpu.VMEM((N_HEADS, SEQ // Q_TILE, HEAD_DIM + 16, Q_TILE), BF16),
                        pltpu.VMEM((2, SEQ, Q_TILE), F32)],
        compiler_params=_compiler_params(),
        name="mla_attention",
    )(proj, proj, proj, proj, gq, gkv, wuq, wukv, cos, sin)


def _post_kernel(x_ref, a_ref, b_ref, c_ref, d_ref, wo_ref, g_mix_ref, g_pre_ref, wg_ref, wu_ref, wd_ref,
                 g_ffn_ref, o_ref, acc_ref):
    mix = (_dot(a_ref[...], wo_ref[0]) + _dot(b_ref[...], wo_ref[1])
           + _dot(c_ref[...], wo_ref[2]) + _dot(d_ref[...], wo_ref[3]))
    x = x_ref[...] + mix * _rms(mix) * g_mix_ref[...]
    h = (x * _rms(x) * g_pre_ref[...]).astype(BF16)
    acc_ref[...] = jnp.zeros_like(acc_ref)

    def body(ci, carry):
        gate = _dot(h, wg_ref[ci])
        up = _dot(h, wu_ref[ci])
        act = (gate * jax.nn.sigmoid(gate) * up).astype(BF16)
        acc_ref[...] += _dot(act, wd_ref[ci])
        return carry

    lax.fori_loop(0, D_FF // FF_CHUNK, body, 0)
    f = acc_ref[...]
    o_ref[...] = x + f * _rms(f) * g_ffn_ref[...]


def _post(x2d, a, b, c, d, wo, g_mix, g_pre, wg, wu, wd, g_ffn):
    t = x2d.shape[0]
    tok = lambda w: pl.BlockSpec((TOKEN_TILE, w), lambda i: (i, 0))
    vec = _const_spec((1, D_MODEL))
    return pl.pallas_call(
        _post_kernel,
        out_shape=jax.ShapeDtypeStruct((t, D_MODEL), F32),
        grid=(t // TOKEN_TILE,),
        in_specs=[
            tok(D_MODEL), tok(GROUP_WIDTH), tok(GROUP_WIDTH), tok(GROUP_WIDTH), tok(GROUP_WIDTH),
            _const_spec(wo.shape), vec, vec, _const_spec(wg.shape), _const_spec(wu.shape), _const_spec(wd.shape),
            vec,
        ],
        out_specs=tok(D_MODEL),
        scratch_shapes=[pltpu.VMEM((TOKEN_TILE, D_MODEL), F32)],
        compiler_params=_compiler_params(),
        name="outproj_swiglu",
    )(x2d, a, b, c, d, wo, g_mix, g_pre, wg, wu, wd, g_ffn)


def _layout_w_in(w):
    src64, sign64 = _rot_perm(HEAD_DIM, HEAD_DIM // 2)
    src_q = np.concatenate([h * HEAD_DIM + src64 for h in range(N_HEADS)])
    src_k = src_q[: 2 * HEAD_DIM]
    src32, sign32 = _rot_perm(MLA_ROPE_DIM, MLA_ROPE_DIM)
    gq_q = w[:, O_GQ_Q:O_GQ_K]
    gq_k = w[:, O_GQ_K:O_GQ_V]
    k_rope = w[:, O_ML_KR:O_ML_KR + MLA_ROPE_DIM]
    pad = lambda m: jnp.pad(m, ((0, 0), (MLA_NOPE_DIM, 128 - MLA_NOPE_DIM - MLA_ROPE_DIM)))
    cols = [
        w[:, :O_GQ_K],
        gq_q[:, src_q] * np.tile(sign64, N_HEADS),
        w[:, O_ML_CQ:O_ML_CKV],
        gq_k, gq_k[:, src_k] * np.tile(sign64, 2),
        w[:, O_GQ_V:O_ML_CQ],
        w[:, O_ML_CKV:O_ML_KR],
        pad(k_rope), pad(k_rope[:, src32] * sign32),
    ]
    return jnp.concatenate(cols, axis=1).astype(BF16)


def _layout_mla_weights(w_uq, w_ukv):
    src32, sign32 = _rot_perm(MLA_ROPE_DIM, MLA_ROPE_DIM)
    wq = w_uq.reshape(GROUP_WIDTH, N_HEADS, MLA_NOPE_DIM + MLA_ROPE_DIM)
    rope = wq[:, :, MLA_NOPE_DIM:]
    tail = 128 - MLA_NOPE_DIM - MLA_ROPE_DIM
    q_main = jnp.pad(wq, ((0, 0), (0, 0), (0, tail)))
    q_rot = jnp.pad(rope[:, :, src32] * sign32, ((0, 0), (0, 0), (MLA_NOPE_DIM, tail)))
    wuq = jnp.concatenate([q_main.reshape(GROUP_WIDTH, -1), q_rot.reshape(GROUP_WIDTH, -1)], axis=1)
    wkv = w_ukv.reshape(MLA_KV_LORA, N_HEADS, MLA_NOPE_DIM + HEAD_DIM)
    k_nope = jnp.pad(wkv[:, :, :MLA_NOPE_DIM], ((0, 0), (0, 0), (0, 128 - MLA_NOPE_DIM)))
    vals = wkv[:, :, MLA_NOPE_DIM:]
    wukv = jnp.concatenate([k_nope.reshape(MLA_KV_LORA, -1), vals.reshape(MLA_KV_LORA, -1)], axis=1)
    return wuq.astype(BF16), wukv.astype(BF16)


def _rotary_tables():
    pos = jnp.arange(SEQ)
    half = HEAD_DIM // 2
    inv = ROPE_THETA ** (-jnp.arange(0, half, 2, dtype=F32) / half)

    def angles(p):
        ang = p.astype(F32)[:, None] * inv[None, :]
        return jnp.concatenate([ang, ang], axis=-1)

    axial = jnp.concatenate([angles(pos // GRID_W), angles(pos % GRID_W)], axis=-1)
    seq = angles(pos)
    tail = 128 - MLA_NOPE_DIM - MLA_ROPE_DIM
    mla_cos = jnp.concatenate([jnp.ones((SEQ, MLA_NOPE_DIM), F32), jnp.cos(seq), jnp.zeros((SEQ, tail), F32)], -1)
    mla_sin = jnp.pad(jnp.sin(seq), ((0, 0), (MLA_NOPE_DIM, tail)))
    pair = lambda t: jnp.tile(t, (1, 2))
    return pair(jnp.cos(axial)), pair(jnp.sin(axial)), mla_cos, mla_sin


def kernel(x, pre_mix_norm, w_in, na_rel_bias, diff_lambda_q1, diff_lambda_k1, diff_lambda_q2, diff_lambda_k2,
           diff_subln, gqa_q_norm, gqa_k_norm, mla_q_norm, mla_kv_norm, mla_w_uq, mla_w_ukv, w_o, post_mix_norm,
           pre_ffn_norm, ffn_w_gate_up, ffn_w_down, post_ffn_norm):
    b, s, d = x.shape
    assert (s, d) == (SEQ, D_MODEL)
    depth = w_in.shape[0]
    src64, _ = _rot_perm(HEAD_DIM, HEAD_DIM // 2)
    ax_cos, ax_sin, mla_cos, mla_sin = _rotary_tables()
    row = lambda v: v.reshape(1, -1).astype(F32)
    n_chunks = D_FF // FF_CHUNK

    x2d = x.reshape(b * s, d)
    for l in range(depth):
        lambda_init = 0.8 - 0.6 * math.exp(-0.3 * l)
        proj = _inproj(x2d, row(pre_mix_norm[l]), _layout_w_in(w_in[l])).reshape(b, s, PROJ_WIDTH)

        a_out = _na_attention(proj, _na_tables(na_rel_bias[l]))
        b_out = _diff_attention(proj, row(diff_lambda_q1[l]), row(diff_lambda_k1[l]), row(diff_lambda_q2[l]),
                                row(diff_lambda_k2[l]), row(diff_subln[l]), lambda_init)
        pair_row = lambda v: row(jnp.tile(v, 2))
        c_out = _gqa_attention(proj, pair_row(gqa_q_norm[l]), pair_row(gqa_q_norm[l][src64]),
                               pair_row(gqa_k_norm[l]), pair_row(gqa_k_norm[l][src64]), ax_cos, ax_sin)
        wuq, wukv = _layout_mla_weights(mla_w_uq[l], mla_w_ukv[l])
        d_out = _mla_attention(proj, row(mla_q_norm[l]), row(mla_kv_norm[l]), wuq, wukv, mla_cos, mla_sin)

        gate_up = ffn_w_gate_up[l].astype(BF16)
        wg = gate_up[:, :D_FF].reshape(d, n_chunks, FF_CHUNK).transpose(1, 0, 2)
        wu = gate_up[:, D_FF:].reshape(d, n_chunks, FF_CHUNK).transpose(1, 0, 2)
        wd = ffn_w_down[l].astype(BF16).reshape(n_chunks, FF_CHUNK, d)
        wo = w_o[l].astype(BF16).reshape(N_HEADS, GROUP_WIDTH, d)
        flat = lambda t: t.reshape(b * s, GROUP_WIDTH)
        x2d = _post(x2d, flat(a_out), flat(b_out), flat(c_out), flat(d_out), wo, row(post_mix_norm[l]),
                    row(pre_ffn_norm[l]), wg, wu, wd, row(post_ffn_norm[l]))
    return x2d.reshape(b, s, d)
```

```python
import functools
import math

import numpy as np
import jax
import jax.numpy as jnp
from jax import lax
from jax.experimental import pallas as pl
from jax.experimental.pallas import tpu as pltpu

F32 = jnp.float32
BF16 = jnp.bfloat16

D_MODEL = 1024
SEQ = 2048
GRID_W = 64
GRID_ROWS = SEQ // GRID_W
HEAD_DIM = 64
N_HEADS = 4
GROUP_WIDTH = 256
EPS = 1e-6
ROPE_THETA = 10000.0

NA_WIN_ROWS = 8
NA_WIN_COLS = 16
DIFF_QK_DIM = 32
POS_SPLIT = 64
MLA_NOPE_DIM = 64
MLA_ROPE_DIM = 32
MLA_KV_LORA = 128
D_FF = 2816

O_GQ_Q, O_GQ_K, O_GQ_V, O_ML_CQ, O_ML_CKV, O_ML_KR = 1536, 1792, 1920, 2048, 2304, 2432

PROJ_WIDTH = 3072
B256_NA_Q, B256_NA_K, B256_NA_V, B256_DF_Q, B256_DF_K, B256_DF_V, B256_GQ_Q, B256_GQ_QR, B256_ML_CQ = range(9)
B128_GQ_K, B128_GQ_KR, B128_GQ_V, B128_ML_CKV, B128_ML_KR, B128_ML_KRR = range(18, 24)

VMEM_LIMIT_BYTES = 56 * 1024 * 1024

TOKEN_TILE = 1024
FF_CHUNK = 256
Q_TILE = 256
NA_GROUP_ROWS = 4
NA_BAND_ROWS = 12
NEG_BIG = -1e30
LOG2E = math.log2(math.e)


def _rot_perm(width, group):
    j = np.arange(width)
    jj = j % group
    half = group // 2
    src = (j // group) * group + (jj + half) % group
    sign = np.where(jj < half, -1.0, 1.0).astype(np.float32)
    return src, sign


def _compiler_params():
    return pltpu.CompilerParams(dimension_semantics=("arbitrary",), vmem_limit_bytes=VMEM_LIMIT_BYTES)


def _const_spec(shape):
    zeros = (0,) * len(shape)
    return pl.BlockSpec(shape, lambda i: zeros, pipeline_mode=pl.Buffered(1))


def _rms(x):
    return lax.rsqrt(jnp.mean(x * x, axis=-1, keepdims=True) + EPS)


def _dot(a, b):
    return jnp.dot(a, b, preferred_element_type=F32)


def _dot_nt(a, b):
    return lax.dot_general(a, b, (((1,), (1,)), ((), ())), preferred_element_type=F32)


def _inproj_kernel(x_ref, g_ref, w_ref, o_ref):
    x = x_ref[...]
    h = (x * _rms(x) * g_ref[...]).astype(BF16)
    o_ref[...] = _dot(h, w_ref[...])


def _inproj(x2d, gain, w):
    t = x2d.shape[0]
    return pl.pallas_call(
        _inproj_kernel,
        out_shape=jax.ShapeDtypeStruct((t, PROJ_WIDTH), F32),
        grid=(t // TOKEN_TILE,),
        in_specs=[
            pl.BlockSpec((TOKEN_TILE, D_MODEL), lambda i: (i, 0)),
            _const_spec((1, D_MODEL)),
            _const_spec((D_MODEL, PROJ_WIDTH)),
        ],
        out_specs=pl.BlockSpec((TOKEN_TILE, PROJ_WIDTH), lambda i: (i, 0)),
        compiler_params=_compiler_params(),
        name="inproj",
    )(x2d, gain, w)


def _softmax_unnormalised(s):
    m = jnp.max(s, axis=-1, keepdims=True)
    p = jnp.exp(s - m)
    return p, jnp.sum(p, axis=-1, keepdims=True)


def _segment_mean_square(x, seg):
    w = x.shape[-1]
    same = (lax.broadcasted_iota(jnp.int32, (w, w), 0) // seg) == (lax.broadcasted_iota(jnp.int32, (w, w), 1) // seg)
    ones = jnp.where(same, 1.0, 0.0).astype(BF16)
    sq = x * x
    hi = sq.astype(BF16)
    lo = (sq - hi.astype(F32)).astype(BF16)
    return (_dot(hi, ones) + _dot(lo, ones)) * (1.0 / seg)


def _ones_row_block(width):
    return jnp.where(lax.broadcasted_iota(jnp.int32, (16, width), 0) == 0, 1.0, 0.0).astype(BF16)


def _softmax_pv_transposed(sts, vts):
    m = functools.reduce(jnp.maximum, sts)
    m = jnp.max(m, axis=0, keepdims=True)
    acc = None
    for st, vt in zip(sts, vts):
        part = _dot(vt, jnp.exp2(st - m).astype(BF16))
        acc = part if acc is None else acc + part
    return acc[:HEAD_DIM] / acc[HEAD_DIM:HEAD_DIM + 1]


def _normalised_heads(accs):
    return jnp.concatenate([acc[:HEAD_DIM] / acc[HEAD_DIM:HEAD_DIM + 1] for acc in accs], axis=0)


def _pipelined_softmax_pv(n_tiles, n_heads, n_chunks, score_chunks, value_chunk, combine, s_scr, o_ref,
                          exp_fn=jnp.exp2):
    assert n_heads % 2 == 0
    rows = s_scr.shape[1] // n_chunks

    def scores_into(slot, qi, h):
        m8 = None
        for c, st in enumerate(score_chunks(qi, h)):
            s_scr[slot, c * rows:(c + 1) * rows, :] = st
            cm = jnp.max(st.reshape(-1, 8, st.shape[-1]), axis=0)
            m8 = cm if m8 is None else jnp.maximum(m8, cm)
        return jnp.max(m8, axis=0, keepdims=True)

    def body(qi, m):
        next_qi = jnp.minimum(qi + 1, n_tiles - 1)
        accs = []
        for h in range(n_heads):
            slot = h % 2
            nq, nh = (qi, h + 1) if h + 1 < n_heads else (next_qi, 0)
            next_m = scores_into(1 - slot, nq, nh)
            acc = None
            for c in range(n_chunks):
                p = exp_fn(s_scr[slot, c * rows:(c + 1) * rows, :] - m).astype(BF16)
                part = _dot(value_chunk(qi, h, c), p)
                acc = part if acc is None else acc + part
            accs.append(acc)
            m = next_m
        r0 = pl.multiple_of(qi * Q_TILE, Q_TILE)
        o_ref[0, pl.ds(r0, Q_TILE), :] = combine(accs).T.astype(o_ref.dtype)
        return m

    lax.fori_loop(0, n_tiles, body, scores_into(0, 0, 0))


def _proj_spec(width, block):
    return pl.BlockSpec((1, SEQ, width), lambda b: (b, 0, block))


def _na_kernel(q_ref, k_ref, v_ref, tab_ref, o_ref, qt_scr, k_scr, vt_scr, s_scr):
    assert NA_GROUP_ROWS * GRID_W == Q_TILE and NA_BAND_ROWS % NA_GROUP_ROWS == 0
    n_tiles = SEQ // Q_TILE
    band_chunks = NA_BAND_ROWS // NA_GROUP_ROWS
    pair = 2 * HEAD_DIM
    q_scale = HEAD_DIM ** -0.5 * LOG2E

    def prep(ci, carry):
        r0 = pl.multiple_of(ci * Q_TILE, Q_TILE)
        rows = pl.ds(r0, Q_TILE)
        qt = (q_ref[0, rows, :] * q_scale).T
        row = lax.broadcasted_iota(jnp.int32, (pair, Q_TILE), 0)
        for h in range(N_HEADS):
            own = (row >= (h % 2) * HEAD_DIM) & (row < (h % 2 + 1) * HEAD_DIM)
            qt_scr[ci, h] = jnp.where(own, qt[(h // 2) * pair:(h // 2 + 1) * pair], 0.0).astype(BF16)
        for p in range(N_HEADS // 2):
            k_scr[p, rows, :] = k_ref[0, rows, p * pair:(p + 1) * pair].astype(BF16)
        vt = v_ref[0, rows, :].T.astype(BF16)
        for h in range(N_HEADS):
            vt_scr[h, ci, 0:HEAD_DIM, :] = vt[h * HEAD_DIM:(h + 1) * HEAD_DIM]
            vt_scr[h, ci, HEAD_DIM:, :] = _ones_row_block(Q_TILE)
        return carry

    lax.fori_loop(0, n_tiles, prep, 0)

    def key_chunk(gi, c):
        return jnp.clip(gi - (NA_WIN_ROWS // 2) // NA_GROUP_ROWS, 0, n_tiles - band_chunks) + c

    def score_chunks(gi, h):
        kind = jnp.where(gi == 0, 0, jnp.where(gi == n_tiles - 1, 2, 1))
        k0 = pl.multiple_of(key_chunk(gi, 0) * Q_TILE, Q_TILE)
        return [_dot(k_scr[h // 2, pl.ds(k0 + c * Q_TILE, Q_TILE), :], qt_scr[gi, h])
                + tab_ref[kind, h, c * Q_TILE:(c + 1) * Q_TILE, :] for c in range(band_chunks)]

    def value_chunk(gi, h, c):
        return vt_scr[h, key_chunk(gi, c)]

    _pipelined_softmax_pv(n_tiles, N_HEADS, band_chunks, score_chunks, value_chunk, _normalised_heads, s_scr, o_ref)


def _na_tables(rel_bias):
    a = np.arange(NA_GROUP_ROWS)[:, None, None, None]
    c = np.arange(GRID_W)[None, :, None, None]
    i = np.arange(NA_BAND_ROWS)[None, None, :, None]
    kc = np.arange(GRID_W)[None, None, None, :]
    cs = np.clip(c - NA_WIN_COLS // 2, 0, GRID_W - NA_WIN_COLS)
    col_ok = (kc >= cs) & (kc < cs + NA_WIN_COLS)
    dc = kc - c + NA_WIN_COLS - 1
    col_sel = (dc[..., None] == np.arange(2 * NA_WIN_COLS - 1)) & col_ok[..., None]
    row_sels = []
    last_r0 = GRID_ROWS - NA_GROUP_ROWS
    for r0, band_row in ((0, 0), (NA_GROUP_ROWS, 0), (last_r0, GRID_ROWS - NA_BAND_ROWS)):
        r = r0 + a
        rs = np.clip(r - NA_WIN_ROWS // 2, 0, GRID_ROWS - NA_WIN_ROWS)
        key_row = band_row + i
        row_ok = (key_row >= rs) & (key_row < rs + NA_WIN_ROWS)
        dr = key_row - r + NA_WIN_ROWS - 1
        row_sels.append((dr[..., None] == np.arange(2 * NA_WIN_ROWS - 1)) & row_ok[..., None])
    row_sel = np.stack(row_sels)[:, :, 0, :, 0, :].astype(np.float32)
    col_sel = col_sel[0, :, 0, :, :].astype(np.float32)
    vals = jnp.einsum("taiu,huv,ckv->thikac", row_sel, rel_bias.astype(F32), col_sel,
                      precision=lax.Precision.HIGHEST)
    inside = np.einsum("taiu,ckv->tikac", row_sel, col_sel) > 0
    tab = jnp.where(inside[:, None], vals * LOG2E, NEG_BIG)
    q_rows, band = NA_GROUP_ROWS * GRID_W, NA_BAND_ROWS * GRID_W
    return tab.reshape(3, N_HEADS, band, q_rows)


def _na_attention(proj, tables):
    b = proj.shape[0]
    return pl.pallas_call(
        _na_kernel,
        out_shape=jax.ShapeDtypeStruct((b, SEQ, GROUP_WIDTH), BF16),
        grid=(b,),
        in_specs=[
            _proj_spec(256, B256_NA_Q), _proj_spec(256, B256_NA_K), _proj_spec(256, B256_NA_V),
            _const_spec(tables.shape),
        ],
        out_specs=pl.BlockSpec((1, SEQ, GROUP_WIDTH), lambda i: (i, 0, 0)),
        scratch_shapes=[pltpu.VMEM((SEQ // Q_TILE, N_HEADS, 2 * HEAD_DIM, Q_TILE), BF16),
                        pltpu.VMEM((N_HEADS // 2, SEQ, 2 * HEAD_DIM), BF16),
                        pltpu.VMEM((N_HEADS, SEQ // Q_TILE, HEAD_DIM + 16, Q_TILE), BF16),
                        pltpu.VMEM((2, NA_BAND_ROWS * GRID_W, Q_TILE), F32)],
        compiler_params=_compiler_params(),
        name="na_attention",
    )(proj, proj, proj, tables)


def _alibi_slope(h):
    return 2.0 ** (-8.0 * (h + 1) / N_HEADS)


def _diff_kernel(lambda_init, q_ref, k_ref, v_ref, lq1_ref, lk1_ref, lq2_ref, lk2_ref, subln_ref, diag_ref, o_ref,
                 qt_scr, qf_scr, k_scr, vt_scr, s_scr):
    scale = DIFF_QK_DIM ** -0.5
    n_tiles = SEQ // Q_TILE
    pair = 2 * HEAD_DIM
    n_feat = 16
    lam = (jnp.exp(jnp.sum(lq1_ref[...] * lk1_ref[...], axis=-1, keepdims=True))
           - jnp.exp(jnp.sum(lq2_ref[...] * lk2_ref[...], axis=-1, keepdims=True)) + lambda_init)

    def prep(ci, carry):
        r0 = pl.multiple_of(ci * Q_TILE, Q_TILE)
        rows = pl.ds(r0, Q_TILE)
        qt = (q_ref[0, rows, :] * scale).T
        row = lax.broadcasted_iota(jnp.int32, (pair, Q_TILE), 0)
        feat_row = lax.broadcasted_iota(jnp.int32, (n_feat, Q_TILE), 0)
        i = r0 + lax.broadcasted_iota(jnp.int32, (n_feat, Q_TILE), 1)
        i_lo = i % POS_SPLIT
        i_hi = i - i_lo
        for h in range(N_HEADS):
            group = qt[(h // 2) * pair:(h // 2 + 1) * pair]
            for mp in range(2):
                lo = (h % 2) * HEAD_DIM + mp * DIFF_QK_DIM
                keep = (row >= lo) & (row < lo + DIFF_QK_DIM)
                qt_scr[ci, 2 * h + mp] = jnp.where(keep, group, 0.0).astype(BF16)
            slope = _alibi_slope(h)
            feat = jnp.where(feat_row < 2, slope,
                             jnp.where(feat_row == 2, -slope * i_hi.astype(F32),
                                       jnp.where(feat_row == 3, -slope * i_lo.astype(F32), 0.0)))
            qf_scr[ci, h, 0:n_feat, :] = feat.astype(BF16)
            qf_scr[ci, h, n_feat:, :] = (-feat).astype(BF16)
        lane = lax.broadcasted_iota(jnp.int32, (Q_TILE, pair), 1)
        j = r0 + lax.broadcasted_iota(jnp.int32, (Q_TILE, pair), 0)
        j_lo = j % POS_SPLIT
        k_feat = jnp.where(lane == 0, (j - j_lo).astype(F32),
                           jnp.where(lane == 1, j_lo.astype(F32), jnp.where(lane < 4, 1.0, 0.0))).astype(BF16)
        for p in range(N_HEADS // 2):
            k_scr[p, rows, 0:pair] = k_ref[0, rows, p * pair:(p + 1) * pair].astype(BF16)
            k_scr[p, rows, pair:] = k_feat
        vt = v_ref[0, rows, :].T.astype(BF16)
        for h in range(N_HEADS):
            vt_scr[h, ci, 0:HEAD_DIM, :] = vt[h * HEAD_DIM:(h + 1) * HEAD_DIM]
            vt_scr[h, ci, HEAD_DIM:, :] = _ones_row_block(Q_TILE)
        return carry

    lax.fori_loop(0, n_tiles, prep, 0)

    def key_chunk(qi, c):
        wrapped = qi + c >= n_tiles
        return jnp.where(wrapped, qi + c - n_tiles, qi + c), wrapped

    def score_chunk(qi, item, c):
        h = item // 2
        kc, left = key_chunk(qi, c)
        form = 0 if c == 0 else pl.multiple_of(jnp.where(left, 0, n_feat), n_feat)
        feat = qf_scr[qi, h, pl.ds(form, n_feat), :]
        rhs = jnp.concatenate([qt_scr[qi, item], feat, jnp.zeros((pair - n_feat, Q_TILE), BF16)], axis=0)
        st = _dot(k_scr[h // 2, pl.ds(pl.multiple_of(kc * Q_TILE, Q_TILE), Q_TILE), :], rhs)
        return st + diag_ref[h] if c == 0 else st

    def score_chunks(qi, item):
        return [score_chunk(qi, item, c) for c in range(n_tiles)]

    def value_chunk(qi, item, c):
        return vt_scr[item // 2, key_chunk(qi, c)[0]]

    def combine(accs):
        outs = []
        for h in range(N_HEADS):
            a1, a2 = accs[2 * h], accs[2 * h + 1]
            o = a1[:HEAD_DIM] / a1[HEAD_DIM:HEAD_DIM + 1] - lam * (a2[:HEAD_DIM] / a2[HEAD_DIM:HEAD_DIM + 1])
            r = lax.rsqrt(jnp.mean(o * o, axis=0, keepdims=True) + EPS)
            outs.append(o * r * subln_ref[...] * (1.0 - lambda_init))
        return jnp.concatenate(outs, axis=0)

    _pipelined_softmax_pv(n_tiles, 2 * N_HEADS, n_tiles, score_chunks, value_chunk, combine, s_scr, o_ref,
                          exp_fn=jnp.exp)


def _diff_attention(proj, lq1, lk1, lq2, lk2, subln, lambda_init):
    b = proj.shape[0]
    vec = _const_spec((1, DIFF_QK_DIM))
    n_tiles = SEQ // Q_TILE
    rel = np.arange(Q_TILE)
    over = -2.0 * np.maximum(rel[:, None] - rel[None, :], 0).astype(np.float32)
    diag = jnp.asarray(np.stack([_alibi_slope(h) * over for h in range(N_HEADS)]))
    subln_cols = jnp.broadcast_to(subln.reshape(HEAD_DIM, 1), (HEAD_DIM, Q_TILE))
    return pl.pallas_call(
        functools.partial(_diff_kernel, lambda_init),
        out_shape=jax.ShapeDtypeStruct((b, SEQ, GROUP_WIDTH), BF16),
        grid=(b,),
        in_specs=[
            _proj_spec(256, B256_DF_Q), _proj_spec(256, B256_DF_K), _proj_spec(256, B256_DF_V),
            vec, vec, vec, vec, _const_spec((HEAD_DIM, Q_TILE)), _const_spec(diag.shape),
        ],
        out_specs=pl.BlockSpec((1, SEQ, GROUP_WIDTH), lambda i: (i, 0, 0)),
        scratch_shapes=[pltpu.VMEM((n_tiles, 2 * N_HEADS, 2 * HEAD_DIM, Q_TILE), BF16),
                        pltpu.VMEM((n_tiles, N_HEADS, 32, Q_TILE), BF16),
                        pltpu.VMEM((N_HEADS // 2, SEQ, 4 * HEAD_DIM), BF16),
                        pltpu.VMEM((N_HEADS, n_tiles, HEAD_DIM + 16, Q_TILE), BF16),
                        pltpu.VMEM((2, SEQ, Q_TILE), F32)],
        compiler_params=_compiler_params(),
        name="diff_attention",
    )(proj, proj, proj, lq1, lk1, lq2, lk2, subln_cols, diag)


def _gqa_kernel(q_ref, qr_ref, k_ref, kr_ref, v_ref, gq_ref, gqp_ref, gk_ref, gkp_ref, cos_ref, sin_ref, o_ref,
                qt_scr, k_scr, vt_scr, s_scr):
    q_scale = HEAD_DIM ** -0.5 * LOG2E
    kv_heads = N_HEADS // 2
    n_tiles = SEQ // Q_TILE
    pair = 2 * HEAD_DIM

    def normed_rotary(x, xr, g, gp, cos, sin):
        r = lax.rsqrt(_segment_mean_square(x, HEAD_DIM) + EPS)
        return (x * r * g) * cos + (xr * r * gp) * sin

    def prep(ci, carry):
        r0 = pl.multiple_of(ci * Q_TILE, Q_TILE)
        rows = pl.ds(r0, Q_TILE)
        cos, sin = cos_ref[rows, :], sin_ref[rows, :]
        zeros = jnp.zeros((HEAD_DIM, Q_TILE), BF16)
        for half in range(2):
            lanes = slice(half * pair, (half + 1) * pair)
            y = normed_rotary(q_ref[0, rows, lanes], qr_ref[0, rows, lanes], gq_ref[...], gqp_ref[...], cos, sin)
            yt = (y * q_scale).T.astype(BF16)
            for j in range(2):
                h = 2 * half + j
                kv = h // (N_HEADS // kv_heads)
                for part in range(kv_heads):
                    block = yt[j * HEAD_DIM:(j + 1) * HEAD_DIM] if part == kv else zeros
                    qt_scr[ci, h * pair + part * HEAD_DIM:h * pair + (part + 1) * HEAD_DIM, :] = block
        k_scr[rows, :] = normed_rotary(k_ref[0, rows, :], kr_ref[0, rows, :], gk_ref[...], gkp_ref[...], cos,
                                       sin).astype(BF16)
        vt = v_ref[0, rows, :].T.astype(BF16)
        for kv in range(kv_heads):
            vt_scr[kv, ci, 0:HEAD_DIM, :] = vt[kv * HEAD_DIM:(kv + 1) * HEAD_DIM]
            vt_scr[kv, ci, HEAD_DIM:, :] = _ones_row_block(Q_TILE)
        return carry

    lax.fori_loop(0, n_tiles, prep, 0)

    def score_chunks(qi, h):
        qt = qt_scr[qi, h * pair:(h + 1) * pair, :]
        return [_dot(k_scr[c * Q_TILE:(c + 1) * Q_TILE, :], qt) for c in range(n_tiles)]

    def value_chunk(qi, h, c):
        return vt_scr[h // (N_HEADS // kv_heads), c]

    _pipelined_softmax_pv(n_tiles, N_HEADS, n_tiles, score_chunks, value_chunk, _normalised_heads, s_scr, o_ref)


def _gqa_attention(proj, gq, gqp, gk, gkp, cos, sin):
    b = proj.shape[0]
    vec = _const_spec((1, 2 * HEAD_DIM))
    tab = _const_spec((SEQ, 2 * HEAD_DIM))
    return pl.pallas_call(
        _gqa_kernel,
        out_shape=jax.ShapeDtypeStruct((b, SEQ, GROUP_WIDTH), BF16),
        grid=(b,),
        in_specs=[
            _proj_spec(256, B256_GQ_Q), _proj_spec(256, B256_GQ_QR),
            _proj_spec(128, B128_GQ_K), _proj_spec(128, B128_GQ_KR), _proj_spec(128, B128_GQ_V),
            vec, vec, vec, vec, tab, tab,
        ],
        out_specs=pl.BlockSpec((1, SEQ, GROUP_WIDTH), lambda i: (i, 0, 0)),
        scratch_shapes=[pltpu.VMEM((SEQ // Q_TILE, N_HEADS * 2 * HEAD_DIM, Q_TILE), BF16),
                        pltpu.VMEM((SEQ, 2 * HEAD_DIM), BF16),
                        pltpu.VMEM((N_HEADS // 2, SEQ // Q_TILE, HEAD_DIM + 16, Q_TILE), BF16),
                        pltpu.VMEM((2, SEQ, Q_TILE), F32)],
        compiler_params=_compiler_params(),
        name="gqa_attention",
    )(proj, proj, proj, proj, proj, gq, gqp, gk, gkp, cos, sin)


def _mla_kernel(cq_ref, ckv_ref, kr_ref, krr_ref, gq_ref, gkv_ref, wuq_ref, wukv_ref, cos_ref, sin_ref, o_ref,
                qt_scr, k_scr, vt_scr, s_scr):
    q_scale = (MLA_NOPE_DIM + MLA_ROPE_DIM) ** -0.5 * LOG2E
    n_tiles = SEQ // Q_TILE
    rot0 = N_HEADS * 128
    v0 = N_HEADS * 128

    def prep(ci, carry):
        r0 = pl.multiple_of(ci * Q_TILE, Q_TILE)
        rows = pl.ds(r0, Q_TILE)
        cos, sin = cos_ref[rows, :], sin_ref[rows, :]
        cq = cq_ref[0, rows, :]
        cqn = (cq * _rms(cq) * gq_ref[...]).astype(BF16)
        ckv = ckv_ref[0, rows, :]
        ckvn = (ckv * _rms(ckv) * gkv_ref[...]).astype(BF16)
        k_rope = kr_ref[0, rows, :] * cos + krr_ref[0, rows, :] * sin
        for h in range(N_HEADS):
            cols = slice(h * 128, (h + 1) * 128)
            rot_cols = slice(rot0 + h * 128, rot0 + (h + 1) * 128)
            qh = _dot(cqn, wuq_ref[:, cols]) * cos + _dot(cqn, wuq_ref[:, rot_cols]) * sin
            qt_scr[ci, h * 128:(h + 1) * 128, :] = (qh * q_scale).T.astype(BF16)
            k_scr[h, rows, :] = (_dot(ckvn, wukv_ref[:, cols]) + k_rope).astype(BF16)
        vt = _dot(ckvn, wukv_ref[:, v0:v0 + GROUP_WIDTH]).T.astype(BF16)
        for h in range(N_HEADS):
            vt_scr[h, ci, 0:HEAD_DIM, :] = vt[h * HEAD_DIM:(h + 1) * HEAD_DIM]
            vt_scr[h, ci, HEAD_DIM:, :] = _ones_row_block(Q_TILE)
        return carry

    lax.fori_loop(0, n_tiles, prep, 0)

    def score_chunks(qi, h):
        qt = qt_scr[qi, h * 128:(h + 1) * 128, :]
        return [_dot(k_scr[h, c * Q_TILE:(c + 1) * Q_TILE, :], qt) for c in range(n_tiles)]

    def value_chunk(qi, h, c):
        return vt_scr[h, c]

    _pipelined_softmax_pv(n_tiles, N_HEADS, n_tiles, score_chunks, value_chunk, _normalised_heads, s_scr, o_ref)


def _mla_attention(proj, gq, gkv, wuq, wukv, cos, sin):
    b = proj.shape[0]
    tab = _const_spec((SEQ, 128))
    return pl.pallas_call(
        _mla_kernel,
        out_shape=jax.ShapeDtypeStruct((b, SEQ, GROUP_WIDTH), BF16),
        grid=(b,),
        in_specs=[
            _proj_spec(256, B256_ML_CQ), _proj_spec(128, B128_ML_CKV),
            _proj_spec(128, B128_ML_KR), _proj_spec(128, B128_ML_KRR),
            _const_spec((1, GROUP_WIDTH)), _const_spec((1, MLA_KV_LORA)),
            _const_spec(wuq.shape), _const_spec(wukv.shape), tab, tab,
        ],
        out_specs=pl.BlockSpec((1, SEQ, GROUP_WIDTH), lambda i: (i, 0, 0)),
        scratch_shapes=[pltpu.VMEM((SEQ // Q_TILE, N_HEADS * 128, Q_TILE), BF16),
                        pltpu.VMEM((N_HEADS, SEQ, 128), BF16),
                        pltpu.VMEM((N_HEADS, SEQ // Q_TILE, HEAD_DIM + 16, Q_TILE), BF16),
                        pltpu.VMEM((2, SEQ, Q_TILE), F32)],
        compiler_params=_compiler_params(),
        name="mla_attention",
    )(proj, proj, proj, proj, gq, gkv, wuq, wukv, cos, sin)


def _post_kernel(x_ref, a_ref, b_ref, c_ref, d_ref, wo_ref, g_mix_ref, g_pre_ref, wg_ref, wu_ref, wd_ref,
                 g_ffn_ref, o_ref, acc_ref):
    mix = (_dot(a_ref[...], wo_ref[0]) + _dot(b_ref[...], wo_ref[1])
           + _dot(c_ref[...], wo_ref[2]) + _dot(d_ref[...], wo_ref[3]))
    x = x_ref[...] + mix * _rms(mix) * g_mix_ref[...]
    h = (x * _rms(x) * g_pre_ref[...]).astype(BF16)
    acc_ref[...] = jnp.zeros_like(acc_ref)

    def body(ci, carry):
        gate = _dot(h, wg_ref[ci])
        up = _dot(h, wu_ref[ci])
        act = (gate * jax.nn.sigmoid(gate) * up).astype(BF16)
        acc_ref[...] += _dot(act, wd_ref[ci])
        return carry

    lax.fori_loop(0, D_FF // FF_CHUNK, body, 0)
    f = acc_ref[...]
    o_ref[...] = x + f * _rms(f) * g_ffn_ref[...]


def _post(x2d, a, b, c, d, wo, g_mix, g_pre, wg, wu, wd, g_ffn):
    t = x2d.shape[0]
    tok = lambda w: pl.BlockSpec((TOKEN_TILE, w), lambda i: (i, 0))
    vec = _const_spec((1, D_MODEL))
    return pl.pallas_call(
        _post_kernel,
        out_shape=jax.ShapeDtypeStruct((t, D_MODEL), F32),
        grid=(t // TOKEN_TILE,),
        in_specs=[
            tok(D_MODEL), tok(GROUP_WIDTH), tok(GROUP_WIDTH), tok(GROUP_WIDTH), tok(GROUP_WIDTH),
            _const_spec(wo.shape), vec, vec, _const_spec(wg.shape), _const_spec(wu.shape), _const_spec(wd.shape),
            vec,
        ],
        out_specs=tok(D_MODEL),
        scratch_shapes=[pltpu.VMEM((TOKEN_TILE, D_MODEL), F32)],
        compiler_params=_compiler_params(),
        name="outproj_swiglu",
    )(x2d, a, b, c, d, wo, g_mix, g_pre, wg, wu, wd, g_ffn)


def _layout_w_in(w):
    src64, sign64 = _rot_perm(HEAD_DIM, HEAD_DIM // 2)
    src_q = np.concatenate([h * HEAD_DIM + src64 for h in range(N_HEADS)])
    src_k = src_q[: 2 * HEAD_DIM]
    src32, sign32 = _rot_perm(MLA_ROPE_DIM, MLA_ROPE_DIM)
    gq_q = w[:, O_GQ_Q:O_GQ_K]
    gq_k = w[:, O_GQ_K:O_GQ_V]
    k_rope = w[:, O_ML_KR:O_ML_KR + MLA_ROPE_DIM]
    pad = lambda m: jnp.pad(m, ((0, 0), (MLA_NOPE_DIM, 128 - MLA_NOPE_DIM - MLA_ROPE_DIM)))
    cols = [
        w[:, :O_GQ_K],
        gq_q[:, src_q] * np.tile(sign64, N_HEADS),
        w[:, O_ML_CQ:O_ML_CKV],
        gq_k, gq_k[:, src_k] * np.tile(sign64, 2),
        w[:, O_GQ_V:O_ML_CQ],
        w[:, O_ML_CKV:O_ML_KR],
        pad(k_rope), pad(k_rope[:, src32] * sign32),
    ]
    return jnp.concatenate(cols, axis=1).astype(BF16)


def _layout_mla_weights(w_uq, w_ukv):
    src32, sign32 = _rot_perm(MLA_ROPE_DIM, MLA_ROPE_DIM)
    wq = w_uq.reshape(GROUP_WIDTH, N_HEADS, MLA_NOPE_DIM + MLA_ROPE_DIM)
    rope = wq[:, :, MLA_NOPE_DIM:]
    tail = 128 - MLA_NOPE_DIM - MLA_ROPE_DIM
    q_main = jnp.pad(wq, ((0, 0), (0, 0), (0, tail)))
    q_rot = jnp.pad(rope[:, :, src32] * sign32, ((0, 0), (0, 0), (MLA_NOPE_DIM, tail)))
    wuq = jnp.concatenate([q_main.reshape(GROUP_WIDTH, -1), q_rot.reshape(GROUP_WIDTH, -1)], axis=1)
    wkv = w_ukv.reshape(MLA_KV_LORA, N_HEADS, MLA_NOPE_DIM + HEAD_DIM)
    k_nope = jnp.pad(wkv[:, :, :MLA_NOPE_DIM], ((0, 0), (0, 0), (0, 128 - MLA_NOPE_DIM)))
    vals = wkv[:, :, MLA_NOPE_DIM:]
    wukv = jnp.concatenate([k_nope.reshape(MLA_KV_LORA, -1), vals.reshape(MLA_KV_LORA, -1)], axis=1)
    return wuq.astype(BF16), wukv.astype(BF16)


def _rotary_tables():
    pos = jnp.arange(SEQ)
    half = HEAD_DIM // 2
    inv = ROPE_THETA ** (-jnp.arange(0, half, 2, dtype=F32) / half)

    def angles(p):
        ang = p.astype(F32)[:, None] * inv[None, :]
        return jnp.concatenate([ang, ang], axis=-1)

    axial = jnp.concatenate([angles(pos // GRID_W), angles(pos % GRID_W)], axis=-1)
    seq = angles(pos)
    tail = 128 - MLA_NOPE_DIM - MLA_ROPE_DIM
    mla_cos = jnp.concatenate([jnp.ones((SEQ, MLA_NOPE_DIM), F32), jnp.cos(seq), jnp.zeros((SEQ, tail), F32)], -1)
    mla_sin = jnp.pad(jnp.sin(seq), ((0, 0), (MLA_NOPE_DIM, tail)))
    pair = lambda t: jnp.tile(t, (1, 2))
    return pair(jnp.cos(axial)), pair(jnp.sin(axial)), mla_cos, mla_sin


def kernel(x, pre_mix_norm, w_in, na_rel_bias, diff_lambda_q1, diff_lambda_k1, diff_lambda_q2, diff_lambda_k2,
           diff_subln, gqa_q_norm, gqa_k_norm, mla_q_norm, mla_kv_norm, mla_w_uq, mla_w_ukv, w_o, post_mix_norm,
           pre_ffn_norm, ffn_w_gate_up, ffn_w_down, post_ffn_norm):
    b, s, d = x.shape
    assert (s, d) == (SEQ, D_MODEL)
    depth = w_in.shape[0]
    src64, _ = _rot_perm(HEAD_DIM, HEAD_DIM // 2)
    ax_cos, ax_sin, mla_cos, mla_sin = _rotary_tables()
    row = lambda v: v.reshape(1, -1).astype(F32)
    n_chunks = D_FF // FF_CHUNK

    x2d = x.reshape(b * s, d)
    for l in range(depth):
        lambda_init = 0.8 - 0.6 * math.exp(-0.3 * l)
        proj = _inproj(x2d, row(pre_mix_norm[l]), _layout_w_in(w_in[l])).reshape(b, s, PROJ_WIDTH)

        a_out = _na_attention(proj, _na_tables(na_rel_bias[l]))
        b_out = _diff_attention(proj, row(diff_lambda_q1[l]), row(diff_lambda_k1[l]), row(diff_lambda_q2[l]),
                                row(diff_lambda_k2[l]), row(diff_subln[l]), lambda_init)
        pair_row = lambda v: row(jnp.tile(v, 2))
        c_out = _gqa_attention(proj, pair_row(gqa_q_norm[l]), pair_row(gqa_q_norm[l][src64]),
                               pair_row(gqa_k_norm[l]), pair_row(gqa_k_norm[l][src64]), ax_cos, ax_sin)
        wuq, wukv = _layout_mla_weights(mla_w_uq[l], mla_w_ukv[l])
        d_out = _mla_attention(proj, row(mla_q_norm[l]), row(mla_kv_norm[l]), wuq, wukv, mla_cos, mla_sin)

        gate_up = ffn_w_gate_up[l].astype(BF16)
        wg = gate_up[:, :D_FF].reshape(d, n_chunks, FF_CHUNK).transpose(1, 0, 2)
        wu = gate_up[:, D_FF:].reshape(d, n_chunks, FF_CHUNK).transpose(1, 0, 2)
        wd = ffn_w_down[l].astype(BF16).reshape(n_chunks, FF_CHUNK, d)
        wo = w_o[l].astype(BF16).reshape(N_HEADS, GROUP_WIDTH, d)
        flat = lambda t: t.reshape(b * s, GROUP_WIDTH)
        x2d = _post(x2d, flat(a_out), flat(b_out), flat(c_out), flat(d_out), wo, row(post_mix_norm[l]),
                    row(pre_ffn_norm[l]), wg, wu, wd, row(post_ffn_norm[l]))
    return x2d.reshape(b, s, d)
```

```python
import functools
import math

import numpy as np
import jax
import jax.numpy as jnp
from jax import lax
from jax.experimental import pallas as pl
from jax.experimental.pallas import tpu as pltpu

F32 = jnp.float32
BF16 = jnp.bfloat16

D_MODEL = 1024
SEQ = 2048
GRID_W = 64
GRID_ROWS = SEQ // GRID_W
HEAD_DIM = 64
N_HEADS = 4
GROUP_WIDTH = 256
EPS = 1e-6
ROPE_THETA = 10000.0

NA_WIN_ROWS = 8
NA_WIN_COLS = 16
DIFF_QK_DIM = 32
POS_SPLIT = 64
MLA_NOPE_DIM = 64
MLA_ROPE_DIM = 32
MLA_KV_LORA = 128
D_FF = 2816

O_GQ_Q, O_GQ_K, O_GQ_V, O_ML_CQ, O_ML_CKV, O_ML_KR = 1536, 1792, 1920, 2048, 2304, 2432

PROJ_WIDTH = 3072
B256_NA_Q, B256_NA_K, B256_NA_V, B256_DF_Q, B256_DF_K, B256_DF_V, B256_GQ_Q, B256_GQ_QR, B256_ML_CQ = range(9)
B128_GQ_K, B128_GQ_KR, B128_GQ_V, B128_ML_CKV, B128_ML_KR, B128_ML_KRR = range(18, 24)

VMEM_LIMIT_BYTES = 56 * 1024 * 1024

TOKEN_TILE = 1024
FF_CHUNK = 256
Q_TILE = 256
NA_GROUP_ROWS = 4
NA_BAND_ROWS = 12
NEG_BIG = -1e30
LOG2E = math.log2(math.e)


def _rot_perm(width, group):
    j = np.arange(width)
    jj = j % group
    half = group // 2
    src = (j // group) * group + (jj + half) % group
    sign = np.where(jj < half, -1.0, 1.0).astype(np.float32)
    return src, sign


def _compiler_params():
    return pltpu.CompilerParams(dimension_semantics=("arbitrary",), vmem_limit_bytes=VMEM_LIMIT_BYTES)


def _const_spec(shape):
    zeros = (0,) * len(shape)
    return pl.BlockSpec(shape, lambda i: zeros, pipeline_mode=pl.Buffered(1))


def _rms(x):
    return lax.rsqrt(jnp.mean(x * x, axis=-1, keepdims=True) + EPS)


def _dot(a, b):
    return jnp.dot(a, b, preferred_element_type=F32)


def _dot_nt(a, b):
    return lax.dot_general(a, b, (((1,), (1,)), ((), ())), preferred_element_type=F32)


def _inproj_kernel(x_ref, g_ref, w_ref, o_ref):
    x = x_ref[...]
    h = (x * _rms(x) * g_ref[...]).astype(BF16)
    o_ref[...] = _dot(h, w_ref[...])


def _inproj(x2d, gain, w):
    t = x2d.shape[0]
    return pl.pallas_call(
        _inproj_kernel,
        out_shape=jax.ShapeDtypeStruct((t, PROJ_WIDTH), F32),
        grid=(t // TOKEN_TILE,),
        in_specs=[
            pl.BlockSpec((TOKEN_TILE, D_MODEL), lambda i: (i, 0)),
            _const_spec((1, D_MODEL)),
            _const_spec((D_MODEL, PROJ_WIDTH)),
        ],
        out_specs=pl.BlockSpec((TOKEN_TILE, PROJ_WIDTH), lambda i: (i, 0)),
        compiler_params=_compiler_params(),
        name="inproj",
    )(x2d, gain, w)


def _softmax_unnormalised(s):
    m = jnp.max(s, axis=-1, keepdims=True)
    p = jnp.exp(s - m)
    return p, jnp.sum(p, axis=-1, keepdims=True)


def _segment_mean_square(x, seg):
    w = x.shape[-1]
    same = (lax.broadcasted_iota(jnp.int32, (w, w), 0) // seg) == (lax.broadcasted_iota(jnp.int32, (w, w), 1) // seg)
    ones = jnp.where(same, 1.0, 0.0).astype(BF16)
    sq = x * x
    hi = sq.astype(BF16)
    lo = (sq - hi.astype(F32)).astype(BF16)
    return (_dot(hi, ones) + _dot(lo, ones)) * (1.0 / seg)


def _ones_row_block(width):
    return jnp.where(lax.broadcasted_iota(jnp.int32, (16, width), 0) == 0, 1.0, 0.0).astype(BF16)


def _softmax_pv_transposed(sts, vts):
    m = functools.reduce(jnp.maximum, sts)
    m = jnp.max(m, axis=0, keepdims=True)
    acc = None
    for st, vt in zip(sts, vts):
        part = _dot(vt, jnp.exp2(st - m).astype(BF16))
        acc = part if acc is None else acc + part
    return acc[:HEAD_DIM] / acc[HEAD_DIM:HEAD_DIM + 1]


def _normalised_heads(accs):
    return jnp.concatenate([acc[:HEAD_DIM] / acc[HEAD_DIM:HEAD_DIM + 1] for acc in accs], axis=0)


def _pipelined_softmax_pv(n_tiles, n_heads, n_chunks, score_chunks, value_chunk, combine, s_scr, acc_scr, o_ref,
                          exp_fn=jnp.exp2):
    assert n_heads % 2 == 0
    rows = s_scr.shape[1] // n_chunks

    def scores_into(slot, qi, h):
        m8 = None
        for c, st in enumerate(score_chunks(qi, h)):
            s_scr[slot, c * rows:(c + 1) * rows, :] = st
            cm = jnp.max(st.reshape(-1, 8, st.shape[-1]), axis=0)
            m8 = cm if m8 is None else jnp.maximum(m8, cm)
        return jnp.max(m8, axis=0, keepdims=True)

    def write_tile(qi):
        r0 = qi * Q_TILE if isinstance(qi, int) else pl.multiple_of(qi * Q_TILE, Q_TILE)
        o_ref[0, pl.ds(r0, Q_TILE), :] = combine([acc_scr[h] for h in range(n_heads)]).T.astype(o_ref.dtype)

    acc_scr[...] = jnp.ones_like(acc_scr)

    def body(qi, m):
        next_qi = jnp.minimum(qi + 1, n_tiles - 1)
        for h in range(n_heads):
            slot = h % 2
            nq, nh = (qi, h + 1) if h + 1 < n_heads else (next_qi, 0)
            next_m = scores_into(1 - slot, nq, nh)
            if h == 0:
                write_tile(jnp.maximum(qi - 1, 0))
            acc = None
            for c in range(n_chunks):
                p = exp_fn(s_scr[slot, c * rows:(c + 1) * rows, :] - m).astype(BF16)
                part = _dot(value_chunk(qi, h, c), p)
                acc = part if acc is None else acc + part
            acc_scr[h] = acc
            m = next_m
        return m

    lax.fori_loop(0, n_tiles, body, scores_into(0, 0, 0))
    write_tile(n_tiles - 1)


def _proj_spec(width, block):
    return pl.BlockSpec((1, SEQ, width), lambda b: (b, 0, block))


def _na_kernel(q_ref, k_ref, v_ref, tab_ref, o_ref, qt_scr, k_scr, vt_scr, s_scr, acc_scr):
    assert NA_GROUP_ROWS * GRID_W == Q_TILE and NA_BAND_ROWS % NA_GROUP_ROWS == 0
    n_tiles = SEQ // Q_TILE
    band_chunks = NA_BAND_ROWS // NA_GROUP_ROWS
    pair = 2 * HEAD_DIM
    q_scale = HEAD_DIM ** -0.5 * LOG2E

    def prep(ci, carry):
        r0 = pl.multiple_of(ci * Q_TILE, Q_TILE)
        rows = pl.ds(r0, Q_TILE)
        qt = (q_ref[0, rows, :] * q_scale).T
        row = lax.broadcasted_iota(jnp.int32, (pair, Q_TILE), 0)
        for h in range(N_HEADS):
            own = (row >= (h % 2) * HEAD_DIM) & (row < (h % 2 + 1) * HEAD_DIM)
            qt_scr[ci, h] = jnp.where(own, qt[(h // 2) * pair:(h // 2 + 1) * pair], 0.0).astype(BF16)
        for p in range(N_HEADS // 2):
            k_scr[p, rows, :] = k_ref[0, rows, p * pair:(p + 1) * pair].astype(BF16)
        vt = v_ref[0, rows, :].T.astype(BF16)
        for h in range(N_HEADS):
            vt_scr[h, ci, 0:HEAD_DIM, :] = vt[h * HEAD_DIM:(h + 1) * HEAD_DIM]
            vt_scr[h, ci, HEAD_DIM:, :] = _ones_row_block(Q_TILE)
        return carry

    lax.fori_loop(0, n_tiles, prep, 0)

    def key_chunk(gi, c):
        return jnp.clip(gi - (NA_WIN_ROWS // 2) // NA_GROUP_ROWS, 0, n_tiles - band_chunks) + c

    def score_chunks(gi, h):
        kind = jnp.where(gi == 0, 0, jnp.where(gi == n_tiles - 1, 2, 1))
        k0 = pl.multiple_of(key_chunk(gi, 0) * Q_TILE, Q_TILE)
        return [_dot(k_scr[h // 2, pl.ds(k0 + c * Q_TILE, Q_TILE), :], qt_scr[gi, h])
                + tab_ref[kind, h, c * Q_TILE:(c + 1) * Q_TILE, :] for c in range(band_chunks)]

    def value_chunk(gi, h, c):
        return vt_scr[h, key_chunk(gi, c)]

    _pipelined_softmax_pv(n_tiles, N_HEADS, band_chunks, score_chunks, value_chunk, _normalised_heads, s_scr, acc_scr,
                          o_ref)


def _na_tables(rel_bias):
    a = np.arange(NA_GROUP_ROWS)[:, None, None, None]
    c = np.arange(GRID_W)[None, :, None, None]
    i = np.arange(NA_BAND_ROWS)[None, None, :, None]
    kc = np.arange(GRID_W)[None, None, None, :]
    cs = np.clip(c - NA_WIN_COLS // 2, 0, GRID_W - NA_WIN_COLS)
    col_ok = (kc >= cs) & (kc < cs + NA_WIN_COLS)
    dc = kc - c + NA_WIN_COLS - 1
    col_sel = (dc[..., None] == np.arange(2 * NA_WIN_COLS - 1)) & col_ok[..., None]
    row_sels = []
    last_r0 = GRID_ROWS - NA_GROUP_ROWS
    for r0, band_row in ((0, 0), (NA_GROUP_ROWS, 0), (last_r0, GRID_ROWS - NA_BAND_ROWS)):
        r = r0 + a
        rs = np.clip(r - NA_WIN_ROWS // 2, 0, GRID_ROWS - NA_WIN_ROWS)
        key_row = band_row + i
        row_ok = (key_row >= rs) & (key_row < rs + NA_WIN_ROWS)
        dr = key_row - r + NA_WIN_ROWS - 1
        row_sels.append((dr[..., None] == np.arange(2 * NA_WIN_ROWS - 1)) & row_ok[..., None])
    row_sel = np.stack(row_sels)[:, :, 0, :, 0, :].astype(np.float32)
    col_sel = col_sel[0, :, 0, :, :].astype(np.float32)
    vals = jnp.einsum("taiu,huv,ckv->thikac", row_sel, rel_bias.astype(F32), col_sel,
                      precision=lax.Precision.HIGHEST)
    inside = np.einsum("taiu,ckv->tikac", row_sel, col_sel) > 0
    tab = jnp.where(inside[:, None], vals * LOG2E, NEG_BIG)
    q_rows, band = NA_GROUP_ROWS * GRID_W, NA_BAND_ROWS * GRID_W
    return tab.reshape(3, N_HEADS, band, q_rows)


def _na_attention(proj, tables):
    b = proj.shape[0]
    return pl.pallas_call(
        _na_kernel,
        out_shape=jax.ShapeDtypeStruct((b, SEQ, GROUP_WIDTH), BF16),
        grid=(b,),
        in_specs=[
            _proj_spec(256, B256_NA_Q), _proj_spec(256, B256_NA_K), _proj_spec(256, B256_NA_V),
            _const_spec(tables.shape),
        ],
        out_specs=pl.BlockSpec((1, SEQ, GROUP_WIDTH), lambda i: (i, 0, 0)),
        scratch_shapes=[pltpu.VMEM((SEQ // Q_TILE, N_HEADS, 2 * HEAD_DIM, Q_TILE), BF16),
                        pltpu.VMEM((N_HEADS // 2, SEQ, 2 * HEAD_DIM), BF16),
                        pltpu.VMEM((N_HEADS, SEQ // Q_TILE, HEAD_DIM + 16, Q_TILE), BF16),
                        pltpu.VMEM((2, NA_BAND_ROWS * GRID_W, Q_TILE), F32),
                        pltpu.VMEM((N_HEADS, HEAD_DIM + 16, Q_TILE), F32)],
        compiler_params=_compiler_params(),
        name="na_attention",
    )(proj, proj, proj, tables)


def _alibi_slope(h):
    return 2.0 ** (-8.0 * (h + 1) / N_HEADS)


def _diff_kernel(lambda_init, q_ref, k_ref, v_ref, lq1_ref, lk1_ref, lq2_ref, lk2_ref, subln_ref, diag_ref, o_ref,
                 qt_scr, qf_scr, k_scr, vt_scr, s_scr, acc_scr):
    scale = DIFF_QK_DIM ** -0.5
    n_tiles = SEQ // Q_TILE
    pair = 2 * HEAD_DIM
    n_feat = 16
    lam = (jnp.exp(jnp.sum(lq1_ref[...] * lk1_ref[...], axis=-1, keepdims=True))
           - jnp.exp(jnp.sum(lq2_ref[...] * lk2_ref[...], axis=-1, keepdims=True)) + lambda_init)

    def prep(ci, carry):
        r0 = pl.multiple_of(ci * Q_TILE, Q_TILE)
        rows = pl.ds(r0, Q_TILE)
        qt = (q_ref[0, rows, :] * scale).T
        row = lax.broadcasted_iota(jnp.int32, (pair, Q_TILE), 0)
        feat_row = lax.broadcasted_iota(jnp.int32, (n_feat, Q_TILE), 0)
        i = r0 + lax.broadcasted_iota(jnp.int32, (n_feat, Q_TILE), 1)
        i_lo = i % POS_SPLIT
        i_hi = i - i_lo
        for h in range(N_HEADS):
            group = qt[(h // 2) * pair:(h // 2 + 1) * pair]
            for mp in range(2):
                lo = (h % 2) * HEAD_DIM + mp * DIFF_QK_DIM
                keep = (row >= lo) & (row < lo + DIFF_QK_DIM)
                qt_scr[ci, 2 * h + mp] = jnp.where(keep, group, 0.0).astype(BF16)
            slope = _alibi_slope(h)
            feat = jnp.where(feat_row < 2, slope,
                             jnp.where(feat_row == 2, -slope * i_hi.astype(F32),
                                       jnp.where(feat_row == 3, -slope * i_lo.astype(F32), 0.0)))
            qf_scr[ci, h, 0:n_feat, :] = feat.astype(BF16)
            qf_scr[ci, h, n_feat:, :] = (-feat).astype(BF16)
        lane = lax.broadcasted_iota(jnp.int32, (Q_TILE, pair), 1)
        j = r0 + lax.broadcasted_iota(jnp.int32, (Q_TILE, pair), 0)
        j_lo = j % POS_SPLIT
        k_feat = jnp.where(lane == 0, (j - j_lo).astype(F32),
                           jnp.where(lane == 1, j_lo.astype(F32), jnp.where(lane < 4, 1.0, 0.0))).astype(BF16)
        for p in range(N_HEADS // 2):
            k_scr[p, rows, 0:pair] = k_ref[0, rows, p * pair:(p + 1) * pair].astype(BF16)
            k_scr[p, rows, pair:] = k_feat
        vt = v_ref[0, rows, :].T.astype(BF16)
        for h in range(N_HEADS):
            vt_scr[h, ci, 0:HEAD_DIM, :] = vt[h * HEAD_DIM:(h + 1) * HEAD_DIM]
            vt_scr[h, ci, HEAD_DIM:, :] = _ones_row_block(Q_TILE)
        return carry

    lax.fori_loop(0, n_tiles, prep, 0)

    def key_chunk(qi, c):
        wrapped = qi + c >= n_tiles
        return jnp.where(wrapped, qi + c - n_tiles, qi + c), wrapped

    def score_chunk(qi, item, c):
        h = item // 2
        kc, left = key_chunk(qi, c)
        form = 0 if c == 0 else pl.multiple_of(jnp.where(left, 0, n_feat), n_feat)
        feat = qf_scr[qi, h, pl.ds(form, n_feat), :]
        rhs = jnp.concatenate([qt_scr[qi, item], feat, jnp.zeros((pair - n_feat, Q_TILE), BF16)], axis=0)
        st = _dot(k_scr[h // 2, pl.ds(pl.multiple_of(kc * Q_TILE, Q_TILE), Q_TILE), :], rhs)
        return st + diag_ref[h] if c == 0 else st

    def score_chunks(qi, item):
        return [score_chunk(qi, item, c) for c in range(n_tiles)]

    def value_chunk(qi, item, c):
        return vt_scr[item // 2, key_chunk(qi, c)[0]]

    def combine(accs):
        outs = []
        for h in range(N_HEADS):
            a1, a2 = accs[2 * h], accs[2 * h + 1]
            o = a1[:HEAD_DIM] / a1[HEAD_DIM:HEAD_DIM + 1] - lam * (a2[:HEAD_DIM] / a2[HEAD_DIM:HEAD_DIM + 1])
            r = lax.rsqrt(jnp.mean(o * o, axis=0, keepdims=True) + EPS)
            outs.append(o * r * subln_ref[...] * (1.0 - lambda_init))
        return jnp.concatenate(outs, axis=0)

    _pipelined_softmax_pv(n_tiles, 2 * N_HEADS, n_tiles, score_chunks, value_chunk, combine, s_scr, acc_scr, o_ref,
                          exp_fn=jnp.exp)


def _diff_attention(proj, lq1, lk1, lq2, lk2, subln, lambda_init):
    b = proj.shape[0]
    vec = _const_spec((1, DIFF_QK_DIM))
    n_tiles = SEQ // Q_TILE
    rel = np.arange(Q_TILE)
    over = -2.0 * np.maximum(rel[:, None] - rel[None, :], 0).astype(np.float32)
    diag = jnp.asarray(np.stack([_alibi_slope(h) * over for h in range(N_HEADS)]))
    subln_cols = jnp.broadcast_to(subln.reshape(HEAD_DIM, 1), (HEAD_DIM, Q_TILE))
    return pl.pallas_call(
        functools.partial(_diff_kernel, lambda_init),
        out_shape=jax.ShapeDtypeStruct((b, SEQ, GROUP_WIDTH), BF16),
        grid=(b,),
        in_specs=[
            _proj_spec(256, B256_DF_Q), _proj_spec(256, B256_DF_K), _proj_spec(256, B256_DF_V),
            vec, vec, vec, vec, _const_spec((HEAD_DIM, Q_TILE)), _const_spec(diag.shape),
        ],
        out_specs=pl.BlockSpec((1, SEQ, GROUP_WIDTH), lambda i: (i, 0, 0)),
        scratch_shapes=[pltpu.VMEM((n_tiles, 2 * N_HEADS, 2 * HEAD_DIM, Q_TILE), BF16),
                        pltpu.VMEM((n_tiles, N_HEADS, 32, Q_TILE), BF16),
                        pltpu.VMEM((N_HEADS // 2, SEQ, 4 * HEAD_DIM), BF16),
                        pltpu.VMEM((N_HEADS, n_tiles, HEAD_DIM + 16, Q_TILE), BF16),
                        pltpu.VMEM((2, SEQ, Q_TILE), F32),
                        pltpu.VMEM((2 * N_HEADS, HEAD_DIM + 16, Q_TILE), F32)],
        compiler_params=_compiler_params(),
        name="diff_attention",
    )(proj, proj, proj, lq1, lk1, lq2, lk2, subln_cols, diag)


def _gqa_kernel(q_ref, qr_ref, k_ref, kr_ref, v_ref, gq_ref, gqp_ref, gk_ref, gkp_ref, cos_ref, sin_ref, o_ref,
                qt_scr, k_scr, vt_scr, s_scr, acc_scr):
    q_scale = HEAD_DIM ** -0.5 * LOG2E
    kv_heads = N_HEADS // 2
    n_tiles = SEQ // Q_TILE
    pair = 2 * HEAD_DIM

    def normed_rotary(x, xr, g, gp, cos, sin):
        r = lax.rsqrt(_segment_mean_square(x, HEAD_DIM) + EPS)
        return (x * r * g) * cos + (xr * r * gp) * sin

    def prep(ci, carry):
        r0 = pl.multiple_of(ci * Q_TILE, Q_TILE)
        rows = pl.ds(r0, Q_TILE)
        cos, sin = cos_ref[rows, :], sin_ref[rows, :]
        zeros = jnp.zeros((HEAD_DIM, Q_TILE), BF16)
        for half in range(2):
            lanes = slice(half * pair, (half + 1) * pair)
            y = normed_rotary(q_ref[0, rows, lanes], qr_ref[0, rows, lanes], gq_ref[...], gqp_ref[...], cos, sin)
            yt = (y * q_scale).T.astype(BF16)
            for j in range(2):
                h = 2 * half + j
                kv = h // (N_HEADS // kv_heads)
                for part in range(kv_heads):
                    block = yt[j * HEAD_DIM:(j + 1) * HEAD_DIM] if part == kv else zeros
                    qt_scr[ci, h * pair + part * HEAD_DIM:h * pair + (part + 1) * HEAD_DIM, :] = block
        k_scr[rows, :] = normed_rotary(k_ref[0, rows, :], kr_ref[0, rows, :], gk_ref[...], gkp_ref[...], cos,
                                       sin).astype(BF16)
        vt = v_ref[0, rows, :].T.astype(BF16)
        for kv in range(kv_heads):
            vt_scr[kv, ci, 0:HEAD_DIM, :] = vt[kv * HEAD_DIM:(kv + 1) * HEAD_DIM]
            vt_scr[kv, ci, HEAD_DIM:, :] = _ones_row_block(Q_TILE)
        return carry

    lax.fori_loop(0, n_tiles, prep, 0)

    def score_chunks(qi, h):
        qt = qt_scr[qi, h * pair:(h + 1) * pair, :]
        return [_dot(k_scr[c * Q_TILE:(c + 1) * Q_TILE, :], qt) for c in range(n_tiles)]

    def value_chunk(qi, h, c):
        return vt_scr[h // (N_HEADS // kv_heads), c]

    _pipelined_softmax_pv(n_tiles, N_HEADS, n_tiles, score_chunks, value_chunk, _normalised_heads, s_scr, acc_scr,
                          o_ref)


def _gqa_attention(proj, gq, gqp, gk, gkp, cos, sin):
    b = proj.shape[0]
    vec = _const_spec((1, 2 * HEAD_DIM))
    tab = _const_spec((SEQ, 2 * HEAD_DIM))
    return pl.pallas_call(
        _gqa_kernel,
        out_shape=jax.ShapeDtypeStruct((b, SEQ, GROUP_WIDTH), BF16),
        grid=(b,),
        in_specs=[
            _proj_spec(256, B256_GQ_Q), _proj_spec(256, B256_GQ_QR),
            _proj_spec(128, B128_GQ_K), _proj_spec(128, B128_GQ_KR), _proj_spec(128, B128_GQ_V),
            vec, vec, vec, vec, tab, tab,
        ],
        out_specs=pl.BlockSpec((1, SEQ, GROUP_WIDTH), lambda i: (i, 0, 0)),
        scratch_shapes=[pltpu.VMEM((SEQ // Q_TILE, N_HEADS * 2 * HEAD_DIM, Q_TILE), BF16),
                        pltpu.VMEM((SEQ, 2 * HEAD_DIM), BF16),
                        pltpu.VMEM((N_HEADS // 2, SEQ // Q_TILE, HEAD_DIM + 16, Q_TILE), BF16),
                        pltpu.VMEM((2, SEQ, Q_TILE), F32),
                        pltpu.VMEM((N_HEADS, HEAD_DIM + 16, Q_TILE), F32)],
        compiler_params=_compiler_params(),
        name="gqa_attention",
    )(proj, proj, proj, proj, proj, gq, gqp, gk, gkp, cos, sin)


def _mla_kernel(cq_ref, ckv_ref, kr_ref, krr_ref, gq_ref, gkv_ref, wuq_ref, wukv_ref, cos_ref, sin_ref, o_ref,
                qt_scr, k_scr, vt_scr, s_scr, acc_scr):
    q_scale = (MLA_NOPE_DIM + MLA_ROPE_DIM) ** -0.5 * LOG2E
    n_tiles = SEQ // Q_TILE
    rot0 = N_HEADS * 128
    v0 = N_HEADS * 128

    def prep(ci, carry):
        r0 = pl.multiple_of(ci * Q_TILE, Q_TILE)
        rows = pl.ds(r0, Q_TILE)
        cos, sin = cos_ref[rows, :], sin_ref[rows, :]
        cq = cq_ref[0, rows, :]
        cqn = (cq * _rms(cq) * gq_ref[...]).astype(BF16)
        ckv = ckv_ref[0, rows, :]
        ckvn = (ckv * _rms(ckv) * gkv_ref[...]).astype(BF16)
        k_rope = kr_ref[0, rows, :] * cos + krr_ref[0, rows, :] * sin
        for h in range(N_HEADS):
            cols = slice(h * 128, (h + 1) * 128)
            rot_cols = slice(rot0 + h * 128, rot0 + (h + 1) * 128)
            qh = _dot(cqn, wuq_ref[:, cols]) * cos + _dot(cqn, wuq_ref[:, rot_cols]) * sin
            qt_scr[ci, h * 128:(h + 1) * 128, :] = (qh * q_scale).T.astype(BF16)
            k_scr[h, rows, :] = (_dot(ckvn, wukv_ref[:, cols]) + k_rope).astype(BF16)
        vt = _dot(ckvn, wukv_ref[:, v0:v0 + GROUP_WIDTH]).T.astype(BF16)
        for h in range(N_HEADS):
            vt_scr[h, ci, 0:HEAD_DIM, :] = vt[h * HEAD_DIM:(h + 1) * HEAD_DIM]
            vt_scr[h, ci, HEAD_DIM:, :] = _ones_row_block(Q_TILE)
        return carry

    lax.fori_loop(0, n_tiles, prep, 0)

    def score_chunks(qi, h):
        qt = qt_scr[qi, h * 128:(h + 1) * 128, :]
        return [_dot(k_scr[h, c * Q_TILE:(c + 1) * Q_TILE, :], qt) for c in range(n_tiles)]

    def value_chunk(qi, h, c):
        return vt_scr[h, c]

    _pipelined_softmax_pv(n_tiles, N_HEADS, n_tiles, score_chunks, value_chunk, _normalised_heads, s_scr, acc_scr,
                          o_ref)


def _mla_attention(proj, gq, gkv, wuq, wukv, cos, sin):
    b = proj.shape[0]
    tab = _const_spec((SEQ, 128))
    return pl.pallas_call(
        _mla_kernel,
        out_shape=jax.ShapeDtypeStruct((b, SEQ, GROUP_WIDTH), BF16),
        grid=(b,),
        in_specs=[
            _proj_spec(256, B256_ML_CQ), _proj_spec(128, B128_ML_CKV),
            _proj_spec(128, B128_ML_KR), _proj_spec(128, B128_ML_KRR),
            _const_spec((1, GROUP_WIDTH)), _const_spec((1, MLA_KV_LORA)),
            _const_spec(wuq.shape), _const_spec(wukv.shape), tab, tab,
        ],
        out_specs=pl.BlockSpec((1, SEQ, GROUP_WIDTH), lambda i: (i, 0, 0)),
        scratch_shapes=[pltpu.VMEM((SEQ // Q_TILE, N_HEADS * 128, Q_TILE), BF16),
                        pltpu.VMEM((N_HEADS, SEQ, 128), BF16),
                        pltpu.VMEM((N_HEADS, SEQ // Q_TILE, HEAD_DIM + 16, Q_TILE), BF16),
                        pltpu.VMEM((2, SEQ, Q_TILE), F32),
                        pltpu.VMEM((N_HEADS, HEAD_DIM + 16, Q_TILE), F32)],
        compiler_params=_compiler_params(),
        name="mla_attention",
    )(proj, proj, proj, proj, gq, gkv, wuq, wukv, cos, sin)


def _post_kernel(x_ref, a_ref, b_ref, c_ref, d_ref, wo_ref, g_mix_ref, g_pre_ref, wg_ref, wu_ref, wd_ref,
                 g_ffn_ref, o_ref, acc_ref):
    mix = (_dot(a_ref[...], wo_ref[0]) + _dot(b_ref[...], wo_ref[1])
           + _dot(c_ref[...], wo_ref[2]) + _dot(d_ref[...], wo_ref[3]))
    x = x_ref[...] + mix * _rms(mix) * g_mix_ref[...]
    h = (x * _rms(x) * g_pre_ref[...]).astype(BF16)
    acc_ref[...] = jnp.zeros_like(acc_ref)

    def body(ci, carry):
        gate = _dot(h, wg_ref[ci])
        up = _dot(h, wu_ref[ci])
        act = (gate * jax.nn.sigmoid(gate) * up).astype(BF16)
        acc_ref[...] += _dot(act, wd_ref[ci])
        return carry

    lax.fori_loop(0, D_FF // FF_CHUNK, body, 0)
    f = acc_ref[...]
    o_ref[...] = x + f * _rms(f) * g_ffn_ref[...]


def _post(x2d, a, b, c, d, wo, g_mix, g_pre, wg, wu, wd, g_ffn):
    t = x2d.shape[0]
    tok = lambda w: pl.BlockSpec((TOKEN_TILE, w), lambda i: (i, 0))
    vec = _const_spec((1, D_MODEL))
    return pl.pallas_call(
        _post_kernel,
        out_shape=jax.ShapeDtypeStruct((t, D_MODEL), F32),
        grid=(t // TOKEN_TILE,),
        in_specs=[
            tok(D_MODEL), tok(GROUP_WIDTH), tok(GROUP_WIDTH), tok(GROUP_WIDTH), tok(GROUP_WIDTH),
            _const_spec(wo.shape), vec, vec, _const_spec(wg.shape), _const_spec(wu.shape), _const_spec(wd.shape),
            vec,
        ],
        out_specs=tok(D_MODEL),
        scratch_shapes=[pltpu.VMEM((TOKEN_TILE, D_MODEL), F32)],
        compiler_params=_compiler_params(),
        name="outproj_swiglu",
    )(x2d, a, b, c, d, wo, g_mix, g_pre, wg, wu, wd, g_ffn)


def _layout_w_in(w):
    src64, sign64 = _rot_perm(HEAD_DIM, HEAD_DIM // 2)
    src_q = np.concatenate([h * HEAD_DIM + src64 for h in range(N_HEADS)])
    src_k = src_q[: 2 * HEAD_DIM]
    src32, sign32 = _rot_perm(MLA_ROPE_DIM, MLA_ROPE_DIM)
    gq_q = w[:, O_GQ_Q:O_GQ_K]
    gq_k = w[:, O_GQ_K:O_GQ_V]
    k_rope = w[:, O_ML_KR:O_ML_KR + MLA_ROPE_DIM]
    pad = lambda m: jnp.pad(m, ((0, 0), (MLA_NOPE_DIM, 128 - MLA_NOPE_DIM - MLA_ROPE_DIM)))
    cols = [
        w[:, :O_GQ_K],
        gq_q[:, src_q] * np.tile(sign64, N_HEADS),
        w[:, O_ML_CQ:O_ML_CKV],
        gq_k, gq_k[:, src_k] * np.tile(sign64, 2),
        w[:, O_GQ_V:O_ML_CQ],
        w[:, O_ML_CKV:O_ML_KR],
        pad(k_rope), pad(k_rope[:, src32] * sign32),
    ]
    return jnp.concatenate(cols, axis=1).astype(BF16)


def _layout_mla_weights(w_uq, w_ukv):
    src32, sign32 = _rot_perm(MLA_ROPE_DIM, MLA_ROPE_DIM)
    wq = w_uq.reshape(GROUP_WIDTH, N_HEADS, MLA_NOPE_DIM + MLA_ROPE_DIM)
    rope = wq[:, :, MLA_NOPE_DIM:]
    tail = 128 - MLA_NOPE_DIM - MLA_ROPE_DIM
    q_main = jnp.pad(wq, ((0, 0), (0, 0), (0, tail)))
    q_rot = jnp.pad(rope[:, :, src32] * sign32, ((0, 0), (0, 0), (MLA_NOPE_DIM, tail)))
    wuq = jnp.concatenate([q_main.reshape(GROUP_WIDTH, -1), q_rot.reshape(GROUP_WIDTH, -1)], axis=1)
    wkv = w_ukv.reshape(MLA_KV_LORA, N_HEADS, MLA_NOPE_DIM + HEAD_DIM)
    k_nope = jnp.pad(wkv[:, :, :MLA_NOPE_DIM], ((0, 0), (0, 0), (0, 128 - MLA_NOPE_DIM)))
    vals = wkv[:, :, MLA_NOPE_DIM:]
    wukv = jnp.concatenate([k_nope.reshape(MLA_KV_LORA, -1), vals.reshape(MLA_KV_LORA, -1)], axis=1)
    return wuq.astype(BF16), wukv.astype(BF16)


def _rotary_tables():
    pos = jnp.arange(SEQ)
    half = HEAD_DIM // 2
    inv = ROPE_THETA ** (-jnp.arange(0, half, 2, dtype=F32) / half)

    def angles(p):
        ang = p.astype(F32)[:, None] * inv[None, :]
        return jnp.concatenate([ang, ang], axis=-1)

    axial = jnp.concatenate([angles(pos // GRID_W), angles(pos % GRID_W)], axis=-1)
    seq = angles(pos)
    tail = 128 - MLA_NOPE_DIM - MLA_ROPE_DIM
    mla_cos = jnp.concatenate([jnp.ones((SEQ, MLA_NOPE_DIM), F32), jnp.cos(seq), jnp.zeros((SEQ, tail), F32)], -1)
    mla_sin = jnp.pad(jnp.sin(seq), ((0, 0), (MLA_NOPE_DIM, tail)))
    pair = lambda t: jnp.tile(t, (1, 2))
    return pair(jnp.cos(axial)), pair(jnp.sin(axial)), mla_cos, mla_sin


def kernel(x, pre_mix_norm, w_in, na_rel_bias, diff_lambda_q1, diff_lambda_k1, diff_lambda_q2, diff_lambda_k2,
           diff_subln, gqa_q_norm, gqa_k_norm, mla_q_norm, mla_kv_norm, mla_w_uq, mla_w_ukv, w_o, post_mix_norm,
           pre_ffn_norm, ffn_w_gate_up, ffn_w_down, post_ffn_norm):
    b, s, d = x.shape
    assert (s, d) == (SEQ, D_MODEL)
    depth = w_in.shape[0]
    src64, _ = _rot_perm(HEAD_DIM, HEAD_DIM // 2)
    ax_cos, ax_sin, mla_cos, mla_sin = _rotary_tables()
    row = lambda v: v.reshape(1, -1).astype(F32)
    n_chunks = D_FF // FF_CHUNK

    x2d = x.reshape(b * s, d)
    for l in range(depth):
        lambda_init = 0.8 - 0.6 * math.exp(-0.3 * l)
        proj = _inproj(x2d, row(pre_mix_norm[l]), _layout_w_in(w_in[l])).reshape(b, s, PROJ_WIDTH)

        a_out = _na_attention(proj, _na_tables(na_rel_bias[l]))
        b_out = _diff_attention(proj, row(diff_lambda_q1[l]), row(diff_lambda_k1[l]), row(diff_lambda_q2[l]),
                                row(diff_lambda_k2[l]), row(diff_subln[l]), lambda_init)
        pair_row = lambda v: row(jnp.tile(v, 2))
        c_out = _gqa_attention(proj, pair_row(gqa_q_norm[l]), pair_row(gqa_q_norm[l][src64]),
                               pair_row(gqa_k_norm[l]), pair_row(gqa_k_norm[l][src64]), ax_cos, ax_sin)
        wuq, wukv = _layout_mla_weights(mla_w_uq[l], mla_w_ukv[l])
        d_out = _mla_attention(proj, row(mla_q_norm[l]), row(mla_kv_norm[l]), wuq, wukv, mla_cos, mla_sin)

        gate_up = ffn_w_gate_up[l].astype(BF16)
        wg = gate_up[:, :D_FF].reshape(d, n_chunks, FF_CHUNK).transpose(1, 0, 2)
        wu = gate_up[:, D_FF:].reshape(d, n_chunks, FF_CHUNK).transpose(1, 0, 2)
        wd = ffn_w_down[l].astype(BF16).reshape(n_chunks, FF_CHUNK, d)
        wo = w_o[l].astype(BF16).reshape(N_HEADS, GROUP_WIDTH, d)
        flat = lambda t: t.reshape(b * s, GROUP_WIDTH)
        x2d = _post(x2d, flat(a_out), flat(b_out), flat(c_out), flat(d_out), wo, row(post_mix_norm[l]),
                    row(pre_ffn_norm[l]), wg, wu, wd, row(post_ffn_norm[l]))
    return x2d.reshape(b, s, d)
```

```python
import functools
import math

import numpy as np
import jax
import jax.numpy as jnp
from jax import lax
from jax.experimental import pallas as pl
from jax.experimental.pallas import tpu as pltpu

F32 = jnp.float32
BF16 = jnp.bfloat16

D_MODEL = 1024
SEQ = 2048
GRID_W = 64
GRID_ROWS = SEQ // GRID_W
HEAD_DIM = 64
N_HEADS = 4
GROUP_WIDTH = 256
EPS = 1e-6
ROPE_THETA = 10000.0

NA_WIN_ROWS = 8
NA_WIN_COLS = 16
DIFF_QK_DIM = 32
POS_SPLIT = 64
MLA_NOPE_DIM = 64
MLA_ROPE_DIM = 32
MLA_KV_LORA = 128
D_FF = 2816

O_GQ_Q, O_GQ_K, O_GQ_V, O_ML_CQ, O_ML_CKV, O_ML_KR = 1536, 1792, 1920, 2048, 2304, 2432

PROJ_WIDTH = 3072
B256_NA_Q, B256_NA_K, B256_NA_V, B256_DF_Q, B256_DF_K, B256_DF_V, B256_GQ_Q, B256_GQ_QR, B256_ML_CQ = range(9)
B128_GQ_K, B128_GQ_KR, B128_GQ_V, B128_ML_CKV, B128_ML_KR, B128_ML_KRR = range(18, 24)

VMEM_LIMIT_BYTES = 56 * 1024 * 1024

TOKEN_TILE = 1024
FF_CHUNK = 256
Q_TILE = 256
NA_GROUP_ROWS = 4
NA_BAND_ROWS = 12
NEG_BIG = -1e30
LOG2E = math.log2(math.e)


def _rot_perm(width, group):
    j = np.arange(width)
    jj = j % group
    half = group // 2
    src = (j // group) * group + (jj + half) % group
    sign = np.where(jj < half, -1.0, 1.0).astype(np.float32)
    return src, sign


def _compiler_params():
    return pltpu.CompilerParams(dimension_semantics=("arbitrary",), vmem_limit_bytes=VMEM_LIMIT_BYTES)


def _const_spec(shape):
    zeros = (0,) * len(shape)
    return pl.BlockSpec(shape, lambda i: zeros, pipeline_mode=pl.Buffered(1))


def _rms(x):
    return lax.rsqrt(jnp.mean(x * x, axis=-1, keepdims=True) + EPS)


def _dot(a, b):
    return jnp.dot(a, b, preferred_element_type=F32)


def _inproj_kernel(x_ref, g_ref, w_ref, o_ref):
    x = x_ref[...]
    h = (x * _rms(x) * g_ref[...]).astype(BF16)
    o_ref[...] = _dot(h, w_ref[...])


def _inproj(x2d, gain, w):
    t = x2d.shape[0]
    return pl.pallas_call(
        _inproj_kernel,
        out_shape=jax.ShapeDtypeStruct((t, PROJ_WIDTH), F32),
        grid=(t // TOKEN_TILE,),
        in_specs=[
            pl.BlockSpec((TOKEN_TILE, D_MODEL), lambda i: (i, 0)),
            _const_spec((1, D_MODEL)),
            _const_spec((D_MODEL, PROJ_WIDTH)),
        ],
        out_specs=pl.BlockSpec((TOKEN_TILE, PROJ_WIDTH), lambda i: (i, 0)),
        compiler_params=_compiler_params(),
        name="inproj",
    )(x2d, gain, w)


def _segment_mean_square(x, seg):
    w = x.shape[-1]
    same = (lax.broadcasted_iota(jnp.int32, (w, w), 0) // seg) == (lax.broadcasted_iota(jnp.int32, (w, w), 1) // seg)
    ones = jnp.where(same, 1.0, 0.0).astype(BF16)
    sq = x * x
    hi = sq.astype(BF16)
    lo = (sq - hi.astype(F32)).astype(BF16)
    return (_dot(hi, ones) + _dot(lo, ones)) * (1.0 / seg)


def _ones_row_block(width):
    return jnp.where(lax.broadcasted_iota(jnp.int32, (16, width), 0) == 0, 1.0, 0.0).astype(BF16)


def _normalised_heads(accs):
    return jnp.concatenate([acc[:HEAD_DIM] / acc[HEAD_DIM:HEAD_DIM + 1] for acc in accs], axis=0)


def _pipelined_softmax_pv(n_tiles, n_heads, n_chunks, score_chunks, value_chunk, combine, s_scr, acc_scr, o_ref,
                          exp_fn=jnp.exp2):
    assert n_heads % 2 == 0
    rows = s_scr.shape[1] // n_chunks

    def scores_into(slot, qi, h):
        m8 = None
        for c, st in enumerate(score_chunks(qi, h)):
            s_scr[slot, c * rows:(c + 1) * rows, :] = st
            cm = jnp.max(st.reshape(-1, 8, st.shape[-1]), axis=0)
            m8 = cm if m8 is None else jnp.maximum(m8, cm)
        return jnp.max(m8, axis=0, keepdims=True)

    def write_tile(qi):
        r0 = qi * Q_TILE if isinstance(qi, int) else pl.multiple_of(qi * Q_TILE, Q_TILE)
        o_ref[0, pl.ds(r0, Q_TILE), :] = combine([acc_scr[h] for h in range(n_heads)]).T.astype(o_ref.dtype)

    acc_scr[...] = jnp.ones_like(acc_scr)

    def body(qi, m):
        next_qi = jnp.minimum(qi + 1, n_tiles - 1)
        for h in range(n_heads):
            slot = h % 2
            nq, nh = (qi, h + 1) if h + 1 < n_heads else (next_qi, 0)
            next_m = scores_into(1 - slot, nq, nh)
            if h == 0:
                write_tile(jnp.maximum(qi - 1, 0))
            acc = None
            for c in range(n_chunks):
                p = exp_fn(s_scr[slot, c * rows:(c + 1) * rows, :] - m).astype(BF16)
                part = _dot(value_chunk(qi, h, c), p)
                acc = part if acc is None else acc + part
            acc_scr[h] = acc
            m = next_m
        return m

    lax.fori_loop(0, n_tiles, body, scores_into(0, 0, 0), unroll=2)
    write_tile(n_tiles - 1)


def _proj_spec(width, block):
    return pl.BlockSpec((1, SEQ, width), lambda b: (b, 0, block))


def _na_kernel(q_ref, k_ref, v_ref, tab_ref, o_ref, qt_scr, k_scr, vt_scr, s_scr, acc_scr):
    assert NA_GROUP_ROWS * GRID_W == Q_TILE and NA_BAND_ROWS % NA_GROUP_ROWS == 0
    n_tiles = SEQ // Q_TILE
    band_chunks = NA_BAND_ROWS // NA_GROUP_ROWS
    pair = 2 * HEAD_DIM
    q_scale = HEAD_DIM ** -0.5 * LOG2E

    def prep(ci, carry):
        r0 = pl.multiple_of(ci * Q_TILE, Q_TILE)
        rows = pl.ds(r0, Q_TILE)
        qt = (q_ref[0, rows, :] * q_scale).T
        row = lax.broadcasted_iota(jnp.int32, (pair, Q_TILE), 0)
        for h in range(N_HEADS):
            own = (row >= (h % 2) * HEAD_DIM) & (row < (h % 2 + 1) * HEAD_DIM)
            qt_scr[ci, h] = jnp.where(own, qt[(h // 2) * pair:(h // 2 + 1) * pair], 0.0).astype(BF16)
        for p in range(N_HEADS // 2):
            k_scr[p, rows, :] = k_ref[0, rows, p * pair:(p + 1) * pair].astype(BF16)
        vt = v_ref[0, rows, :].T.astype(BF16)
        for h in range(N_HEADS):
            vt_scr[h, ci, 0:HEAD_DIM, :] = vt[h * HEAD_DIM:(h + 1) * HEAD_DIM]
            vt_scr[h, ci, HEAD_DIM:, :] = _ones_row_block(Q_TILE)
        return carry

    lax.fori_loop(0, n_tiles, prep, 0)

    def key_chunk(gi, c):
        return jnp.clip(gi - (NA_WIN_ROWS // 2) // NA_GROUP_ROWS, 0, n_tiles - band_chunks) + c

    def score_chunks(gi, h):
        kind = jnp.where(gi == 0, 0, jnp.where(gi == n_tiles - 1, 2, 1))
        k0 = pl.multiple_of(key_chunk(gi, 0) * Q_TILE, Q_TILE)
        return [_dot(k_scr[h // 2, pl.ds(k0 + c * Q_TILE, Q_TILE), :], qt_scr[gi, h])
                + tab_ref[kind, h, c * Q_TILE:(c + 1) * Q_TILE, :] for c in range(band_chunks)]

    def value_chunk(gi, h, c):
        return vt_scr[h, key_chunk(gi, c)]

    _pipelined_softmax_pv(n_tiles, N_HEADS, band_chunks, score_chunks, value_chunk, _normalised_heads, s_scr, acc_scr,
                          o_ref)


def _na_tables(rel_bias):
    a = np.arange(NA_GROUP_ROWS)[:, None, None, None]
    c = np.arange(GRID_W)[None, :, None, None]
    i = np.arange(NA_BAND_ROWS)[None, None, :, None]
    kc = np.arange(GRID_W)[None, None, None, :]
    cs = np.clip(c - NA_WIN_COLS // 2, 0, GRID_W - NA_WIN_COLS)
    col_ok = (kc >= cs) & (kc < cs + NA_WIN_COLS)
    dc = kc - c + NA_WIN_COLS - 1
    col_sel = (dc[..., None] == np.arange(2 * NA_WIN_COLS - 1)) & col_ok[..., None]
    row_sels = []
    last_r0 = GRID_ROWS - NA_GROUP_ROWS
    for r0, band_row in ((0, 0), (NA_GROUP_ROWS, 0), (last_r0, GRID_ROWS - NA_BAND_ROWS)):
        r = r0 + a
        rs = np.clip(r - NA_WIN_ROWS // 2, 0, GRID_ROWS - NA_WIN_ROWS)
        key_row = band_row + i
        row_ok = (key_row >= rs) & (key_row < rs + NA_WIN_ROWS)
        dr = key_row - r + NA_WIN_ROWS - 1
        row_sels.append((dr[..., None] == np.arange(2 * NA_WIN_ROWS - 1)) & row_ok[..., None])
    row_sel = np.stack(row_sels)[:, :, 0, :, 0, :].astype(np.float32)
    col_sel = col_sel[0, :, 0, :, :].astype(np.float32)
    vals = jnp.einsum("taiu,huv,ckv->thikac", row_sel, rel_bias.astype(F32), col_sel,
                      precision=lax.Precision.HIGHEST)
    inside = np.einsum("taiu,ckv->tikac", row_sel, col_sel) > 0
    tab = jnp.where(inside[:, None], vals * LOG2E, NEG_BIG)
    q_rows, band = NA_GROUP_ROWS * GRID_W, NA_BAND_ROWS * GRID_W
    return tab.reshape(3, N_HEADS, band, q_rows)


def _na_attention(proj, tables):
    b = proj.shape[0]
    return pl.pallas_call(
        _na_kernel,
        out_shape=jax.ShapeDtypeStruct((b, SEQ, GROUP_WIDTH), BF16),
        grid=(b,),
        in_specs=[
            _proj_spec(256, B256_NA_Q), _proj_spec(256, B256_NA_K), _proj_spec(256, B256_NA_V),
            _const_spec(tables.shape),
        ],
        out_specs=pl.BlockSpec((1, SEQ, GROUP_WIDTH), lambda i: (i, 0, 0)),
        scratch_shapes=[pltpu.VMEM((SEQ // Q_TILE, N_HEADS, 2 * HEAD_DIM, Q_TILE), BF16),
                        pltpu.VMEM((N_HEADS // 2, SEQ, 2 * HEAD_DIM), BF16),
                        pltpu.VMEM((N_HEADS, SEQ // Q_TILE, HEAD_DIM + 16, Q_TILE), BF16),
                        pltpu.VMEM((2, NA_BAND_ROWS * GRID_W, Q_TILE), F32),
                        pltpu.VMEM((N_HEADS, HEAD_DIM + 16, Q_TILE), F32)],
        compiler_params=_compiler_params(),
        name="na_attention",
    )(proj, proj, proj, tables)


def _alibi_slope(h):
    return 2.0 ** (-8.0 * (h + 1) / N_HEADS)


def _diff_kernel(lambda_init, q_ref, k_ref, v_ref, lq1_ref, lk1_ref, lq2_ref, lk2_ref, subln_ref, diag_ref, o_ref,
                 qt_scr, qf_scr, k_scr, vt_scr, s_scr, acc_scr):
    scale = DIFF_QK_DIM ** -0.5
    n_tiles = SEQ // Q_TILE
    pair = 2 * HEAD_DIM
    n_feat = 16
    lam = (jnp.exp(jnp.sum(lq1_ref[...] * lk1_ref[...], axis=-1, keepdims=True))
           - jnp.exp(jnp.sum(lq2_ref[...] * lk2_ref[...], axis=-1, keepdims=True)) + lambda_init)

    def prep(ci, carry):
        r0 = pl.multiple_of(ci * Q_TILE, Q_TILE)
        rows = pl.ds(r0, Q_TILE)
        qt = (q_ref[0, rows, :] * scale).T
        row = lax.broadcasted_iota(jnp.int32, (pair, Q_TILE), 0)
        feat_row = lax.broadcasted_iota(jnp.int32, (n_feat, Q_TILE), 0)
        i = r0 + lax.broadcasted_iota(jnp.int32, (n_feat, Q_TILE), 1)
        i_lo = i % POS_SPLIT
        i_hi = i - i_lo
        for h in range(N_HEADS):
            group = qt[(h // 2) * pair:(h // 2 + 1) * pair]
            for mp in range(2):
                lo = (h % 2) * HEAD_DIM + mp * DIFF_QK_DIM
                keep = (row >= lo) & (row < lo + DIFF_QK_DIM)
                qt_scr[ci, 2 * h + mp] = jnp.where(keep, group, 0.0).astype(BF16)
            slope = _alibi_slope(h)
            feat = jnp.where(feat_row < 2, slope,
                             jnp.where(feat_row == 2, -slope * i_hi.astype(F32),
                                       jnp.where(feat_row == 3, -slope * i_lo.astype(F32), 0.0)))
            qf_scr[ci, h, 0:n_feat, :] = feat.astype(BF16)
            qf_scr[ci, h, n_feat:, :] = (-feat).astype(BF16)
        lane = lax.broadcasted_iota(jnp.int32, (Q_TILE, pair), 1)
        j = r0 + lax.broadcasted_iota(jnp.int32, (Q_TILE, pair), 0)
        j_lo = j % POS_SPLIT
        k_feat = jnp.where(lane == 0, (j - j_lo).astype(F32),
                           jnp.where(lane == 1, j_lo.astype(F32), jnp.where(lane < 4, 1.0, 0.0))).astype(BF16)
        for p in range(N_HEADS // 2):
            k_scr[p, rows, 0:pair] = k_ref[0, rows, p * pair:(p + 1) * pair].astype(BF16)
            k_scr[p, rows, pair:] = k_feat
        vt = v_ref[0, rows, :].T.astype(BF16)
        for h in range(N_HEADS):
            vt_scr[h, ci, 0:HEAD_DIM, :] = vt[h * HEAD_DIM:(h + 1) * HEAD_DIM]
            vt_scr[h, ci, HEAD_DIM:, :] = _ones_row_block(Q_TILE)
        return carry

    lax.fori_loop(0, n_tiles, prep, 0)

    def key_chunk(qi, c):
        wrapped = qi + c >= n_tiles
        return jnp.where(wrapped, qi + c - n_tiles, qi + c), wrapped

    def score_chunk(qi, item, c):
        h = item // 2
        kc, left = key_chunk(qi, c)
        form = 0 if c == 0 else pl.multiple_of(jnp.where(left, 0, n_feat), n_feat)
        feat = qf_scr[qi, h, pl.ds(form, n_feat), :]
        rhs = jnp.concatenate([qt_scr[qi, item], feat, jnp.zeros((pair - n_feat, Q_TILE), BF16)], axis=0)
        st = _dot(k_scr[h // 2, pl.ds(pl.multiple_of(kc * Q_TILE, Q_TILE), Q_TILE), :], rhs)
        return st + diag_ref[h] if c == 0 else st

    def score_chunks(qi, item):
        return [score_chunk(qi, item, c) for c in range(n_tiles)]

    def value_chunk(qi, item, c):
        return vt_scr[item // 2, key_chunk(qi, c)[0]]

    def combine(accs):
        outs = []
        for h in range(N_HEADS):
            a1, a2 = accs[2 * h], accs[2 * h + 1]
            o = a1[:HEAD_DIM] / a1[HEAD_DIM:HEAD_DIM + 1] - lam * (a2[:HEAD_DIM] / a2[HEAD_DIM:HEAD_DIM + 1])
            r = lax.rsqrt(jnp.mean(o * o, axis=0, keepdims=True) + EPS)
            outs.append(o * r * subln_ref[...] * (1.0 - lambda_init))
        return jnp.concatenate(outs, axis=0)

    _pipelined_softmax_pv(n_tiles, 2 * N_HEADS, n_tiles, score_chunks, value_chunk, combine, s_scr, acc_scr, o_ref,
                          exp_fn=jnp.exp)


def _diff_attention(proj, lq1, lk1, lq2, lk2, subln, lambda_init):
    b = proj.shape[0]
    vec = _const_spec((1, DIFF_QK_DIM))
    n_tiles = SEQ // Q_TILE
    rel = np.arange(Q_TILE)
    over = -2.0 * np.maximum(rel[:, None] - rel[None, :], 0).astype(np.float32)
    diag = jnp.asarray(np.stack([_alibi_slope(h) * over for h in range(N_HEADS)]))
    subln_cols = jnp.broadcast_to(subln.reshape(HEAD_DIM, 1), (HEAD_DIM, Q_TILE))
    return pl.pallas_call(
        functools.partial(_diff_kernel, lambda_init),
        out_shape=jax.ShapeDtypeStruct((b, SEQ, GROUP_WIDTH), BF16),
        grid=(b,),
        in_specs=[
            _proj_spec(256, B256_DF_Q), _proj_spec(256, B256_DF_K), _proj_spec(256, B256_DF_V),
            vec, vec, vec, vec, _const_spec((HEAD_DIM, Q_TILE)), _const_spec(diag.shape),
        ],
        out_specs=pl.BlockSpec((1, SEQ, GROUP_WIDTH), lambda i: (i, 0, 0)),
        scratch_shapes=[pltpu.VMEM((n_tiles, 2 * N_HEADS, 2 * HEAD_DIM, Q_TILE), BF16),
                        pltpu.VMEM((n_tiles, N_HEADS, 32, Q_TILE), BF16),
                        pltpu.VMEM((N_HEADS // 2, SEQ, 4 * HEAD_DIM), BF16),
                        pltpu.VMEM((N_HEADS, n_tiles, HEAD_DIM + 16, Q_TILE), BF16),
                        pltpu.VMEM((2, SEQ, Q_TILE), F32),
                        pltpu.VMEM((2 * N_HEADS, HEAD_DIM + 16, Q_TILE), F32)],
        compiler_params=_compiler_params(),
        name="diff_attention",
    )(proj, proj, proj, lq1, lk1, lq2, lk2, subln_cols, diag)


def _gqa_kernel(q_ref, qr_ref, k_ref, kr_ref, v_ref, gq_ref, gqp_ref, gk_ref, gkp_ref, cos_ref, sin_ref, o_ref,
                qt_scr, k_scr, vt_scr, s_scr, acc_scr):
    q_scale = HEAD_DIM ** -0.5 * LOG2E
    kv_heads = N_HEADS // 2
    n_tiles = SEQ // Q_TILE
    pair = 2 * HEAD_DIM

    def normed_rotary(x, xr, g, gp, cos, sin):
        r = lax.rsqrt(_segment_mean_square(x, HEAD_DIM) + EPS)
        return (x * r * g) * cos + (xr * r * gp) * sin

    def prep(ci, carry):
        r0 = pl.multiple_of(ci * Q_TILE, Q_TILE)
        rows = pl.ds(r0, Q_TILE)
        cos, sin = cos_ref[rows, :], sin_ref[rows, :]
        zeros = jnp.zeros((HEAD_DIM, Q_TILE), BF16)
        for half in range(2):
            lanes = slice(half * pair, (half + 1) * pair)
            y = normed_rotary(q_ref[0, rows, lanes], qr_ref[0, rows, lanes], gq_ref[...], gqp_ref[...], cos, sin)
            yt = (y * q_scale).T.astype(BF16)
            for j in range(2):
                h = 2 * half + j
                kv = h // (N_HEADS // kv_heads)
                for part in range(kv_heads):
                    block = yt[j * HEAD_DIM:(j + 1) * HEAD_DIM] if part == kv else zeros
                    qt_scr[ci, h * pair + part * HEAD_DIM:h * pair + (part + 1) * HEAD_DIM, :] = block
        k_scr[rows, :] = normed_rotary(k_ref[0, rows, :], kr_ref[0, rows, :], gk_ref[...], gkp_ref[...], cos,
                                       sin).astype(BF16)
        vt = v_ref[0, rows, :].T.astype(BF16)
        for kv in range(kv_heads):
            vt_scr[kv, ci, 0:HEAD_DIM, :] = vt[kv * HEAD_DIM:(kv + 1) * HEAD_DIM]
            vt_scr[kv, ci, HEAD_DIM:, :] = _ones_row_block(Q_TILE)
        return carry

    lax.fori_loop(0, n_tiles, prep, 0)

    def score_chunks(qi, h):
        qt = qt_scr[qi, h * pair:(h + 1) * pair, :]
        return [_dot(k_scr[c * Q_TILE:(c + 1) * Q_TILE, :], qt) for c in range(n_tiles)]

    def value_chunk(qi, h, c):
        return vt_scr[h // (N_HEADS // kv_heads), c]

    _pipelined_softmax_pv(n_tiles, N_HEADS, n_tiles, score_chunks, value_chunk, _normalised_heads, s_scr, acc_scr,
                          o_ref)


def _gqa_attention(proj, gq, gqp, gk, gkp, cos, sin):
    b = proj.shape[0]
    vec = _const_spec((1, 2 * HEAD_DIM))
    tab = _const_spec((SEQ, 2 * HEAD_DIM))
    return pl.pallas_call(
        _gqa_kernel,
        out_shape=jax.ShapeDtypeStruct((b, SEQ, GROUP_WIDTH), BF16),
        grid=(b,),
        in_specs=[
            _proj_spec(256, B256_GQ_Q), _proj_spec(256, B256_GQ_QR),
            _proj_spec(128, B128_GQ_K), _proj_spec(128, B128_GQ_KR), _proj_spec(128, B128_GQ_V),
            vec, vec, vec, vec, tab, tab,
        ],
        out_specs=pl.BlockSpec((1, SEQ, GROUP_WIDTH), lambda i: (i, 0, 0)),
        scratch_shapes=[pltpu.VMEM((SEQ // Q_TILE, N_HEADS * 2 * HEAD_DIM, Q_TILE), BF16),
                        pltpu.VMEM((SEQ, 2 * HEAD_DIM), BF16),
                        pltpu.VMEM((N_HEADS // 2, SEQ // Q_TILE, HEAD_DIM + 16, Q_TILE), BF16),
                        pltpu.VMEM((2, SEQ, Q_TILE), F32),
                        pltpu.VMEM((N_HEADS, HEAD_DIM + 16, Q_TILE), F32)],
        compiler_params=_compiler_params(),
        name="gqa_attention",
    )(proj, proj, proj, proj, proj, gq, gqp, gk, gkp, cos, sin)


def _mla_kernel(cq_ref, ckv_ref, kr_ref, krr_ref, gq_ref, gkv_ref, wuq_ref, wukv_ref, cos_ref, sin_ref, o_ref,
                qt_scr, k_scr, vt_scr, s_scr, acc_scr):
    q_scale = (MLA_NOPE_DIM + MLA_ROPE_DIM) ** -0.5 * LOG2E
    n_tiles = SEQ // Q_TILE
    rot0 = N_HEADS * 128
    v0 = N_HEADS * 128

    def prep(ci, carry):
        r0 = pl.multiple_of(ci * Q_TILE, Q_TILE)
        rows = pl.ds(r0, Q_TILE)
        cos, sin = cos_ref[rows, :], sin_ref[rows, :]
        cq = cq_ref[0, rows, :]
        cqn = (cq * _rms(cq) * gq_ref[...]).astype(BF16)
        ckv = ckv_ref[0, rows, :]
        ckvn = (ckv * _rms(ckv) * gkv_ref[...]).astype(BF16)
        k_rope = kr_ref[0, rows, :] * cos + krr_ref[0, rows, :] * sin
        for h in range(N_HEADS):
            cols = slice(h * 128, (h + 1) * 128)
            rot_cols = slice(rot0 + h * 128, rot0 + (h + 1) * 128)
            qh = _dot(cqn, wuq_ref[:, cols]) * cos + _dot(cqn, wuq_ref[:, rot_cols]) * sin
            qt_scr[ci, h * 128:(h + 1) * 128, :] = (qh * q_scale).T.astype(BF16)
            k_scr[h, rows, :] = (_dot(ckvn, wukv_ref[:, cols]) + k_rope).astype(BF16)
        vt = _dot(ckvn, wukv_ref[:, v0:v0 + GROUP_WIDTH]).T.astype(BF16)
        for h in range(N_HEADS):
            vt_scr[h, ci, 0:HEAD_DIM, :] = vt[h * HEAD_DIM:(h + 1) * HEAD_DIM]
            vt_scr[h, ci, HEAD_DIM:, :] = _ones_row_block(Q_TILE)
        return carry

    lax.fori_loop(0, n_tiles, prep, 0)

    def score_chunks(qi, h):
        qt = qt_scr[qi, h * 128:(h + 1) * 128, :]
        return [_dot(k_scr[h, c * Q_TILE:(c + 1) * Q_TILE, :], qt) for c in range(n_tiles)]

    def value_chunk(qi, h, c):
        return vt_scr[h, c]

    _pipelined_softmax_pv(n_tiles, N_HEADS, n_tiles, score_chunks, value_chunk, _normalised_heads, s_scr, acc_scr,
                          o_ref)


def _mla_attention(proj, gq, gkv, wuq, wukv, cos, sin):
    b = proj.shape[0]
    tab = _const_spec((SEQ, 128))
    return pl.pallas_call(
        _mla_kernel,
        out_shape=jax.ShapeDtypeStruct((b, SEQ, GROUP_WIDTH), BF16),
        grid=(b,),
        in_specs=[
            _proj_spec(256, B256_ML_CQ), _proj_spec(128, B128_ML_CKV),
            _proj_spec(128, B128_ML_KR), _proj_spec(128, B128_ML_KRR),
            _const_spec((1, GROUP_WIDTH)), _const_spec((1, MLA_KV_LORA)),
            _const_spec(wuq.shape), _const_spec(wukv.shape), tab, tab,
        ],
        out_specs=pl.BlockSpec((1, SEQ, GROUP_WIDTH), lambda i: (i, 0, 0)),
        scratch_shapes=[pltpu.VMEM((SEQ // Q_TILE, N_HEADS * 128, Q_TILE), BF16),
                        pltpu.VMEM((N_HEADS, SEQ, 128), BF16),
                        pltpu.VMEM((N_HEADS, SEQ // Q_TILE, HEAD_DIM + 16, Q_TILE), BF16),
                        pltpu.VMEM((2, SEQ, Q_TILE), F32),
                        pltpu.VMEM((N_HEADS, HEAD_DIM + 16, Q_TILE), F32)],
        compiler_params=_compiler_params(),
        name="mla_attention",
    )(proj, proj, proj, proj, gq, gkv, wuq, wukv, cos, sin)


def _post_kernel(x_ref, a_ref, b_ref, c_ref, d_ref, wo_ref, g_mix_ref, g_pre_ref, wg_ref, wu_ref, wd_ref,
                 g_ffn_ref, o_ref, acc_ref):
    mix = (_dot(a_ref[...], wo_ref[0]) + _dot(b_ref[...], wo_ref[1])
           + _dot(c_ref[...], wo_ref[2]) + _dot(d_ref[...], wo_ref[3]))
    x = x_ref[...] + mix * _rms(mix) * g_mix_ref[...]
    h = (x * _rms(x) * g_pre_ref[...]).astype(BF16)
    acc_ref[...] = jnp.zeros_like(acc_ref)

    def body(ci, carry):
        gate = _dot(h, wg_ref[ci])
        up = _dot(h, wu_ref[ci])
        act = (gate * jax.nn.sigmoid(gate) * up).astype(BF16)
        acc_ref[...] += _dot(act, wd_ref[ci])
        return carry

    lax.fori_loop(0, D_FF // FF_CHUNK, body, 0, unroll=True)
    f = acc_ref[...]
    o_ref[...] = x + f * _rms(f) * g_ffn_ref[...]


def _post(x2d, a, b, c, d, wo, g_mix, g_pre, wg, wu, wd, g_ffn):
    t = x2d.shape[0]
    tok = lambda w: pl.BlockSpec((TOKEN_TILE, w), lambda i: (i, 0))
    vec = _const_spec((1, D_MODEL))
    return pl.pallas_call(
        _post_kernel,
        out_shape=jax.ShapeDtypeStruct((t, D_MODEL), F32),
        grid=(t // TOKEN_TILE,),
        in_specs=[
            tok(D_MODEL), tok(GROUP_WIDTH), tok(GROUP_WIDTH), tok(GROUP_WIDTH), tok(GROUP_WIDTH),
            _const_spec(wo.shape), vec, vec, _const_spec(wg.shape), _const_spec(wu.shape), _const_spec(wd.shape),
            vec,
        ],
        out_specs=tok(D_MODEL),
        scratch_shapes=[pltpu.VMEM((TOKEN_TILE, D_MODEL), F32)],
        compiler_params=_compiler_params(),
        name="outproj_swiglu",
    )(x2d, a, b, c, d, wo, g_mix, g_pre, wg, wu, wd, g_ffn)


def _layout_w_in(w):
    src64, sign64 = _rot_perm(HEAD_DIM, HEAD_DIM // 2)
    src_q = np.concatenate([h * HEAD_DIM + src64 for h in range(N_HEADS)])
    src_k = src_q[: 2 * HEAD_DIM]
    src32, sign32 = _rot_perm(MLA_ROPE_DIM, MLA_ROPE_DIM)
    gq_q = w[:, O_GQ_Q:O_GQ_K]
    gq_k = w[:, O_GQ_K:O_GQ_V]
    k_rope = w[:, O_ML_KR:O_ML_KR + MLA_ROPE_DIM]
    pad = lambda m: jnp.pad(m, ((0, 0), (MLA_NOPE_DIM, 128 - MLA_NOPE_DIM - MLA_ROPE_DIM)))
    cols = [
        w[:, :O_GQ_K],
        gq_q[:, src_q] * np.tile(sign64, N_HEADS),
        w[:, O_ML_CQ:O_ML_CKV],
        gq_k, gq_k[:, src_k] * np.tile(sign64, 2),
        w[:, O_GQ_V:O_ML_CQ],
        w[:, O_ML_CKV:O_ML_KR],
        pad(k_rope), pad(k_rope[:, src32] * sign32),
    ]
    return jnp.concatenate(cols, axis=1).astype(BF16)


def _layout_mla_weights(w_uq, w_ukv):
    src32, sign32 = _rot_perm(MLA_ROPE_DIM, MLA_ROPE_DIM)
    wq = w_uq.reshape(GROUP_WIDTH, N_HEADS, MLA_NOPE_DIM + MLA_ROPE_DIM)
    rope = wq[:, :, MLA_NOPE_DIM:]
    tail = 128 - MLA_NOPE_DIM - MLA_ROPE_DIM
    q_main = jnp.pad(wq, ((0, 0), (0, 0), (0, tail)))
    q_rot = jnp.pad(rope[:, :, src32] * sign32, ((0, 0), (0, 0), (MLA_NOPE_DIM, tail)))
    wuq = jnp.concatenate([q_main.reshape(GROUP_WIDTH, -1), q_rot.reshape(GROUP_WIDTH, -1)], axis=1)
    wkv = w_ukv.reshape(MLA_KV_LORA, N_HEADS, MLA_NOPE_DIM + HEAD_DIM)
    k_nope = jnp.pad(wkv[:, :, :MLA_NOPE_DIM], ((0, 0), (0, 0), (0, 128 - MLA_NOPE_DIM)))
    vals = wkv[:, :, MLA_NOPE_DIM:]
    wukv = jnp.concatenate([k_nope.reshape(MLA_KV_LORA, -1), vals.reshape(MLA_KV_LORA, -1)], axis=1)
    return wuq.astype(BF16), wukv.astype(BF16)


def _rotary_tables():
    pos = jnp.arange(SEQ)
    half = HEAD_DIM // 2
    inv = ROPE_THETA ** (-jnp.arange(0, half, 2, dtype=F32) / half)

    def angles(p):
        ang = p.astype(F32)[:, None] * inv[None, :]
        return jnp.concatenate([ang, ang], axis=-1)

    axial = jnp.concatenate([angles(pos // GRID_W), angles(pos % GRID_W)], axis=-1)
    seq = angles(pos)
    tail = 128 - MLA_NOPE_DIM - MLA_ROPE_DIM
    mla_cos = jnp.concatenate([jnp.ones((SEQ, MLA_NOPE_DIM), F32), jnp.cos(seq), jnp.zeros((SEQ, tail), F32)], -1)
    mla_sin = jnp.pad(jnp.sin(seq), ((0, 0), (MLA_NOPE_DIM, tail)))
    pair = lambda t: jnp.tile(t, (1, 2))
    return pair(jnp.cos(axial)), pair(jnp.sin(axial)), mla_cos, mla_sin


def kernel(x, pre_mix_norm, w_in, na_rel_bias, diff_lambda_q1, diff_lambda_k1, diff_lambda_q2, diff_lambda_k2,
           diff_subln, gqa_q_norm, gqa_k_norm, mla_q_norm, mla_kv_norm, mla_w_uq, mla_w_ukv, w_o, post_mix_norm,
           pre_ffn_norm, ffn_w_gate_up, ffn_w_down, post_ffn_norm):
    b, s, d = x.shape
    assert (s, d) == (SEQ, D_MODEL)
    depth = w_in.shape[0]
    src64, _ = _rot_perm(HEAD_DIM, HEAD_DIM // 2)
    ax_cos, ax_sin, mla_cos, mla_sin = _rotary_tables()
    row = lambda v: v.reshape(1, -1).astype(F32)
    n_chunks = D_FF // FF_CHUNK

    x2d = x.reshape(b * s, d)
    for l in range(depth):
        lambda_init = 0.8 - 0.6 * math.exp(-0.3 * l)
        proj = _inproj(x2d, row(pre_mix_norm[l]), _layout_w_in(w_in[l])).reshape(b, s, PROJ_WIDTH)

        a_out = _na_attention(proj, _na_tables(na_rel_bias[l]))
        b_out = _diff_attention(proj, row(diff_lambda_q1[l]), row(diff_lambda_k1[l]), row(diff_lambda_q2[l]),
                                row(diff_lambda_k2[l]), row(diff_subln[l]), lambda_init)
        pair_row = lambda v: row(jnp.tile(v, 2))
        c_out = _gqa_attention(proj, pair_row(gqa_q_norm[l]), pair_row(gqa_q_norm[l][src64]),
                               pair_row(gqa_k_norm[l]), pair_row(gqa_k_norm[l][src64]), ax_cos, ax_sin)
        wuq, wukv = _layout_mla_weights(mla_w_uq[l], mla_w_ukv[l])
        d_out = _mla_attention(proj, row(mla_q_norm[l]), row(mla_kv_norm[l]), wuq, wukv, mla_cos, mla_sin)

        gate_up = ffn_w_gate_up[l].astype(BF16)
        wg = gate_up[:, :D_FF].reshape(d, n_chunks, FF_CHUNK).transpose(1, 0, 2)
        wu = gate_up[:, D_FF:].reshape(d, n_chunks, FF_CHUNK).transpose(1, 0, 2)
        wd = ffn_w_down[l].astype(BF16).reshape(n_chunks, FF_CHUNK, d)
        wo = w_o[l].astype(BF16).reshape(N_HEADS, GROUP_WIDTH, d)
        flat = lambda t: t.reshape(b * s, GROUP_WIDTH)
        x2d = _post(x2d, flat(a_out), flat(b_out), flat(c_out), flat(d_out), wo, row(post_mix_norm[l]),
                    row(pre_ffn_norm[l]), wg, wu, wd, row(post_ffn_norm[l]))
    return x2d.reshape(b, s, d)
```

```python
import functools
import math

import numpy as np
import jax
import jax.numpy as jnp
from jax import lax
from jax.experimental import pallas as pl
from jax.experimental.pallas import tpu as pltpu

F32 = jnp.float32
BF16 = jnp.bfloat16

D_MODEL = 1024
SEQ = 2048
GRID_W = 64
GRID_ROWS = SEQ // GRID_W
HEAD_DIM = 64
N_HEADS = 4
GROUP_WIDTH = 256
EPS = 1e-6
ROPE_THETA = 10000.0

NA_WIN_ROWS = 8
NA_WIN_COLS = 16
DIFF_QK_DIM = 32
POS_SPLIT = 64
MLA_NOPE_DIM = 64
MLA_ROPE_DIM = 32
MLA_KV_LORA = 128
D_FF = 2816

O_GQ_Q, O_GQ_K, O_GQ_V, O_ML_CQ, O_ML_CKV, O_ML_KR = 1536, 1792, 1920, 2048, 2304, 2432

PROJ_WIDTH = 3072
B256_NA_Q, B256_NA_K, B256_NA_V, B256_DF_Q, B256_DF_K, B256_DF_V, B256_GQ_Q, B256_GQ_QR, B256_ML_CQ = range(9)
B128_GQ_K, B128_GQ_KR, B128_GQ_V, B128_ML_CKV, B128_ML_KR, B128_ML_KRR = range(18, 24)

VMEM_LIMIT_BYTES = 56 * 1024 * 1024

TOKEN_TILE = 1024
FF_CHUNK = 256
Q_TILE = 256
NA_GROUP_ROWS = 4
NA_BAND_ROWS = 12
NEG_BIG = -1e30
LOG2E = math.log2(math.e)


def _rot_perm(width, group):
    j = np.arange(width)
    jj = j % group
    half = group // 2
    src = (j // group) * group + (jj + half) % group
    sign = np.where(jj < half, -1.0, 1.0).astype(np.float32)
    return src, sign


def _compiler_params():
    return pltpu.CompilerParams(dimension_semantics=("arbitrary",), vmem_limit_bytes=VMEM_LIMIT_BYTES)


def _const_spec(shape):
    zeros = (0,) * len(shape)
    return pl.BlockSpec(shape, lambda i: zeros, pipeline_mode=pl.Buffered(1))


def _rms(x):
    return lax.rsqrt(jnp.mean(x * x, axis=-1, keepdims=True) + EPS)


def _dot(a, b):
    return jnp.dot(a, b, preferred_element_type=F32)


def _inproj_kernel(x_ref, g_ref, w_ref, o_ref):
    x = x_ref[...]
    h = (x * _rms(x) * g_ref[...]).astype(BF16)
    o_ref[...] = _dot(h, w_ref[...])


def _inproj(x2d, gain, w):
    t = x2d.shape[0]
    return pl.pallas_call(
        _inproj_kernel,
        out_shape=jax.ShapeDtypeStruct((t, PROJ_WIDTH), F32),
        grid=(t // TOKEN_TILE,),
        in_specs=[
            pl.BlockSpec((TOKEN_TILE, D_MODEL), lambda i: (i, 0)),
            _const_spec((1, D_MODEL)),
            _const_spec((D_MODEL, PROJ_WIDTH)),
        ],
        out_specs=pl.BlockSpec((TOKEN_TILE, PROJ_WIDTH), lambda i: (i, 0)),
        compiler_params=_compiler_params(),
        name="inproj",
    )(x2d, gain, w)


def _segment_mean_square(x, seg):
    w = x.shape[-1]
    same = (lax.broadcasted_iota(jnp.int32, (w, w), 0) // seg) == (lax.broadcasted_iota(jnp.int32, (w, w), 1) // seg)
    ones = jnp.where(same, 1.0, 0.0).astype(BF16)
    sq = x * x
    hi = sq.astype(BF16)
    lo = (sq - hi.astype(F32)).astype(BF16)
    return (_dot(hi, ones) + _dot(lo, ones)) * (1.0 / seg)


def _ones_row_block(width):
    return jnp.where(lax.broadcasted_iota(jnp.int32, (16, width), 0) == 0, 1.0, 0.0).astype(BF16)


def _normalised_heads(accs):
    return jnp.concatenate([acc[:HEAD_DIM] / acc[HEAD_DIM:HEAD_DIM + 1] for acc in accs], axis=0)


def _pipelined_softmax_pv(n_tiles, n_heads, n_chunks, score_chunks, value_chunk, combine, s_scr, acc_scr, o_ref,
                          exp_fn=jnp.exp2):
    assert n_heads % 2 == 0
    rows = s_scr.shape[1] // n_chunks

    def scores_into(slot, qi, h):
        m8 = None
        for c, st in enumerate(score_chunks(qi, h)):
            s_scr[slot, c * rows:(c + 1) * rows, :] = st
            cm = jnp.max(st.reshape(-1, 8, st.shape[-1]), axis=0)
            m8 = cm if m8 is None else jnp.maximum(m8, cm)
        return jnp.max(m8, axis=0, keepdims=True)

    def write_tile(qi):
        r0 = qi * Q_TILE if isinstance(qi, int) else pl.multiple_of(qi * Q_TILE, Q_TILE)
        o_ref[0, pl.ds(r0, Q_TILE), :] = combine([acc_scr[h] for h in range(n_heads)]).T.astype(o_ref.dtype)

    acc_scr[...] = jnp.ones_like(acc_scr)

    def body(qi, m):
        next_qi = jnp.minimum(qi + 1, n_tiles - 1)
        for h in range(n_heads):
            slot = h % 2
            nq, nh = (qi, h + 1) if h + 1 < n_heads else (next_qi, 0)
            next_m = scores_into(1 - slot, nq, nh)
            if h == 0:
                write_tile(jnp.maximum(qi - 1, 0))
            acc = None
            for c in range(n_chunks):
                p = exp_fn(s_scr[slot, c * rows:(c + 1) * rows, :] - m).astype(BF16)
                part = _dot(value_chunk(qi, h, c), p)
                acc = part if acc is None else acc + part
            acc_scr[h] = acc
            m = next_m
        return m

    lax.fori_loop(0, n_tiles, body, scores_into(0, 0, 0), unroll=2)
    write_tile(n_tiles - 1)


def _proj_spec(width, block):
    return pl.BlockSpec((1, SEQ, width), lambda b: (b, 0, block))


def _na_kernel(q_ref, k_ref, v_ref, tab_ref, o_ref, qt_scr, k_scr, vt_scr, s_scr, acc_scr):
    assert NA_GROUP_ROWS * GRID_W == Q_TILE and NA_BAND_ROWS % NA_GROUP_ROWS == 0
    n_tiles = SEQ // Q_TILE
    band_chunks = NA_BAND_ROWS // NA_GROUP_ROWS
    pair = 2 * HEAD_DIM
    q_scale = HEAD_DIM ** -0.5 * LOG2E

    def prep(ci, carry):
        r0 = pl.multiple_of(ci * Q_TILE, Q_TILE)
        rows = pl.ds(r0, Q_TILE)
        qt = (q_ref[0, rows, :] * q_scale).T
        row = lax.broadcasted_iota(jnp.int32, (pair, Q_TILE), 0)
        for h in range(N_HEADS):
            own = (row >= (h % 2) * HEAD_DIM) & (row < (h % 2 + 1) * HEAD_DIM)
            qt_scr[ci, h] = jnp.where(own, qt[(h // 2) * pair:(h // 2 + 1) * pair], 0.0).astype(BF16)
        for p in range(N_HEADS // 2):
            k_scr[p, rows, :] = k_ref[0, rows, p * pair:(p + 1) * pair].astype(BF16)
        vt = v_ref[0, rows, :].T.astype(BF16)
        for h in range(N_HEADS):
            vt_scr[h, ci, 0:HEAD_DIM, :] = vt[h * HEAD_DIM:(h + 1) * HEAD_DIM]
            vt_scr[h, ci, HEAD_DIM:, :] = _ones_row_block(Q_TILE)
        return carry

    lax.fori_loop(0, n_tiles, prep, 0, unroll=4)

    def key_chunk(gi, c):
        return jnp.clip(gi - (NA_WIN_ROWS // 2) // NA_GROUP_ROWS, 0, n_tiles - band_chunks) + c

    def score_chunks(gi, h):
        kind = jnp.where(gi == 0, 0, jnp.where(gi == n_tiles - 1, 2, 1))
        k0 = pl.multiple_of(key_chunk(gi, 0) * Q_TILE, Q_TILE)
        return [_dot(k_scr[h // 2, pl.ds(k0 + c * Q_TILE, Q_TILE), :], qt_scr[gi, h])
                + tab_ref[kind, h, c * Q_TILE:(c + 1) * Q_TILE, :] for c in range(band_chunks)]

    def value_chunk(gi, h, c):
        return vt_scr[h, key_chunk(gi, c)]

    _pipelined_softmax_pv(n_tiles, N_HEADS, band_chunks, score_chunks, value_chunk, _normalised_heads, s_scr, acc_scr,
                          o_ref)


def _na_tables(rel_bias):
    a = np.arange(NA_GROUP_ROWS)[:, None, None, None]
    c = np.arange(GRID_W)[None, :, None, None]
    i = np.arange(NA_BAND_ROWS)[None, None, :, None]
    kc = np.arange(GRID_W)[None, None, None, :]
    cs = np.clip(c - NA_WIN_COLS // 2, 0, GRID_W - NA_WIN_COLS)
    col_ok = (kc >= cs) & (kc < cs + NA_WIN_COLS)
    dc = kc - c + NA_WIN_COLS - 1
    col_sel = (dc[..., None] == np.arange(2 * NA_WIN_COLS - 1)) & col_ok[..., None]
    row_sels = []
    last_r0 = GRID_ROWS - NA_GROUP_ROWS
    for r0, band_row in ((0, 0), (NA_GROUP_ROWS, 0), (last_r0, GRID_ROWS - NA_BAND_ROWS)):
        r = r0 + a
        rs = np.clip(r - NA_WIN_ROWS // 2, 0, GRID_ROWS - NA_WIN_ROWS)
        key_row = band_row + i
        row_ok = (key_row >= rs) & (key_row < rs + NA_WIN_ROWS)
        dr = key_row - r + NA_WIN_ROWS - 1
        row_sels.append((dr[..., None] == np.arange(2 * NA_WIN_ROWS - 1)) & row_ok[..., None])
    row_sel = np.stack(row_sels)[:, :, 0, :, 0, :].astype(np.float32)
    col_sel = col_sel[0, :, 0, :, :].astype(np.float32)
    vals = jnp.einsum("taiu,huv,ckv->thikac", row_sel, rel_bias.astype(F32), col_sel,
                      precision=lax.Precision.HIGHEST)
    inside = np.einsum("taiu,ckv->tikac", row_sel, col_sel) > 0
    tab = jnp.where(inside[:, None], vals * LOG2E, NEG_BIG)
    q_rows, band = NA_GROUP_ROWS * GRID_W, NA_BAND_ROWS * GRID_W
    return tab.reshape(3, N_HEADS, band, q_rows)


def _na_attention(proj, tables):
    b = proj.shape[0]
    return pl.pallas_call(
        _na_kernel,
        out_shape=jax.ShapeDtypeStruct((b, SEQ, GROUP_WIDTH), BF16),
        grid=(b,),
        in_specs=[
            _proj_spec(256, B256_NA_Q), _proj_spec(256, B256_NA_K), _proj_spec(256, B256_NA_V),
            _const_spec(tables.shape),
        ],
        out_specs=pl.BlockSpec((1, SEQ, GROUP_WIDTH), lambda i: (i, 0, 0)),
        scratch_shapes=[pltpu.VMEM((SEQ // Q_TILE, N_HEADS, 2 * HEAD_DIM, Q_TILE), BF16),
                        pltpu.VMEM((N_HEADS // 2, SEQ, 2 * HEAD_DIM), BF16),
                        pltpu.VMEM((N_HEADS, SEQ // Q_TILE, HEAD_DIM + 16, Q_TILE), BF16),
                        pltpu.VMEM((2, NA_BAND_ROWS * GRID_W, Q_TILE), F32),
                        pltpu.VMEM((N_HEADS, HEAD_DIM + 16, Q_TILE), F32)],
        compiler_params=_compiler_params(),
        name="na_attention",
    )(proj, proj, proj, tables)


def _alibi_slope(h):
    return 2.0 ** (-8.0 * (h + 1) / N_HEADS)


def _diff_kernel(lambda_init, q_ref, k_ref, v_ref, lq1_ref, lk1_ref, lq2_ref, lk2_ref, subln_ref, diag_ref, o_ref,
                 qt_scr, qf_scr, k_scr, vt_scr, s_scr, acc_scr):
    scale = DIFF_QK_DIM ** -0.5
    n_tiles = SEQ // Q_TILE
    pair = 2 * HEAD_DIM
    n_feat = 16
    lam = (jnp.exp(jnp.sum(lq1_ref[...] * lk1_ref[...], axis=-1, keepdims=True))
           - jnp.exp(jnp.sum(lq2_ref[...] * lk2_ref[...], axis=-1, keepdims=True)) + lambda_init)

    def prep(ci, carry):
        r0 = pl.multiple_of(ci * Q_TILE, Q_TILE)
        rows = pl.ds(r0, Q_TILE)
        qt = (q_ref[0, rows, :] * scale).T
        row = lax.broadcasted_iota(jnp.int32, (pair, Q_TILE), 0)
        feat_row = lax.broadcasted_iota(jnp.int32, (n_feat, Q_TILE), 0)
        i = r0 + lax.broadcasted_iota(jnp.int32, (n_feat, Q_TILE), 1)
        i_lo = i % POS_SPLIT
        i_hi = i - i_lo
        for h in range(N_HEADS):
            group = qt[(h // 2) * pair:(h // 2 + 1) * pair]
            for mp in range(2):
                lo = (h % 2) * HEAD_DIM + mp * DIFF_QK_DIM
                keep = (row >= lo) & (row < lo + DIFF_QK_DIM)
                qt_scr[ci, 2 * h + mp] = jnp.where(keep, group, 0.0).astype(BF16)
            slope = _alibi_slope(h)
            feat = jnp.where(feat_row < 2, slope,
                             jnp.where(feat_row == 2, -slope * i_hi.astype(F32),
                                       jnp.where(feat_row == 3, -slope * i_lo.astype(F32), 0.0)))
            qf_scr[ci, h, 0:n_feat, :] = feat.astype(BF16)
            qf_scr[ci, h, n_feat:, :] = (-feat).astype(BF16)
        lane = lax.broadcasted_iota(jnp.int32, (Q_TILE, pair), 1)
        j = r0 + lax.broadcasted_iota(jnp.int32, (Q_TILE, pair), 0)
        j_lo = j % POS_SPLIT
        k_feat = jnp.where(lane == 0, (j - j_lo).astype(F32),
                           jnp.where(lane == 1, j_lo.astype(F32), jnp.where(lane < 4, 1.0, 0.0))).astype(BF16)
        for p in range(N_HEADS // 2):
            k_scr[p, rows, 0:pair] = k_ref[0, rows, p * pair:(p + 1) * pair].astype(BF16)
            k_scr[p, rows, pair:] = k_feat
        vt = v_ref[0, rows, :].T.astype(BF16)
        for h in range(N_HEADS):
            vt_scr[h, ci, 0:HEAD_DIM, :] = vt[h * HEAD_DIM:(h + 1) * HEAD_DIM]
            vt_scr[h, ci, HEAD_DIM:, :] = _ones_row_block(Q_TILE)
        return carry

    lax.fori_loop(0, n_tiles, prep, 0, unroll=4)

    def key_chunk(qi, c):
        wrapped = qi + c >= n_tiles
        return jnp.where(wrapped, qi + c - n_tiles, qi + c), wrapped

    def score_chunk(qi, item, c):
        h = item // 2
        kc, left = key_chunk(qi, c)
        form = 0 if c == 0 else pl.multiple_of(jnp.where(left, 0, n_feat), n_feat)
        feat = qf_scr[qi, h, pl.ds(form, n_feat), :]
        rhs = jnp.concatenate([qt_scr[qi, item], feat, jnp.zeros((pair - n_feat, Q_TILE), BF16)], axis=0)
        st = _dot(k_scr[h // 2, pl.ds(pl.multiple_of(kc * Q_TILE, Q_TILE), Q_TILE), :], rhs)
        return st + diag_ref[h] if c == 0 else st

    def score_chunks(qi, item):
        return [score_chunk(qi, item, c) for c in range(n_tiles)]

    def value_chunk(qi, item, c):
        return vt_scr[item // 2, key_chunk(qi, c)[0]]

    def combine(accs):
        outs = []
        for h in range(N_HEADS):
            a1, a2 = accs[2 * h], accs[2 * h + 1]
            o = a1[:HEAD_DIM] / a1[HEAD_DIM:HEAD_DIM + 1] - lam * (a2[:HEAD_DIM] / a2[HEAD_DIM:HEAD_DIM + 1])
            r = lax.rsqrt(jnp.mean(o * o, axis=0, keepdims=True) + EPS)
            outs.append(o * r * subln_ref[...] * (1.0 - lambda_init))
        return jnp.concatenate(outs, axis=0)

    _pipelined_softmax_pv(n_tiles, 2 * N_HEADS, n_tiles, score_chunks, value_chunk, combine, s_scr, acc_scr, o_ref,
                          exp_fn=jnp.exp)


def _diff_attention(proj, lq1, lk1, lq2, lk2, subln, lambda_init):
    b = proj.shape[0]
    vec = _const_spec((1, DIFF_QK_DIM))
    n_tiles = SEQ // Q_TILE
    rel = np.arange(Q_TILE)
    over = -2.0 * np.maximum(rel[:, None] - rel[None, :], 0).astype(np.float32)
    diag = jnp.asarray(np.stack([_alibi_slope(h) * over for h in range(N_HEADS)]))
    subln_cols = jnp.broadcast_to(subln.reshape(HEAD_DIM, 1), (HEAD_DIM, Q_TILE))
    return pl.pallas_call(
        functools.partial(_diff_kernel, lambda_init),
        out_shape=jax.ShapeDtypeStruct((b, SEQ, GROUP_WIDTH), BF16),
        grid=(b,),
        in_specs=[
            _proj_spec(256, B256_DF_Q), _proj_spec(256, B256_DF_K), _proj_spec(256, B256_DF_V),
            vec, vec, vec, vec, _const_spec((HEAD_DIM, Q_TILE)), _const_spec(diag.shape),
        ],
        out_specs=pl.BlockSpec((1, SEQ, GROUP_WIDTH), lambda i: (i, 0, 0)),
        scratch_shapes=[pltpu.VMEM((n_tiles, 2 * N_HEADS, 2 * HEAD_DIM, Q_TILE), BF16),
                        pltpu.VMEM((n_tiles, N_HEADS, 32, Q_TILE), BF16),
                        pltpu.VMEM((N_HEADS // 2, SEQ, 4 * HEAD_DIM), BF16),
                        pltpu.VMEM((N_HEADS, n_tiles, HEAD_DIM + 16, Q_TILE), BF16),
                        pltpu.VMEM((2, SEQ, Q_TILE), F32),
                        pltpu.VMEM((2 * N_HEADS, HEAD_DIM + 16, Q_TILE), F32)],
        compiler_params=_compiler_params(),
        name="diff_attention",
    )(proj, proj, proj, lq1, lk1, lq2, lk2, subln_cols, diag)


def _gqa_kernel(q_ref, qr_ref, k_ref, kr_ref, v_ref, qc_ref, qs_ref, kc_ref, ks_ref, o_ref,
                qt_scr, k_scr, vt_scr, s_scr, acc_scr):
    kv_heads = N_HEADS // 2
    n_tiles = SEQ // Q_TILE
    pair = 2 * HEAD_DIM

    def normed_rotary(x, xr, gain_cos, gain_sin):
        return (x * gain_cos + xr * gain_sin) * lax.rsqrt(_segment_mean_square(x, HEAD_DIM) + EPS)

    def prep(ci, carry):
        r0 = pl.multiple_of(ci * Q_TILE, Q_TILE)
        rows = pl.ds(r0, Q_TILE)
        zeros = jnp.zeros((HEAD_DIM, Q_TILE), BF16)
        for half in range(2):
            lanes = slice(half * pair, (half + 1) * pair)
            y = normed_rotary(q_ref[0, rows, lanes], qr_ref[0, rows, lanes], qc_ref[rows, :], qs_ref[rows, :])
            yt = y.T.astype(BF16)
            for j in range(2):
                h = 2 * half + j
                kv = h // (N_HEADS // kv_heads)
                for part in range(kv_heads):
                    block = yt[j * HEAD_DIM:(j + 1) * HEAD_DIM] if part == kv else zeros
                    qt_scr[ci, h * pair + part * HEAD_DIM:h * pair + (part + 1) * HEAD_DIM, :] = block
        k_scr[rows, :] = normed_rotary(k_ref[0, rows, :], kr_ref[0, rows, :], kc_ref[rows, :],
                                       ks_ref[rows, :]).astype(BF16)
        vt = v_ref[0, rows, :].T.astype(BF16)
        for kv in range(kv_heads):
            vt_scr[kv, ci, 0:HEAD_DIM, :] = vt[kv * HEAD_DIM:(kv + 1) * HEAD_DIM]
            vt_scr[kv, ci, HEAD_DIM:, :] = _ones_row_block(Q_TILE)
        return carry

    lax.fori_loop(0, n_tiles, prep, 0, unroll=4)

    def score_chunks(qi, h):
        qt = qt_scr[qi, h * pair:(h + 1) * pair, :]
        return [_dot(k_scr[c * Q_TILE:(c + 1) * Q_TILE, :], qt) for c in range(n_tiles)]

    def value_chunk(qi, h, c):
        return vt_scr[h // (N_HEADS // kv_heads), c]

    _pipelined_softmax_pv(n_tiles, N_HEADS, n_tiles, score_chunks, value_chunk, _normalised_heads, s_scr, acc_scr,
                          o_ref)


def _gqa_attention(proj, gq, gqp, gk, gkp, cos, sin):
    b = proj.shape[0]
    tab = _const_spec((SEQ, 2 * HEAD_DIM))
    q_scale = HEAD_DIM ** -0.5 * LOG2E
    tables = (gq * cos * q_scale, gqp * sin * q_scale, gk * cos, gkp * sin)
    return pl.pallas_call(
        _gqa_kernel,
        out_shape=jax.ShapeDtypeStruct((b, SEQ, GROUP_WIDTH), BF16),
        grid=(b,),
        in_specs=[
            _proj_spec(256, B256_GQ_Q), _proj_spec(256, B256_GQ_QR),
            _proj_spec(128, B128_GQ_K), _proj_spec(128, B128_GQ_KR), _proj_spec(128, B128_GQ_V),
            tab, tab, tab, tab,
        ],
        out_specs=pl.BlockSpec((1, SEQ, GROUP_WIDTH), lambda i: (i, 0, 0)),
        scratch_shapes=[pltpu.VMEM((SEQ // Q_TILE, N_HEADS * 2 * HEAD_DIM, Q_TILE), BF16),
                        pltpu.VMEM((SEQ, 2 * HEAD_DIM), BF16),
                        pltpu.VMEM((N_HEADS // 2, SEQ // Q_TILE, HEAD_DIM + 16, Q_TILE), BF16),
                        pltpu.VMEM((2, SEQ, Q_TILE), F32),
                        pltpu.VMEM((N_HEADS, HEAD_DIM + 16, Q_TILE), F32)],
        compiler_params=_compiler_params(),
        name="gqa_attention",
    )(proj, proj, proj, proj, proj, *tables)


def _mla_kernel(cq_ref, ckv_ref, kr_ref, krr_ref, gq_ref, gkv_ref, wuq_ref, wukv_ref, cos_ref, sin_ref, o_ref,
                qt_scr, k_scr, vt_scr, s_scr, acc_scr):
    q_scale = (MLA_NOPE_DIM + MLA_ROPE_DIM) ** -0.5 * LOG2E
    n_tiles = SEQ // Q_TILE
    rot0 = N_HEADS * 128
    v0 = N_HEADS * 128

    def prep(ci, carry):
        r0 = pl.multiple_of(ci * Q_TILE, Q_TILE)
        rows = pl.ds(r0, Q_TILE)
        cos, sin = cos_ref[rows, :], sin_ref[rows, :]
        cq = cq_ref[0, rows, :]
        cqn = (cq * _rms(cq) * gq_ref[...]).astype(BF16)
        ckv = ckv_ref[0, rows, :]
        ckvn = (ckv * _rms(ckv) * gkv_ref[...]).astype(BF16)
        k_rope = kr_ref[0, rows, :] * cos + krr_ref[0, rows, :] * sin
        two = lambda t: jnp.concatenate([t, t], axis=1)
        q_cos, q_sin, k_rope2 = two(cos * q_scale), two(sin * q_scale), two(k_rope)
        for p in range(N_HEADS // 2):
            cols = slice(p * 256, (p + 1) * 256)
            rot_cols = slice(rot0 + p * 256, rot0 + (p + 1) * 256)
            q2 = _dot(cqn, wuq_ref[:, cols]) * q_cos + _dot(cqn, wuq_ref[:, rot_cols]) * q_sin
            qt_scr[ci, p * 256:(p + 1) * 256, :] = q2.T.astype(BF16)
            k2 = (_dot(ckvn, wukv_ref[:, cols]) + k_rope2).astype(BF16)
            k_scr[2 * p, rows, :] = k2[:, :128]
            k_scr[2 * p + 1, rows, :] = k2[:, 128:]
        vt = _dot(ckvn, wukv_ref[:, v0:v0 + GROUP_WIDTH]).T.astype(BF16)
        for h in range(N_HEADS):
            vt_scr[h, ci, 0:HEAD_DIM, :] = vt[h * HEAD_DIM:(h + 1) * HEAD_DIM]
            vt_scr[h, ci, HEAD_DIM:, :] = _ones_row_block(Q_TILE)
        return carry

    lax.fori_loop(0, n_tiles, prep, 0, unroll=4)

    def score_chunks(qi, h):
        qt = qt_scr[qi, h * 128:(h + 1) * 128, :]
        return [_dot(k_scr[h, c * Q_TILE:(c + 1) * Q_TILE, :], qt) for c in range(n_tiles)]

    def value_chunk(qi, h, c):
        return vt_scr[h, c]

    _pipelined_softmax_pv(n_tiles, N_HEADS, n_tiles, score_chunks, value_chunk, _normalised_heads, s_scr, acc_scr,
                          o_ref)


def _mla_attention(proj, gq, gkv, wuq, wukv, cos, sin):
    b = proj.shape[0]
    tab = _const_spec((SEQ, 128))
    return pl.pallas_call(
        _mla_kernel,
        out_shape=jax.ShapeDtypeStruct((b, SEQ, GROUP_WIDTH), BF16),
        grid=(b,),
        in_specs=[
            _proj_spec(256, B256_ML_CQ), _proj_spec(128, B128_ML_CKV),
            _proj_spec(128, B128_ML_KR), _proj_spec(128, B128_ML_KRR),
            _const_spec((1, GROUP_WIDTH)), _const_spec((1, MLA_KV_LORA)),
            _const_spec(wuq.shape), _const_spec(wukv.shape), tab, tab,
        ],
        out_specs=pl.BlockSpec((1, SEQ, GROUP_WIDTH), lambda i: (i, 0, 0)),
        scratch_shapes=[pltpu.VMEM((SEQ // Q_TILE, N_HEADS * 128, Q_TILE), BF16),
                        pltpu.VMEM((N_HEADS, SEQ, 128), BF16),
                        pltpu.VMEM((N_HEADS, SEQ // Q_TILE, HEAD_DIM + 16, Q_TILE), BF16),
                        pltpu.VMEM((2, SEQ, Q_TILE), F32),
                        pltpu.VMEM((N_HEADS, HEAD_DIM + 16, Q_TILE), F32)],
        compiler_params=_compiler_params(),
        name="mla_attention",
    )(proj, proj, proj, proj, gq, gkv, wuq, wukv, cos, sin)


def _post_kernel(x_ref, a_ref, b_ref, c_ref, d_ref, wo_ref, g_mix_ref, g_pre_ref, wg_ref, wu_ref, wd_ref,
                 g_ffn_ref, o_ref, acc_ref):
    mix = (_dot(a_ref[...], wo_ref[0]) + _dot(b_ref[...], wo_ref[1])
           + _dot(c_ref[...], wo_ref[2]) + _dot(d_ref[...], wo_ref[3]))
    x = x_ref[...] + mix * _rms(mix) * g_mix_ref[...]
    h = (x * _rms(x) * g_pre_ref[...]).astype(BF16)
    acc_ref[...] = jnp.zeros_like(acc_ref)

    def body(ci, carry):
        gate = _dot(h, wg_ref[ci])
        up = _dot(h, wu_ref[ci])
        act = (gate * jax.nn.sigmoid(gate) * up).astype(BF16)
        acc_ref[...] += _dot(act, wd_ref[ci])
        return carry

    lax.fori_loop(0, D_FF // FF_CHUNK, body, 0, unroll=True)
    f = acc_ref[...]
    o_ref[...] = x + f * _rms(f) * g_ffn_ref[...]


def _post(x2d, a, b, c, d, wo, g_mix, g_pre, wg, wu, wd, g_ffn):
    t = x2d.shape[0]
    tok = lambda w: pl.BlockSpec((TOKEN_TILE, w), lambda i: (i, 0))
    vec = _const_spec((1, D_MODEL))
    return pl.pallas_call(
        _post_kernel,
        out_shape=jax.ShapeDtypeStruct((t, D_MODEL), F32),
        grid=(t // TOKEN_TILE,),
        in_specs=[
            tok(D_MODEL), tok(GROUP_WIDTH), tok(GROUP_WIDTH), tok(GROUP_WIDTH), tok(GROUP_WIDTH),
            _const_spec(wo.shape), vec, vec, _const_spec(wg.shape), _const_spec(wu.shape), _const_spec(wd.shape),
            vec,
        ],
        out_specs=tok(D_MODEL),
        scratch_shapes=[pltpu.VMEM((TOKEN_TILE, D_MODEL), F32)],
        compiler_params=_compiler_params(),
        name="outproj_swiglu",
    )(x2d, a, b, c, d, wo, g_mix, g_pre, wg, wu, wd, g_ffn)


def _layout_w_in(w):
    src64, sign64 = _rot_perm(HEAD_DIM, HEAD_DIM // 2)
    src_q = np.concatenate([h * HEAD_DIM + src64 for h in range(N_HEADS)])
    src_k = src_q[: 2 * HEAD_DIM]
    src32, sign32 = _rot_perm(MLA_ROPE_DIM, MLA_ROPE_DIM)
    gq_q = w[:, O_GQ_Q:O_GQ_K]
    gq_k = w[:, O_GQ_K:O_GQ_V]
    k_rope = w[:, O_ML_KR:O_ML_KR + MLA_ROPE_DIM]
    pad = lambda m: jnp.pad(m, ((0, 0), (MLA_NOPE_DIM, 128 - MLA_NOPE_DIM - MLA_ROPE_DIM)))
    cols = [
        w[:, :O_GQ_K],
        gq_q[:, src_q] * np.tile(sign64, N_HEADS),
        w[:, O_ML_CQ:O_ML_CKV],
        gq_k, gq_k[:, src_k] * np.tile(sign64, 2),
        w[:, O_GQ_V:O_ML_CQ],
        w[:, O_ML_CKV:O_ML_KR],
        pad(k_rope), pad(k_rope[:, src32] * sign32),
    ]
    return jnp.concatenate(cols, axis=1).astype(BF16)


def _layout_mla_weights(w_uq, w_ukv):
    src32, sign32 = _rot_perm(MLA_ROPE_DIM, MLA_ROPE_DIM)
    wq = w_uq.reshape(GROUP_WIDTH, N_HEADS, MLA_NOPE_DIM + MLA_ROPE_DIM)
    rope = wq[:, :, MLA_NOPE_DIM:]
    tail = 128 - MLA_NOPE_DIM - MLA_ROPE_DIM
    q_main = jnp.pad(wq, ((0, 0), (0, 0), (0, tail)))
    q_rot = jnp.pad(rope[:, :, src32] * sign32, ((0, 0), (0, 0), (MLA_NOPE_DIM, tail)))
    wuq = jnp.concatenate([q_main.reshape(GROUP_WIDTH, -1), q_rot.reshape(GROUP_WIDTH, -1)], axis=1)
    wkv = w_ukv.reshape(MLA_KV_LORA, N_HEADS, MLA_NOPE_DIM + HEAD_DIM)
    k_nope = jnp.pad(wkv[:, :, :MLA_NOPE_DIM], ((0, 0), (0, 0), (0, 128 - MLA_NOPE_DIM)))
    vals = wkv[:, :, MLA_NOPE_DIM:]
    wukv = jnp.concatenate([k_nope.reshape(MLA_KV_LORA, -1), vals.reshape(MLA_KV_LORA, -1)], axis=1)
    return wuq.astype(BF16), wukv.astype(BF16)


def _rotary_tables():
    pos = jnp.arange(SEQ)
    half = HEAD_DIM // 2
    inv = ROPE_THETA ** (-jnp.arange(0, half, 2, dtype=F32) / half)

    def angles(p):
        ang = p.astype(F32)[:, None] * inv[None, :]
        return jnp.concatenate([ang, ang], axis=-1)

    axial = jnp.concatenate([angles(pos // GRID_W), angles(pos % GRID_W)], axis=-1)
    seq = angles(pos)
    tail = 128 - MLA_NOPE_DIM - MLA_ROPE_DIM
    mla_cos = jnp.concatenate([jnp.ones((SEQ, MLA_NOPE_DIM), F32), jnp.cos(seq), jnp.zeros((SEQ, tail), F32)], -1)
    mla_sin = jnp.pad(jnp.sin(seq), ((0, 0), (MLA_NOPE_DIM, tail)))
    pair = lambda t: jnp.tile(t, (1, 2))
    return pair(jnp.cos(axial)), pair(jnp.sin(axial)), mla_cos, mla_sin


def kernel(x, pre_mix_norm, w_in, na_rel_bias, diff_lambda_q1, diff_lambda_k1, diff_lambda_q2, diff_lambda_k2,
           diff_subln, gqa_q_norm, gqa_k_norm, mla_q_norm, mla_kv_norm, mla_w_uq, mla_w_ukv, w_o, post_mix_norm,
           pre_ffn_norm, ffn_w_gate_up, ffn_w_down, post_ffn_norm):
    b, s, d = x.shape
    assert (s, d) == (SEQ, D_MODEL)
    depth = w_in.shape[0]
    src64, _ = _rot_perm(HEAD_DIM, HEAD_DIM // 2)
    ax_cos, ax_sin, mla_cos, mla_sin = _rotary_tables()
    row = lambda v: v.reshape(1, -1).astype(F32)
    n_chunks = D_FF // FF_CHUNK

    x2d = x.reshape(b * s, d)
    for l in range(depth):
        lambda_init = 0.8 - 0.6 * math.exp(-0.3 * l)
        proj = _inproj(x2d, row(pre_mix_norm[l]), _layout_w_in(w_in[l])).reshape(b, s, PROJ_WIDTH)

        a_out = _na_attention(proj, _na_tables(na_rel_bias[l]))
        b_out = _diff_attention(proj, row(diff_lambda_q1[l]), row(diff_lambda_k1[l]), row(diff_lambda_q2[l]),
                                row(diff_lambda_k2[l]), row(diff_subln[l]), lambda_init)
        pair_row = lambda v: row(jnp.tile(v, 2))
        c_out = _gqa_attention(proj, pair_row(gqa_q_norm[l]), pair_row(gqa_q_norm[l][src64]),
                               pair_row(gqa_k_norm[l]), pair_row(gqa_k_norm[l][src64]), ax_cos, ax_sin)
        wuq, wukv = _layout_mla_weights(mla_w_uq[l], mla_w_ukv[l])
        d_out = _mla_attention(proj, row(mla_q_norm[l]), row(mla_kv_norm[l]), wuq, wukv, mla_cos, mla_sin)

        gate_up = ffn_w_gate_up[l].astype(BF16)
        wg = gate_up[:, :D_FF].reshape(d, n_chunks, FF_CHUNK).transpose(1, 0, 2)
        wu = gate_up[:, D_FF:].reshape(d, n_chunks, FF_CHUNK).transpose(1, 0, 2)
        wd = ffn_w_down[l].astype(BF16).reshape(n_chunks, FF_CHUNK, d)
        wo = w_o[l].astype(BF16).reshape(N_HEADS, GROUP_WIDTH, d)
        flat = lambda t: t.reshape(b * s, GROUP_WIDTH)
        x2d = _post(x2d, flat(a_out), flat(b_out), flat(c_out), flat(d_out), wo, row(post_mix_norm[l]),
                    row(pre_ffn_norm[l]), wg, wu, wd, row(post_ffn_norm[l]))
    return x2d.reshape(b, s, d)
```

```python
import functools
import math

import numpy as np
import jax
import jax.numpy as jnp
from jax import lax
from jax.experimental import pallas as pl
from jax.experimental.pallas import tpu as pltpu

F32 = jnp.float32
BF16 = jnp.bfloat16

D_MODEL = 1024
SEQ = 2048
GRID_W = 64
GRID_ROWS = SEQ // GRID_W
HEAD_DIM = 64
N_HEADS = 4
GROUP_WIDTH = 256
EPS = 1e-6
ROPE_THETA = 10000.0

NA_WIN_ROWS = 8
NA_WIN_COLS = 16
DIFF_QK_DIM = 32
POS_SPLIT = 64
MLA_NOPE_DIM = 64
MLA_ROPE_DIM = 32
MLA_KV_LORA = 128
D_FF = 2816

O_GQ_Q, O_GQ_K, O_GQ_V, O_ML_CQ, O_ML_CKV, O_ML_KR = 1536, 1792, 1920, 2048, 2304, 2432

PROJ_WIDTH = 2560
B256_NA_Q, B256_NA_K, B256_NA_V, B256_DF_Q, B256_DF_K, B256_DF_V, B256_GQ_Q, B256_ML_CQ = range(8)
B128_GQ_K, B128_GQ_V, B128_ML_CKV, B128_ML_KR = range(16, 20)

VMEM_LIMIT_BYTES = 56 * 1024 * 1024

TOKEN_TILE = 1024
FF_CHUNK = 256
Q_TILE = 256
NA_GROUP_ROWS = 4
NA_BAND_ROWS = 12
NEG_BIG = -1e30
LOG2E = math.log2(math.e)


def _rot_perm(width, group):
    j = np.arange(width)
    jj = j % group
    half = group // 2
    src = (j // group) * group + (jj + half) % group
    sign = np.where(jj < half, -1.0, 1.0).astype(np.float32)
    return src, sign


def _compiler_params():
    return pltpu.CompilerParams(dimension_semantics=("arbitrary",), vmem_limit_bytes=VMEM_LIMIT_BYTES)


def _const_spec(shape):
    zeros = (0,) * len(shape)
    return pl.BlockSpec(shape, lambda i: zeros, pipeline_mode=pl.Buffered(1))


def _rms(x):
    return lax.rsqrt(jnp.mean(x * x, axis=-1, keepdims=True) + EPS)


def _dot(a, b):
    return jnp.dot(a, b, preferred_element_type=F32)


def _inproj_kernel(x_ref, g_ref, w_ref, o_ref):
    x = x_ref[...]
    h = (x * _rms(x) * g_ref[...]).astype(BF16)
    o_ref[...] = _dot(h, w_ref[...])


def _inproj(x2d, gain, w):
    t = x2d.shape[0]
    return pl.pallas_call(
        _inproj_kernel,
        out_shape=jax.ShapeDtypeStruct((t, PROJ_WIDTH), F32),
        grid=(t // TOKEN_TILE,),
        in_specs=[
            pl.BlockSpec((TOKEN_TILE, D_MODEL), lambda i: (i, 0)),
            _const_spec((1, D_MODEL)),
            _const_spec((D_MODEL, PROJ_WIDTH)),
        ],
        out_specs=pl.BlockSpec((TOKEN_TILE, PROJ_WIDTH), lambda i: (i, 0)),
        compiler_params=_compiler_params(),
        name="inproj",
    )(x2d, gain, w)


def _segment_mean_square(x, seg):
    w = x.shape[-1]
    same = (lax.broadcasted_iota(jnp.int32, (w, w), 0) // seg) == (lax.broadcasted_iota(jnp.int32, (w, w), 1) // seg)
    ones = jnp.where(same, 1.0, 0.0).astype(BF16)
    sq = x * x
    hi = sq.astype(BF16)
    lo = (sq - hi.astype(F32)).astype(BF16)
    return (_dot(hi, ones) + _dot(lo, ones)) * (1.0 / seg)


def _ones_row_block(width):
    return jnp.where(lax.broadcasted_iota(jnp.int32, (16, width), 0) == 0, 1.0, 0.0).astype(BF16)


def _normalised_heads(accs):
    return jnp.concatenate([acc[:HEAD_DIM] / acc[HEAD_DIM:HEAD_DIM + 1] for acc in accs], axis=0)


def _pipelined_softmax_pv(n_tiles, n_heads, n_chunks, score_chunks, value_chunk, combine, s_scr, acc_scr, o_ref,
                          exp_fn=jnp.exp2):
    assert n_heads % 2 == 0
    rows = s_scr.shape[1] // n_chunks

    def scores_into(slot, qi, h):
        m8 = None
        for c, st in enumerate(score_chunks(qi, h)):
            s_scr[slot, c * rows:(c + 1) * rows, :] = st
            cm = jnp.max(st.reshape(-1, 8, st.shape[-1]), axis=0)
            m8 = cm if m8 is None else jnp.maximum(m8, cm)
        return jnp.max(m8, axis=0, keepdims=True)

    def write_tile(qi):
        r0 = qi * Q_TILE if isinstance(qi, int) else pl.multiple_of(qi * Q_TILE, Q_TILE)
        o_ref[0, pl.ds(r0, Q_TILE), :] = combine([acc_scr[h] for h in range(n_heads)]).T.astype(o_ref.dtype)

    acc_scr[...] = jnp.ones_like(acc_scr)

    def body(qi, m):
        next_qi = jnp.minimum(qi + 1, n_tiles - 1)
        for h in range(n_heads):
            slot = h % 2
            nq, nh = (qi, h + 1) if h + 1 < n_heads else (next_qi, 0)
            next_m = scores_into(1 - slot, nq, nh)
            if h == 0:
                write_tile(jnp.maximum(qi - 1, 0))
            acc = None
            for c in range(n_chunks):
                p = exp_fn(s_scr[slot, c * rows:(c + 1) * rows, :] - m).astype(BF16)
                part = _dot(value_chunk(qi, h, c), p)
                acc = part if acc is None else acc + part
            acc_scr[h] = acc
            m = next_m
        return m

    lax.fori_loop(0, n_tiles, body, scores_into(0, 0, 0), unroll=2)
    write_tile(n_tiles - 1)


def _proj_spec(width, block):
    return pl.BlockSpec((1, SEQ, width), lambda b: (b, 0, block))


def _na_kernel(q_ref, k_ref, v_ref, tab_ref, o_ref, qt_scr, k_scr, vt_scr, s_scr, acc_scr):
    assert NA_GROUP_ROWS * GRID_W == Q_TILE and NA_BAND_ROWS % NA_GROUP_ROWS == 0
    n_tiles = SEQ // Q_TILE
    band_chunks = NA_BAND_ROWS // NA_GROUP_ROWS
    pair = 2 * HEAD_DIM
    q_scale = HEAD_DIM ** -0.5 * LOG2E

    def prep(ci, carry):
        r0 = pl.multiple_of(ci * Q_TILE, Q_TILE)
        rows = pl.ds(r0, Q_TILE)
        qt = (q_ref[0, rows, :] * q_scale).T
        row = lax.broadcasted_iota(jnp.int32, (pair, Q_TILE), 0)
        for h in range(N_HEADS):
            own = (row >= (h % 2) * HEAD_DIM) & (row < (h % 2 + 1) * HEAD_DIM)
            qt_scr[ci, h] = jnp.where(own, qt[(h // 2) * pair:(h // 2 + 1) * pair], 0.0).astype(BF16)
        for p in range(N_HEADS // 2):
            k_scr[p, rows, :] = k_ref[0, rows, p * pair:(p + 1) * pair].astype(BF16)
        vt = v_ref[0, rows, :].T.astype(BF16)
        for h in range(N_HEADS):
            vt_scr[h, ci, 0:HEAD_DIM, :] = vt[h * HEAD_DIM:(h + 1) * HEAD_DIM]
            vt_scr[h, ci, HEAD_DIM:, :] = _ones_row_block(Q_TILE)
        return carry

    lax.fori_loop(0, n_tiles, prep, 0, unroll=4)

    def key_chunk(gi, c):
        return jnp.clip(gi - (NA_WIN_ROWS // 2) // NA_GROUP_ROWS, 0, n_tiles - band_chunks) + c

    def score_chunks(gi, h):
        kind = jnp.where(gi == 0, 0, jnp.where(gi == n_tiles - 1, 2, 1))
        k0 = pl.multiple_of(key_chunk(gi, 0) * Q_TILE, Q_TILE)
        return [_dot(k_scr[h // 2, pl.ds(k0 + c * Q_TILE, Q_TILE), :], qt_scr[gi, h])
                + tab_ref[kind, h, c * Q_TILE:(c + 1) * Q_TILE, :] for c in range(band_chunks)]

    def value_chunk(gi, h, c):
        return vt_scr[h, key_chunk(gi, c)]

    _pipelined_softmax_pv(n_tiles, N_HEADS, band_chunks, score_chunks, value_chunk, _normalised_heads, s_scr, acc_scr,
                          o_ref)


def _na_tables(rel_bias):
    a = np.arange(NA_GROUP_ROWS)[:, None, None, None]
    c = np.arange(GRID_W)[None, :, None, None]
    i = np.arange(NA_BAND_ROWS)[None, None, :, None]
    kc = np.arange(GRID_W)[None, None, None, :]
    cs = np.clip(c - NA_WIN_COLS // 2, 0, GRID_W - NA_WIN_COLS)
    col_ok = (kc >= cs) & (kc < cs + NA_WIN_COLS)
    dc = kc - c + NA_WIN_COLS - 1
    col_sel = (dc[..., None] == np.arange(2 * NA_WIN_COLS - 1)) & col_ok[..., None]
    row_sels = []
    last_r0 = GRID_ROWS - NA_GROUP_ROWS
    for r0, band_row in ((0, 0), (NA_GROUP_ROWS, 0), (last_r0, GRID_ROWS - NA_BAND_ROWS)):
        r = r0 + a
        rs = np.clip(r - NA_WIN_ROWS // 2, 0, GRID_ROWS - NA_WIN_ROWS)
        key_row = band_row + i
        row_ok = (key_row >= rs) & (key_row < rs + NA_WIN_ROWS)
        dr = key_row - r + NA_WIN_ROWS - 1
        row_sels.append((dr[..., None] == np.arange(2 * NA_WIN_ROWS - 1)) & row_ok[..., None])
    row_sel = np.stack(row_sels)[:, :, 0, :, 0, :].astype(np.float32)
    col_sel = col_sel[0, :, 0, :, :].astype(np.float32)
    vals = jnp.einsum("taiu,huv,ckv->thikac", row_sel, rel_bias.astype(F32), col_sel,
                      precision=lax.Precision.HIGHEST)
    inside = np.einsum("taiu,ckv->tikac", row_sel, col_sel) > 0
    tab = jnp.where(inside[:, None], vals * LOG2E, NEG_BIG)
    q_rows, band = NA_GROUP_ROWS * GRID_W, NA_BAND_ROWS * GRID_W
    return tab.reshape(3, N_HEADS, band, q_rows)


def _na_attention(proj, tables):
    b = proj.shape[0]
    return pl.pallas_call(
        _na_kernel,
        out_shape=jax.ShapeDtypeStruct((b, SEQ, GROUP_WIDTH), BF16),
        grid=(b,),
        in_specs=[
            _proj_spec(256, B256_NA_Q), _proj_spec(256, B256_NA_K), _proj_spec(256, B256_NA_V),
            _const_spec(tables.shape),
        ],
        out_specs=pl.BlockSpec((1, SEQ, GROUP_WIDTH), lambda i: (i, 0, 0)),
        scratch_shapes=[pltpu.VMEM((SEQ // Q_TILE, N_HEADS, 2 * HEAD_DIM, Q_TILE), BF16),
                        pltpu.VMEM((N_HEADS // 2, SEQ, 2 * HEAD_DIM), BF16),
                        pltpu.VMEM((N_HEADS, SEQ // Q_TILE, HEAD_DIM + 16, Q_TILE), BF16),
                        pltpu.VMEM((2, NA_BAND_ROWS * GRID_W, Q_TILE), F32),
                        pltpu.VMEM((N_HEADS, HEAD_DIM + 16, Q_TILE), F32)],
        compiler_params=_compiler_params(),
        name="na_attention",
    )(proj, proj, proj, tables)


def _alibi_slope(h):
    return 2.0 ** (-8.0 * (h + 1) / N_HEADS)


def _diff_kernel(lambda_init, q_ref, k_ref, v_ref, lq1_ref, lk1_ref, lq2_ref, lk2_ref, subln_ref, diag_ref, o_ref,
                 qt_scr, qf_scr, k_scr, vt_scr, s_scr, acc_scr):
    scale = DIFF_QK_DIM ** -0.5
    n_tiles = SEQ // Q_TILE
    pair = 2 * HEAD_DIM
    n_feat = 16
    lam = (jnp.exp(jnp.sum(lq1_ref[...] * lk1_ref[...], axis=-1, keepdims=True))
           - jnp.exp(jnp.sum(lq2_ref[...] * lk2_ref[...], axis=-1, keepdims=True)) + lambda_init)

    def prep(ci, carry):
        r0 = pl.multiple_of(ci * Q_TILE, Q_TILE)
        rows = pl.ds(r0, Q_TILE)
        qt = (q_ref[0, rows, :] * scale).T
        row = lax.broadcasted_iota(jnp.int32, (pair, Q_TILE), 0)
        feat_row = lax.broadcasted_iota(jnp.int32, (n_feat, Q_TILE), 0)
        i = r0 + lax.broadcasted_iota(jnp.int32, (n_feat, Q_TILE), 1)
        i_lo = i % POS_SPLIT
        i_hi = i - i_lo
        for h in range(N_HEADS):
            group = qt[(h // 2) * pair:(h // 2 + 1) * pair]
            for mp in range(2):
                lo = (h % 2) * HEAD_DIM + mp * DIFF_QK_DIM
                keep = (row >= lo) & (row < lo + DIFF_QK_DIM)
                qt_scr[ci, 2 * h + mp] = jnp.where(keep, group, 0.0).astype(BF16)
            slope = _alibi_slope(h)
            feat = jnp.where(feat_row < 2, slope,
                             jnp.where(feat_row == 2, -slope * i_hi.astype(F32),
                                       jnp.where(feat_row == 3, -slope * i_lo.astype(F32), 0.0)))
            qf_scr[ci, h, 0:n_feat, :] = feat.astype(BF16)
            qf_scr[ci, h, n_feat:, :] = (-feat).astype(BF16)
        lane = lax.broadcasted_iota(jnp.int32, (Q_TILE, pair), 1)
        j = r0 + lax.broadcasted_iota(jnp.int32, (Q_TILE, pair), 0)
        j_lo = j % POS_SPLIT
        k_feat = jnp.where(lane == 0, (j - j_lo).astype(F32),
                           jnp.where(lane == 1, j_lo.astype(F32), jnp.where(lane < 4, 1.0, 0.0))).astype(BF16)
        for p in range(N_HEADS // 2):
            k_scr[p, rows, 0:pair] = k_ref[0, rows, p * pair:(p + 1) * pair].astype(BF16)
            k_scr[p, rows, pair:] = k_feat
        vt = v_ref[0, rows, :].T.astype(BF16)
        for h in range(N_HEADS):
            vt_scr[h, ci, 0:HEAD_DIM, :] = vt[h * HEAD_DIM:(h + 1) * HEAD_DIM]
            vt_scr[h, ci, HEAD_DIM:, :] = _ones_row_block(Q_TILE)
        return carry

    lax.fori_loop(0, n_tiles, prep, 0, unroll=4)

    def key_chunk(qi, c):
        wrapped = qi + c >= n_tiles
        return jnp.where(wrapped, qi + c - n_tiles, qi + c), wrapped

    def score_chunk(qi, item, c):
        h = item // 2
        kc, left = key_chunk(qi, c)
        form = 0 if c == 0 else pl.multiple_of(jnp.where(left, 0, n_feat), n_feat)
        feat = qf_scr[qi, h, pl.ds(form, n_feat), :]
        rhs = jnp.concatenate([qt_scr[qi, item], feat, jnp.zeros((pair - n_feat, Q_TILE), BF16)], axis=0)
        st = _dot(k_scr[h // 2, pl.ds(pl.multiple_of(kc * Q_TILE, Q_TILE), Q_TILE), :], rhs)
        return st + diag_ref[h] if c == 0 else st

    def score_chunks(qi, item):
        return [score_chunk(qi, item, c) for c in range(n_tiles)]

    def value_chunk(qi, item, c):
        return vt_scr[item // 2, key_chunk(qi, c)[0]]

    def combine(accs):
        outs = []
        for h in range(N_HEADS):
            a1, a2 = accs[2 * h], accs[2 * h + 1]
            o = a1[:HEAD_DIM] / a1[HEAD_DIM:HEAD_DIM + 1] - lam * (a2[:HEAD_DIM] / a2[HEAD_DIM:HEAD_DIM + 1])
            r = lax.rsqrt(jnp.mean(o * o, axis=0, keepdims=True) + EPS)
            outs.append(o * r * subln_ref[...] * (1.0 - lambda_init))
        return jnp.concatenate(outs, axis=0)

    _pipelined_softmax_pv(n_tiles, 2 * N_HEADS, n_tiles, score_chunks, value_chunk, combine, s_scr, acc_scr, o_ref,
                          exp_fn=jnp.exp)


def _diff_attention(proj, lq1, lk1, lq2, lk2, subln, lambda_init):
    b = proj.shape[0]
    vec = _const_spec((1, DIFF_QK_DIM))
    n_tiles = SEQ // Q_TILE
    rel = np.arange(Q_TILE)
    over = -2.0 * np.maximum(rel[:, None] - rel[None, :], 0).astype(np.float32)
    diag = jnp.asarray(np.stack([_alibi_slope(h) * over for h in range(N_HEADS)]))
    subln_cols = jnp.broadcast_to(subln.reshape(HEAD_DIM, 1), (HEAD_DIM, Q_TILE))
    return pl.pallas_call(
        functools.partial(_diff_kernel, lambda_init),
        out_shape=jax.ShapeDtypeStruct((b, SEQ, GROUP_WIDTH), BF16),
        grid=(b,),
        in_specs=[
            _proj_spec(256, B256_DF_Q), _proj_spec(256, B256_DF_K), _proj_spec(256, B256_DF_V),
            vec, vec, vec, vec, _const_spec((HEAD_DIM, Q_TILE)), _const_spec(diag.shape),
        ],
        out_specs=pl.BlockSpec((1, SEQ, GROUP_WIDTH), lambda i: (i, 0, 0)),
        scratch_shapes=[pltpu.VMEM((n_tiles, 2 * N_HEADS, 2 * HEAD_DIM, Q_TILE), BF16),
                        pltpu.VMEM((n_tiles, N_HEADS, 32, Q_TILE), BF16),
                        pltpu.VMEM((N_HEADS // 2, SEQ, 4 * HEAD_DIM), BF16),
                        pltpu.VMEM((N_HEADS, n_tiles, HEAD_DIM + 16, Q_TILE), BF16),
                        pltpu.VMEM((2, SEQ, Q_TILE), F32),
                        pltpu.VMEM((2 * N_HEADS, HEAD_DIM + 16, Q_TILE), F32)],
        compiler_params=_compiler_params(),
        name="diff_attention",
    )(proj, proj, proj, lq1, lk1, lq2, lk2, subln_cols, diag)


def _swap_rotary_halves(x):
    quarter = HEAD_DIM // 4
    first = (lax.broadcasted_iota(jnp.int32, x.shape, 1) % (2 * quarter)) < quarter
    lanes = x.shape[1]
    return jnp.where(first, pltpu.roll(x, lanes - quarter, 1), pltpu.roll(x, quarter, 1))


def _gqa_kernel(q_ref, k_ref, v_ref, qc_ref, qs_ref, kc_ref, ks_ref, o_ref,
                qt_scr, k_scr, vt_scr, s_scr, acc_scr):
    kv_heads = N_HEADS // 2
    n_tiles = SEQ // Q_TILE
    pair = 2 * HEAD_DIM

    def normed_rotary(x, gain_cos, gain_sin):
        rotated = x * gain_cos + _swap_rotary_halves(x) * gain_sin
        return rotated * lax.rsqrt(_segment_mean_square(x, HEAD_DIM) + EPS)

    def prep(ci, carry):
        r0 = pl.multiple_of(ci * Q_TILE, Q_TILE)
        rows = pl.ds(r0, Q_TILE)
        zeros = jnp.zeros((HEAD_DIM, Q_TILE), BF16)
        for half in range(2):
            lanes = slice(half * pair, (half + 1) * pair)
            y = normed_rotary(q_ref[0, rows, lanes], qc_ref[rows, :], qs_ref[rows, :])
            yt = y.T.astype(BF16)
            for j in range(2):
                h = 2 * half + j
                kv = h // (N_HEADS // kv_heads)
                for part in range(kv_heads):
                    block = yt[j * HEAD_DIM:(j + 1) * HEAD_DIM] if part == kv else zeros
                    qt_scr[ci, h * pair + part * HEAD_DIM:h * pair + (part + 1) * HEAD_DIM, :] = block
        k_scr[rows, :] = normed_rotary(k_ref[0, rows, :], kc_ref[rows, :], ks_ref[rows, :]).astype(BF16)
        vt = v_ref[0, rows, :].T.astype(BF16)
        for kv in range(kv_heads):
            vt_scr[kv, ci, 0:HEAD_DIM, :] = vt[kv * HEAD_DIM:(kv + 1) * HEAD_DIM]
            vt_scr[kv, ci, HEAD_DIM:, :] = _ones_row_block(Q_TILE)
        return carry

    lax.fori_loop(0, n_tiles, prep, 0, unroll=4)

    def score_chunks(qi, h):
        qt = qt_scr[qi, h * pair:(h + 1) * pair, :]
        return [_dot(k_scr[c * Q_TILE:(c + 1) * Q_TILE, :], qt) for c in range(n_tiles)]

    def value_chunk(qi, h, c):
        return vt_scr[h // (N_HEADS // kv_heads), c]

    _pipelined_softmax_pv(n_tiles, N_HEADS, n_tiles, score_chunks, value_chunk, _normalised_heads, s_scr, acc_scr,
                          o_ref)


def _gqa_attention(proj, gq, gqp, gk, gkp, cos, sin):
    b = proj.shape[0]
    tab = _const_spec((SEQ, 2 * HEAD_DIM))
    q_scale = HEAD_DIM ** -0.5 * LOG2E
    _, sign = _rot_perm(2 * HEAD_DIM, HEAD_DIM // 2)
    tables = (gq * cos * q_scale, gqp * (sign * q_scale) * sin, gk * cos, gkp * sign * sin)
    return pl.pallas_call(
        _gqa_kernel,
        out_shape=jax.ShapeDtypeStruct((b, SEQ, GROUP_WIDTH), BF16),
        grid=(b,),
        in_specs=[
            _proj_spec(256, B256_GQ_Q), _proj_spec(128, B128_GQ_K), _proj_spec(128, B128_GQ_V),
            tab, tab, tab, tab,
        ],
        out_specs=pl.BlockSpec((1, SEQ, GROUP_WIDTH), lambda i: (i, 0, 0)),
        scratch_shapes=[pltpu.VMEM((SEQ // Q_TILE, N_HEADS * 2 * HEAD_DIM, Q_TILE), BF16),
                        pltpu.VMEM((SEQ, 2 * HEAD_DIM), BF16),
                        pltpu.VMEM((N_HEADS // 2, SEQ // Q_TILE, HEAD_DIM + 16, Q_TILE), BF16),
                        pltpu.VMEM((2, SEQ, Q_TILE), F32),
                        pltpu.VMEM((N_HEADS, HEAD_DIM + 16, Q_TILE), F32)],
        compiler_params=_compiler_params(),
        name="gqa_attention",
    )(proj, proj, proj, *tables)


def _mla_kernel(cq_ref, ckv_ref, kr_ref, gq_ref, gkv_ref, wuq_ref, wukv_ref, cos_ref, sin_ref, o_ref,
                qt_scr, k_scr, vt_scr, s_scr, acc_scr):
    q_scale = (MLA_NOPE_DIM + MLA_ROPE_DIM) ** -0.5 * LOG2E
    n_tiles = SEQ // Q_TILE
    rot0 = N_HEADS * 128
    v0 = N_HEADS * 128

    def prep(ci, carry):
        r0 = pl.multiple_of(ci * Q_TILE, Q_TILE)
        rows = pl.ds(r0, Q_TILE)
        cos, sin = cos_ref[rows, :], sin_ref[rows, :]
        cq = cq_ref[0, rows, :]
        cqn = (cq * _rms(cq) * gq_ref[...]).astype(BF16)
        ckv = ckv_ref[0, rows, :]
        ckvn = (ckv * _rms(ckv) * gkv_ref[...]).astype(BF16)
        kr = kr_ref[0, rows, :]
        k_rope = kr * cos + pltpu.roll(kr, 128 - MLA_ROPE_DIM, 1) * sin
        two = lambda t: jnp.concatenate([t, t], axis=1)
        q_cos, q_sin, k_rope2 = two(cos * q_scale), two(sin * q_scale), two(k_rope)
        for p in range(N_HEADS // 2):
            cols = slice(p * 256, (p + 1) * 256)
            rot_cols = slice(rot0 + p * 256, rot0 + (p + 1) * 256)
            q2 = _dot(cqn, wuq_ref[:, cols]) * q_cos + _dot(cqn, wuq_ref[:, rot_cols]) * q_sin
            qt_scr[ci, p * 256:(p + 1) * 256, :] = q2.T.astype(BF16)
            k2 = (_dot(ckvn, wukv_ref[:, cols]) + k_rope2).astype(BF16)
            k_scr[2 * p, rows, :] = k2[:, :128]
            k_scr[2 * p + 1, rows, :] = k2[:, 128:]
        vt = _dot(ckvn, wukv_ref[:, v0:v0 + GROUP_WIDTH]).T.astype(BF16)
        for h in range(N_HEADS):
            vt_scr[h, ci, 0:HEAD_DIM, :] = vt[h * HEAD_DIM:(h + 1) * HEAD_DIM]
            vt_scr[h, ci, HEAD_DIM:, :] = _ones_row_block(Q_TILE)
        return carry

    lax.fori_loop(0, n_tiles, prep, 0, unroll=4)

    def score_chunks(qi, h):
        qt = qt_scr[qi, h * 128:(h + 1) * 128, :]
        return [_dot(k_scr[h, c * Q_TILE:(c + 1) * Q_TILE, :], qt) for c in range(n_tiles)]

    def value_chunk(qi, h, c):
        return vt_scr[h, c]

    _pipelined_softmax_pv(n_tiles, N_HEADS, n_tiles, score_chunks, value_chunk, _normalised_heads, s_scr, acc_scr,
                          o_ref)


def _mla_attention(proj, gq, gkv, wuq, wukv, cos, sin):
    b = proj.shape[0]
    tab = _const_spec((SEQ, 128))
    return pl.pallas_call(
        _mla_kernel,
        out_shape=jax.ShapeDtypeStruct((b, SEQ, GROUP_WIDTH), BF16),
        grid=(b,),
        in_specs=[
            _proj_spec(256, B256_ML_CQ), _proj_spec(128, B128_ML_CKV), _proj_spec(128, B128_ML_KR),
            _const_spec((1, GROUP_WIDTH)), _const_spec((1, MLA_KV_LORA)),
            _const_spec(wuq.shape), _const_spec(wukv.shape), tab, tab,
        ],
        out_specs=pl.BlockSpec((1, SEQ, GROUP_WIDTH), lambda i: (i, 0, 0)),
        scratch_shapes=[pltpu.VMEM((SEQ // Q_TILE, N_HEADS * 128, Q_TILE), BF16),
                        pltpu.VMEM((N_HEADS, SEQ, 128), BF16),
                        pltpu.VMEM((N_HEADS, SEQ // Q_TILE, HEAD_DIM + 16, Q_TILE), BF16),
                        pltpu.VMEM((2, SEQ, Q_TILE), F32),
                        pltpu.VMEM((N_HEADS, HEAD_DIM + 16, Q_TILE), F32)],
        compiler_params=_compiler_params(),
        name="mla_attention",
    )(proj, proj, proj, gq, gkv, wuq, wukv, cos, sin)


def _post_kernel(x_ref, a_ref, b_ref, c_ref, d_ref, wo_ref, g_mix_ref, g_pre_ref, wg_ref, wu_ref, wd_ref,
                 g_ffn_ref, o_ref, acc_ref):
    mix = (_dot(a_ref[...], wo_ref[0]) + _dot(b_ref[...], wo_ref[1])
           + _dot(c_ref[...], wo_ref[2]) + _dot(d_ref[...], wo_ref[3]))
    x = x_ref[...] + mix * _rms(mix) * g_mix_ref[...]
    h = (x * _rms(x) * g_pre_ref[...]).astype(BF16)
    acc_ref[...] = jnp.zeros_like(acc_ref)

    def body(ci, carry):
        gate = _dot(h, wg_ref[ci])
        up = _dot(h, wu_ref[ci])
        act = (gate * jax.nn.sigmoid(gate) * up).astype(BF16)
        acc_ref[...] += _dot(act, wd_ref[ci])
        return carry

    lax.fori_loop(0, D_FF // FF_CHUNK, body, 0, unroll=True)
    f = acc_ref[...]
    o_ref[...] = x + f * _rms(f) * g_ffn_ref[...]


def _post(x2d, a, b, c, d, wo, g_mix, g_pre, wg, wu, wd, g_ffn):
    t = x2d.shape[0]
    tok = lambda w: pl.BlockSpec((TOKEN_TILE, w), lambda i: (i, 0))
    vec = _const_spec((1, D_MODEL))
    return pl.pallas_call(
        _post_kernel,
        out_shape=jax.ShapeDtypeStruct((t, D_MODEL), F32),
        grid=(t // TOKEN_TILE,),
        in_specs=[
            tok(D_MODEL), tok(GROUP_WIDTH), tok(GROUP_WIDTH), tok(GROUP_WIDTH), tok(GROUP_WIDTH),
            _const_spec(wo.shape), vec, vec, _const_spec(wg.shape), _const_spec(wu.shape), _const_spec(wd.shape),
            vec,
        ],
        out_specs=tok(D_MODEL),
        scratch_shapes=[pltpu.VMEM((TOKEN_TILE, D_MODEL), F32)],
        compiler_params=_compiler_params(),
        name="outproj_swiglu",
    )(x2d, a, b, c, d, wo, g_mix, g_pre, wg, wu, wd, g_ffn)


def _layout_w_in(w):
    src32, sign32 = _rot_perm(MLA_ROPE_DIM, MLA_ROPE_DIM)
    k_rope = w[:, O_ML_KR:O_ML_KR + MLA_ROPE_DIM]
    cols = [
        w[:, :O_GQ_K],
        w[:, O_ML_CQ:O_ML_CKV],
        w[:, O_GQ_K:O_ML_CQ],
        w[:, O_ML_CKV:O_ML_KR],
        jnp.zeros((D_MODEL, MLA_NOPE_DIM), w.dtype), k_rope, k_rope[:, src32] * sign32,
    ]
    return jnp.concatenate(cols, axis=1).astype(BF16)


def _layout_mla_weights(w_uq, w_ukv):
    src32, sign32 = _rot_perm(MLA_ROPE_DIM, MLA_ROPE_DIM)
    wq = w_uq.reshape(GROUP_WIDTH, N_HEADS, MLA_NOPE_DIM + MLA_ROPE_DIM)
    rope = wq[:, :, MLA_NOPE_DIM:]
    tail = 128 - MLA_NOPE_DIM - MLA_ROPE_DIM
    q_main = jnp.pad(wq, ((0, 0), (0, 0), (0, tail)))
    q_rot = jnp.pad(rope[:, :, src32] * sign32, ((0, 0), (0, 0), (MLA_NOPE_DIM, tail)))
    wuq = jnp.concatenate([q_main.reshape(GROUP_WIDTH, -1), q_rot.reshape(GROUP_WIDTH, -1)], axis=1)
    wkv = w_ukv.reshape(MLA_KV_LORA, N_HEADS, MLA_NOPE_DIM + HEAD_DIM)
    k_nope = jnp.pad(wkv[:, :, :MLA_NOPE_DIM], ((0, 0), (0, 0), (0, 128 - MLA_NOPE_DIM)))
    vals = wkv[:, :, MLA_NOPE_DIM:]
    wukv = jnp.concatenate([k_nope.reshape(MLA_KV_LORA, -1), vals.reshape(MLA_KV_LORA, -1)], axis=1)
    return wuq.astype(BF16), wukv.astype(BF16)


def _rotary_tables():
    pos = jnp.arange(SEQ)
    half = HEAD_DIM // 2
    inv = ROPE_THETA ** (-jnp.arange(0, half, 2, dtype=F32) / half)

    def angles(p):
        ang = p.astype(F32)[:, None] * inv[None, :]
        return jnp.concatenate([ang, ang], axis=-1)

    axial = jnp.concatenate([angles(pos // GRID_W), angles(pos % GRID_W)], axis=-1)
    seq = angles(pos)
    tail = 128 - MLA_NOPE_DIM - MLA_ROPE_DIM
    mla_cos = jnp.concatenate([jnp.ones((SEQ, MLA_NOPE_DIM), F32), jnp.cos(seq), jnp.zeros((SEQ, tail), F32)], -1)
    mla_sin = jnp.pad(jnp.sin(seq), ((0, 0), (MLA_NOPE_DIM, tail)))
    pair = lambda t: jnp.tile(t, (1, 2))
    return pair(jnp.cos(axial)), pair(jnp.sin(axial)), mla_cos, mla_sin


def kernel(x, pre_mix_norm, w_in, na_rel_bias, diff_lambda_q1, diff_lambda_k1, diff_lambda_q2, diff_lambda_k2,
           diff_subln, gqa_q_norm, gqa_k_norm, mla_q_norm, mla_kv_norm, mla_w_uq, mla_w_ukv, w_o, post_mix_norm,
           pre_ffn_norm, ffn_w_gate_up, ffn_w_down, post_ffn_norm):
    b, s, d = x.shape
    assert (s, d) == (SEQ, D_MODEL)
    depth = w_in.shape[0]
    src64, _ = _rot_perm(HEAD_DIM, HEAD_DIM // 2)
    ax_cos, ax_sin, mla_cos, mla_sin = _rotary_tables()
    row = lambda v: v.reshape(1, -1).astype(F32)
    n_chunks = D_FF // FF_CHUNK

    x2d = x.reshape(b * s, d)
    for l in range(depth):
        lambda_init = 0.8 - 0.6 * math.exp(-0.3 * l)
        proj = _inproj(x2d, row(pre_mix_norm[l]), _layout_w_in(w_in[l])).reshape(b, s, PROJ_WIDTH)

        a_out = _na_attention(proj, _na_tables(na_rel_bias[l]))
        b_out = _diff_attention(proj, row(diff_lambda_q1[l]), row(diff_lambda_k1[l]), row(diff_lambda_q2[l]),
                                row(diff_lambda_k2[l]), row(diff_subln[l]), lambda_init)
        pair_row = lambda v: row(jnp.tile(v, 2))
        c_out = _gqa_attention(proj, pair_row(gqa_q_norm[l]), pair_row(gqa_q_norm[l][src64]),
                               pair_row(gqa_k_norm[l]), pair_row(gqa_k_norm[l][src64]), ax_cos, ax_sin)
        wuq, wukv = _layout_mla_weights(mla_w_uq[l], mla_w_ukv[l])
        d_out = _mla_attention(proj, row(mla_q_norm[l]), row(mla_kv_norm[l]), wuq, wukv, mla_cos, mla_sin)

        gate_up = ffn_w_gate_up[l].astype(BF16)
        wg = gate_up[:, :D_FF].reshape(d, n_chunks, FF_CHUNK).transpose(1, 0, 2)
        wu = gate_up[:, D_FF:].reshape(d, n_chunks, FF_CHUNK).transpose(1, 0, 2)
        wd = ffn_w_down[l].astype(BF16).reshape(n_chunks, FF_CHUNK, d)
        wo = w_o[l].astype(BF16).reshape(N_HEADS, GROUP_WIDTH, d)
        flat = lambda t: t.reshape(b * s, GROUP_WIDTH)
        x2d = _post(x2d, flat(a_out), flat(b_out), flat(c_out), flat(d_out), wo, row(post_mix_norm[l]),
                    row(pre_ffn_norm[l]), wg, wu, wd, row(post_ffn_norm[l]))
    return x2d.reshape(b, s, d)
```

```python
import functools
import math

import numpy as np
import jax
import jax.numpy as jnp
from jax import lax
from jax.experimental import pallas as pl
from jax.experimental.pallas import tpu as pltpu

F32 = jnp.float32
BF16 = jnp.bfloat16

D_MODEL = 1024
SEQ = 2048
GRID_W = 64
GRID_ROWS = SEQ // GRID_W
HEAD_DIM = 64
N_HEADS = 4
GROUP_WIDTH = 256
EPS = 1e-6
ROPE_THETA = 10000.0

NA_WIN_ROWS = 8
NA_WIN_COLS = 16
DIFF_QK_DIM = 32
POS_SPLIT = 64
MLA_NOPE_DIM = 64
MLA_ROPE_DIM = 32
MLA_KV_LORA = 128
D_FF = 2816

O_GQ_Q, O_GQ_K, O_GQ_V, O_ML_CQ, O_ML_CKV, O_ML_KR = 1536, 1792, 1920, 2048, 2304, 2432

PROJ_WIDTH = 2560
B256_NA_Q, B256_NA_K, B256_NA_V, B256_DF_Q, B256_DF_K, B256_DF_V, B256_GQ_Q, B256_ML_CQ = range(8)
B128_GQ_K, B128_GQ_V, B128_ML_CKV, B128_ML_KR = range(16, 20)

VMEM_LIMIT_BYTES = 56 * 1024 * 1024

TOKEN_TILE = 1024
FF_CHUNK = 256
Q_TILE = 256
NA_GROUP_ROWS = 4
NA_BAND_ROWS = 12
SCORE_SLOT_PAD_ROWS = 8
NEG_BIG = -1e30
LOG2E = math.log2(math.e)


def _rot_perm(width, group):
    j = np.arange(width)
    jj = j % group
    half = group // 2
    src = (j // group) * group + (jj + half) % group
    sign = np.where(jj < half, -1.0, 1.0).astype(np.float32)
    return src, sign


def _compiler_params():
    return pltpu.CompilerParams(dimension_semantics=("arbitrary",), vmem_limit_bytes=VMEM_LIMIT_BYTES)


def _const_spec(shape):
    zeros = (0,) * len(shape)
    return pl.BlockSpec(shape, lambda i: zeros, pipeline_mode=pl.Buffered(1))


def _rms(x):
    return lax.rsqrt(jnp.mean(x * x, axis=-1, keepdims=True) + EPS)


def _dot(a, b):
    return jnp.dot(a, b, preferred_element_type=F32)


def _inproj_kernel(x_ref, g_ref, w_ref, o_ref):
    x = x_ref[...]
    h = (x * _rms(x) * g_ref[...]).astype(BF16)
    o_ref[...] = _dot(h, w_ref[...])


def _inproj(x2d, gain, w):
    t = x2d.shape[0]
    return pl.pallas_call(
        _inproj_kernel,
        out_shape=jax.ShapeDtypeStruct((t, PROJ_WIDTH), F32),
        grid=(t // TOKEN_TILE,),
        in_specs=[
            pl.BlockSpec((TOKEN_TILE, D_MODEL), lambda i: (i, 0)),
            _const_spec((1, D_MODEL)),
            _const_spec((D_MODEL, PROJ_WIDTH)),
        ],
        out_specs=pl.BlockSpec((TOKEN_TILE, PROJ_WIDTH), lambda i: (i, 0)),
        compiler_params=_compiler_params(),
        name="inproj",
    )(x2d, gain, w)


def _segment_mean_square(x, seg):
    w = x.shape[-1]
    same = (lax.broadcasted_iota(jnp.int32, (w, w), 0) // seg) == (lax.broadcasted_iota(jnp.int32, (w, w), 1) // seg)
    ones = jnp.where(same, 1.0, 0.0).astype(BF16)
    sq = x * x
    hi = sq.astype(BF16)
    lo = (sq - hi.astype(F32)).astype(BF16)
    return (_dot(hi, ones) + _dot(lo, ones)) * (1.0 / seg)


def _ones_row_block(width):
    return jnp.where(lax.broadcasted_iota(jnp.int32, (16, width), 0) == 0, 1.0, 0.0).astype(BF16)


def _normalised_heads(accs):
    return jnp.concatenate([acc[:HEAD_DIM] / acc[HEAD_DIM:HEAD_DIM + 1] for acc in accs], axis=0)


def _pipelined_softmax_pv(n_tiles, n_heads, n_chunks, score_chunks, value_chunk, combine, s_scr, acc_scr, o_ref,
                          exp_fn=jnp.exp2):
    assert n_heads % 2 == 0
    rows = (s_scr.shape[1] - SCORE_SLOT_PAD_ROWS) // n_chunks

    def scores_into(slot, qi, h):
        m8 = None
        for c, st in enumerate(score_chunks(qi, h)):
            s_scr[slot, c * rows:(c + 1) * rows, :] = st
            cm = jnp.max(st.reshape(-1, 8, st.shape[-1]), axis=0)
            m8 = cm if m8 is None else jnp.maximum(m8, cm)
        return jnp.max(m8, axis=0, keepdims=True)

    def write_tile(qi):
        r0 = qi * Q_TILE if isinstance(qi, int) else pl.multiple_of(qi * Q_TILE, Q_TILE)
        o_ref[0, pl.ds(r0, Q_TILE), :] = combine([acc_scr[h] for h in range(n_heads)]).T.astype(o_ref.dtype)

    acc_scr[...] = jnp.ones_like(acc_scr)

    def body(qi, m):
        next_qi = jnp.minimum(qi + 1, n_tiles - 1)
        for h in range(n_heads):
            slot = h % 2
            nq, nh = (qi, h + 1) if h + 1 < n_heads else (next_qi, 0)
            next_m = scores_into(1 - slot, nq, nh)
            if h == 0:
                write_tile(jnp.maximum(qi - 1, 0))
            acc = None
            for c in range(n_chunks):
                p = exp_fn(s_scr[slot, c * rows:(c + 1) * rows, :] - m).astype(BF16)
                part = _dot(value_chunk(qi, h, c), p)
                acc = part if acc is None else acc + part
            acc_scr[h] = acc
            m = next_m
        return m

    lax.fori_loop(0, n_tiles, body, scores_into(0, 0, 0), unroll=2)
    write_tile(n_tiles - 1)


def _score_scratch(keys):
    return pltpu.VMEM((2, keys + SCORE_SLOT_PAD_ROWS, Q_TILE), F32)


def _proj_spec(width, block):
    return pl.BlockSpec((1, SEQ, width), lambda b: (b, 0, block))


def _na_kernel(q_ref, k_ref, v_ref, tab_ref, o_ref, qt_scr, k_scr, vt_scr, s_scr, acc_scr):
    assert NA_GROUP_ROWS * GRID_W == Q_TILE and NA_BAND_ROWS % NA_GROUP_ROWS == 0
    n_tiles = SEQ // Q_TILE
    band_chunks = NA_BAND_ROWS // NA_GROUP_ROWS
    pair = 2 * HEAD_DIM
    q_scale = HEAD_DIM ** -0.5 * LOG2E

    def prep(ci, carry):
        r0 = pl.multiple_of(ci * Q_TILE, Q_TILE)
        rows = pl.ds(r0, Q_TILE)
        qt = (q_ref[0, rows, :] * q_scale).T
        row = lax.broadcasted_iota(jnp.int32, (pair, Q_TILE), 0)
        for h in range(N_HEADS):
            own = (row >= (h % 2) * HEAD_DIM) & (row < (h % 2 + 1) * HEAD_DIM)
            qt_scr[ci, h] = jnp.where(own, qt[(h // 2) * pair:(h // 2 + 1) * pair], 0.0).astype(BF16)
        for p in range(N_HEADS // 2):
            k_scr[p, rows, :] = k_ref[0, rows, p * pair:(p + 1) * pair].astype(BF16)
        vt = v_ref[0, rows, :].T.astype(BF16)
        for h in range(N_HEADS):
            vt_scr[h, ci, 0:HEAD_DIM, :] = vt[h * HEAD_DIM:(h + 1) * HEAD_DIM]
            vt_scr[h, ci, HEAD_DIM:, :] = _ones_row_block(Q_TILE)
        return carry

    lax.fori_loop(0, n_tiles, prep, 0, unroll=4)

    def key_chunk(gi, c):
        return jnp.clip(gi - (NA_WIN_ROWS // 2) // NA_GROUP_ROWS, 0, n_tiles - band_chunks) + c

    def score_chunks(gi, h):
        kind = jnp.where(gi == 0, 0, jnp.where(gi == n_tiles - 1, 2, 1))
        k0 = pl.multiple_of(key_chunk(gi, 0) * Q_TILE, Q_TILE)
        return [_dot(k_scr[h // 2, pl.ds(k0 + c * Q_TILE, Q_TILE), :], qt_scr[gi, h])
                + tab_ref[kind, h, c * Q_TILE:(c + 1) * Q_TILE, :] for c in range(band_chunks)]

    def value_chunk(gi, h, c):
        return vt_scr[h, key_chunk(gi, c)]

    _pipelined_softmax_pv(n_tiles, N_HEADS, band_chunks, score_chunks, value_chunk, _normalised_heads, s_scr, acc_scr,
                          o_ref)


def _na_tables(rel_bias):
    a = np.arange(NA_GROUP_ROWS)[:, None, None, None]
    c = np.arange(GRID_W)[None, :, None, None]
    i = np.arange(NA_BAND_ROWS)[None, None, :, None]
    kc = np.arange(GRID_W)[None, None, None, :]
    cs = np.clip(c - NA_WIN_COLS // 2, 0, GRID_W - NA_WIN_COLS)
    col_ok = (kc >= cs) & (kc < cs + NA_WIN_COLS)
    dc = kc - c + NA_WIN_COLS - 1
    col_sel = (dc[..., None] == np.arange(2 * NA_WIN_COLS - 1)) & col_ok[..., None]
    row_sels = []
    last_r0 = GRID_ROWS - NA_GROUP_ROWS
    for r0, band_row in ((0, 0), (NA_GROUP_ROWS, 0), (last_r0, GRID_ROWS - NA_BAND_ROWS)):
        r = r0 + a
        rs = np.clip(r - NA_WIN_ROWS // 2, 0, GRID_ROWS - NA_WIN_ROWS)
        key_row = band_row + i
        row_ok = (key_row >= rs) & (key_row < rs + NA_WIN_ROWS)
        dr = key_row - r + NA_WIN_ROWS - 1
        row_sels.append((dr[..., None] == np.arange(2 * NA_WIN_ROWS - 1)) & row_ok[..., None])
    row_sel = np.stack(row_sels)[:, :, 0, :, 0, :].astype(np.float32)
    col_sel = col_sel[0, :, 0, :, :].astype(np.float32)
    vals = jnp.einsum("taiu,huv,ckv->thikac", row_sel, rel_bias.astype(F32), col_sel,
                      precision=lax.Precision.HIGHEST)
    inside = np.einsum("taiu,ckv->tikac", row_sel, col_sel) > 0
    tab = jnp.where(inside[:, None], vals * LOG2E, NEG_BIG)
    q_rows, band = NA_GROUP_ROWS * GRID_W, NA_BAND_ROWS * GRID_W
    return tab.reshape(3, N_HEADS, band, q_rows)


def _na_attention(proj, tables):
    b = proj.shape[0]
    return pl.pallas_call(
        _na_kernel,
        out_shape=jax.ShapeDtypeStruct((b, SEQ, GROUP_WIDTH), BF16),
        grid=(b,),
        in_specs=[
            _proj_spec(256, B256_NA_Q), _proj_spec(256, B256_NA_K), _proj_spec(256, B256_NA_V),
            _const_spec(tables.shape),
        ],
        out_specs=pl.BlockSpec((1, SEQ, GROUP_WIDTH), lambda i: (i, 0, 0)),
        scratch_shapes=[pltpu.VMEM((SEQ // Q_TILE, N_HEADS, 2 * HEAD_DIM, Q_TILE), BF16),
                        pltpu.VMEM((N_HEADS // 2, SEQ, 2 * HEAD_DIM), BF16),
                        pltpu.VMEM((N_HEADS, SEQ // Q_TILE, HEAD_DIM + 16, Q_TILE), BF16),
                        _score_scratch(NA_BAND_ROWS * GRID_W),
                        pltpu.VMEM((N_HEADS, HEAD_DIM + 16, Q_TILE), F32)],
        compiler_params=_compiler_params(),
        name="na_attention",
    )(proj, proj, proj, tables)


def _alibi_slope(h):
    return 2.0 ** (-8.0 * (h + 1) / N_HEADS)


def _diff_kernel(lambda_init, q_ref, k_ref, v_ref, lq1_ref, lk1_ref, lq2_ref, lk2_ref, subln_ref, diag_ref, o_ref,
                 qt_scr, qf_scr, k_scr, vt_scr, s_scr, acc_scr):
    scale = DIFF_QK_DIM ** -0.5
    n_tiles = SEQ // Q_TILE
    pair = 2 * HEAD_DIM
    n_feat = 16
    lam = (jnp.exp(jnp.sum(lq1_ref[...] * lk1_ref[...], axis=-1, keepdims=True))
           - jnp.exp(jnp.sum(lq2_ref[...] * lk2_ref[...], axis=-1, keepdims=True)) + lambda_init)

    def prep(ci, carry):
        r0 = pl.multiple_of(ci * Q_TILE, Q_TILE)
        rows = pl.ds(r0, Q_TILE)
        qt = (q_ref[0, rows, :] * scale).T
        row = lax.broadcasted_iota(jnp.int32, (pair, Q_TILE), 0)
        feat_row = lax.broadcasted_iota(jnp.int32, (n_feat, Q_TILE), 0)
        i = r0 + lax.broadcasted_iota(jnp.int32, (n_feat, Q_TILE), 1)
        i_lo = i % POS_SPLIT
        i_hi = i - i_lo
        for h in range(N_HEADS):
            group = qt[(h // 2) * pair:(h // 2 + 1) * pair]
            for mp in range(2):
                lo = (h % 2) * HEAD_DIM + mp * DIFF_QK_DIM
                keep = (row >= lo) & (row < lo + DIFF_QK_DIM)
                qt_scr[ci, 2 * h + mp] = jnp.where(keep, group, 0.0).astype(BF16)
            slope = _alibi_slope(h)
            feat = jnp.where(feat_row < 2, slope,
                             jnp.where(feat_row == 2, -slope * i_hi.astype(F32),
                                       jnp.where(feat_row == 3, -slope * i_lo.astype(F32), 0.0)))
            qf_scr[ci, h, 0:n_feat, :] = feat.astype(BF16)
            qf_scr[ci, h, n_feat:, :] = (-feat).astype(BF16)
        lane = lax.broadcasted_iota(jnp.int32, (Q_TILE, pair), 1)
        j = r0 + lax.broadcasted_iota(jnp.int32, (Q_TILE, pair), 0)
        j_lo = j % POS_SPLIT
        k_feat = jnp.where(lane == 0, (j - j_lo).astype(F32),
                           jnp.where(lane == 1, j_lo.astype(F32), jnp.where(lane < 4, 1.0, 0.0))).astype(BF16)
        for p in range(N_HEADS // 2):
            k_scr[p, rows, 0:pair] = k_ref[0, rows, p * pair:(p + 1) * pair].astype(BF16)
            k_scr[p, rows, pair:] = k_feat
        vt = v_ref[0, rows, :].T.astype(BF16)
        for h in range(N_HEADS):
            vt_scr[h, ci, 0:HEAD_DIM, :] = vt[h * HEAD_DIM:(h + 1) * HEAD_DIM]
            vt_scr[h, ci, HEAD_DIM:, :] = _ones_row_block(Q_TILE)
        return carry

    lax.fori_loop(0, n_tiles, prep, 0, unroll=4)

    def key_chunk(qi, c):
        wrapped = qi + c >= n_tiles
        return jnp.where(wrapped, qi + c - n_tiles, qi + c), wrapped

    def score_chunk(qi, item, c):
        h = item // 2
        kc, left = key_chunk(qi, c)
        form = 0 if c == 0 else pl.multiple_of(jnp.where(left, 0, n_feat), n_feat)
        feat = qf_scr[qi, h, pl.ds(form, n_feat), :]
        rhs = jnp.concatenate([qt_scr[qi, item], feat, jnp.zeros((pair - n_feat, Q_TILE), BF16)], axis=0)
        st = _dot(k_scr[h // 2, pl.ds(pl.multiple_of(kc * Q_TILE, Q_TILE), Q_TILE), :], rhs)
        return st + diag_ref[h] if c == 0 else st

    def score_chunks(qi, item):
        return [score_chunk(qi, item, c) for c in range(n_tiles)]

    def value_chunk(qi, item, c):
        return vt_scr[item // 2, key_chunk(qi, c)[0]]

    def combine(accs):
        outs = []
        for h in range(N_HEADS):
            a1, a2 = accs[2 * h], accs[2 * h + 1]
            o = a1[:HEAD_DIM] / a1[HEAD_DIM:HEAD_DIM + 1] - lam * (a2[:HEAD_DIM] / a2[HEAD_DIM:HEAD_DIM + 1])
            r = lax.rsqrt(jnp.mean(o * o, axis=0, keepdims=True) + EPS)
            outs.append(o * r * subln_ref[...] * (1.0 - lambda_init))
        return jnp.concatenate(outs, axis=0)

    _pipelined_softmax_pv(n_tiles, 2 * N_HEADS, n_tiles, score_chunks, value_chunk, combine, s_scr, acc_scr, o_ref,
                          exp_fn=jnp.exp)


def _diff_attention(proj, lq1, lk1, lq2, lk2, subln, lambda_init):
    b = proj.shape[0]
    vec = _const_spec((1, DIFF_QK_DIM))
    n_tiles = SEQ // Q_TILE
    rel = np.arange(Q_TILE)
    over = -2.0 * np.maximum(rel[:, None] - rel[None, :], 0).astype(np.float32)
    diag = jnp.asarray(np.stack([_alibi_slope(h) * over for h in range(N_HEADS)]))
    subln_cols = jnp.broadcast_to(subln.reshape(HEAD_DIM, 1), (HEAD_DIM, Q_TILE))
    return pl.pallas_call(
        functools.partial(_diff_kernel, lambda_init),
        out_shape=jax.ShapeDtypeStruct((b, SEQ, GROUP_WIDTH), BF16),
        grid=(b,),
        in_specs=[
            _proj_spec(256, B256_DF_Q), _proj_spec(256, B256_DF_K), _proj_spec(256, B256_DF_V),
            vec, vec, vec, vec, _const_spec((HEAD_DIM, Q_TILE)), _const_spec(diag.shape),
        ],
        out_specs=pl.BlockSpec((1, SEQ, GROUP_WIDTH), lambda i: (i, 0, 0)),
        scratch_shapes=[pltpu.VMEM((n_tiles, 2 * N_HEADS, 2 * HEAD_DIM, Q_TILE), BF16),
                        pltpu.VMEM((n_tiles, N_HEADS, 32, Q_TILE), BF16),
                        pltpu.VMEM((N_HEADS // 2, SEQ, 4 * HEAD_DIM), BF16),
                        pltpu.VMEM((N_HEADS, n_tiles, HEAD_DIM + 16, Q_TILE), BF16),
                        _score_scratch(SEQ),
                        pltpu.VMEM((2 * N_HEADS, HEAD_DIM + 16, Q_TILE), F32)],
        compiler_params=_compiler_params(),
        name="diff_attention",
    )(proj, proj, proj, lq1, lk1, lq2, lk2, subln_cols, diag)


def _swap_rotary_halves(x):
    quarter = HEAD_DIM // 4
    first = (lax.broadcasted_iota(jnp.int32, x.shape, 1) % (2 * quarter)) < quarter
    lanes = x.shape[1]
    return jnp.where(first, pltpu.roll(x, lanes - quarter, 1), pltpu.roll(x, quarter, 1))


def _gqa_kernel(q_ref, k_ref, v_ref, qc_ref, qs_ref, kc_ref, ks_ref, o_ref,
                qt_scr, k_scr, vt_scr, s_scr, acc_scr):
    kv_heads = N_HEADS // 2
    n_tiles = SEQ // Q_TILE
    pair = 2 * HEAD_DIM

    def normed_rotary(x, gain_cos, gain_sin):
        rotated = x * gain_cos + _swap_rotary_halves(x) * gain_sin
        return rotated * lax.rsqrt(_segment_mean_square(x, HEAD_DIM) + EPS)

    def prep(ci, carry):
        r0 = pl.multiple_of(ci * Q_TILE, Q_TILE)
        rows = pl.ds(r0, Q_TILE)
        zeros = jnp.zeros((HEAD_DIM, Q_TILE), BF16)
        for half in range(2):
            lanes = slice(half * pair, (half + 1) * pair)
            y = normed_rotary(q_ref[0, rows, lanes], qc_ref[rows, :], qs_ref[rows, :])
            yt = y.T.astype(BF16)
            for j in range(2):
                h = 2 * half + j
                kv = h // (N_HEADS // kv_heads)
                for part in range(kv_heads):
                    block = yt[j * HEAD_DIM:(j + 1) * HEAD_DIM] if part == kv else zeros
                    qt_scr[ci, h * pair + part * HEAD_DIM:h * pair + (part + 1) * HEAD_DIM, :] = block
        k_scr[rows, :] = normed_rotary(k_ref[0, rows, :], kc_ref[rows, :], ks_ref[rows, :]).astype(BF16)
        vt = v_ref[0, rows, :].T.astype(BF16)
        for kv in range(kv_heads):
            vt_scr[kv, ci, 0:HEAD_DIM, :] = vt[kv * HEAD_DIM:(kv + 1) * HEAD_DIM]
            vt_scr[kv, ci, HEAD_DIM:, :] = _ones_row_block(Q_TILE)
        return carry

    lax.fori_loop(0, n_tiles, prep, 0, unroll=4)

    def score_chunks(qi, h):
        qt = qt_scr[qi, h * pair:(h + 1) * pair, :]
        return [_dot(k_scr[c * Q_TILE:(c + 1) * Q_TILE, :], qt) for c in range(n_tiles)]

    def value_chunk(qi, h, c):
        return vt_scr[h // (N_HEADS // kv_heads), c]

    _pipelined_softmax_pv(n_tiles, N_HEADS, n_tiles, score_chunks, value_chunk, _normalised_heads, s_scr, acc_scr,
                          o_ref)


def _gqa_attention(proj, gq, gqp, gk, gkp, cos, sin):
    b = proj.shape[0]
    tab = _const_spec((SEQ, 2 * HEAD_DIM))
    q_scale = HEAD_DIM ** -0.5 * LOG2E
    _, sign = _rot_perm(2 * HEAD_DIM, HEAD_DIM // 2)
    tables = (gq * cos * q_scale, gqp * (sign * q_scale) * sin, gk * cos, gkp * sign * sin)
    return pl.pallas_call(
        _gqa_kernel,
        out_shape=jax.ShapeDtypeStruct((b, SEQ, GROUP_WIDTH), BF16),
        grid=(b,),
        in_specs=[
            _proj_spec(256, B256_GQ_Q), _proj_spec(128, B128_GQ_K), _proj_spec(128, B128_GQ_V),
            tab, tab, tab, tab,
        ],
        out_specs=pl.BlockSpec((1, SEQ, GROUP_WIDTH), lambda i: (i, 0, 0)),
        scratch_shapes=[pltpu.VMEM((SEQ // Q_TILE, N_HEADS * 2 * HEAD_DIM, Q_TILE), BF16),
                        pltpu.VMEM((SEQ, 2 * HEAD_DIM), BF16),
                        pltpu.VMEM((N_HEADS // 2, SEQ // Q_TILE, HEAD_DIM + 16, Q_TILE), BF16),
                        _score_scratch(SEQ),
                        pltpu.VMEM((N_HEADS, HEAD_DIM + 16, Q_TILE), F32)],
        compiler_params=_compiler_params(),
        name="gqa_attention",
    )(proj, proj, proj, *tables)


def _mla_kernel(cq_ref, ckv_ref, kr_ref, gq_ref, gkv_ref, wuq_ref, wukv_ref, cos_ref, sin_ref, o_ref,
                qt_scr, k_scr, vt_scr, s_scr, acc_scr):
    q_scale = (MLA_NOPE_DIM + MLA_ROPE_DIM) ** -0.5 * LOG2E
    n_tiles = SEQ // Q_TILE
    rot0 = N_HEADS * 128
    v0 = N_HEADS * 128

    def prep(ci, carry):
        r0 = pl.multiple_of(ci * Q_TILE, Q_TILE)
        rows = pl.ds(r0, Q_TILE)
        cos, sin = cos_ref[rows, :], sin_ref[rows, :]
        cq = cq_ref[0, rows, :]
        cqn = (cq * _rms(cq) * gq_ref[...]).astype(BF16)
        ckv = ckv_ref[0, rows, :]
        ckvn = (ckv * _rms(ckv) * gkv_ref[...]).astype(BF16)
        kr = kr_ref[0, rows, :]
        k_rope = kr * cos + pltpu.roll(kr, 128 - MLA_ROPE_DIM, 1) * sin
        two = lambda t: jnp.concatenate([t, t], axis=1)
        q_cos, q_sin, k_rope2 = two(cos * q_scale), two(sin * q_scale), two(k_rope)
        for p in range(N_HEADS // 2):
            cols = slice(p * 256, (p + 1) * 256)
            rot_cols = slice(rot0 + p * 256, rot0 + (p + 1) * 256)
            q2 = _dot(cqn, wuq_ref[:, cols]) * q_cos + _dot(cqn, wuq_ref[:, rot_cols]) * q_sin
            qt_scr[ci, p * 256:(p + 1) * 256, :] = q2.T.astype(BF16)
            k2 = (_dot(ckvn, wukv_ref[:, cols]) + k_rope2).astype(BF16)
            k_scr[2 * p, rows, :] = k2[:, :128]
            k_scr[2 * p + 1, rows, :] = k2[:, 128:]
        vt = _dot(ckvn, wukv_ref[:, v0:v0 + GROUP_WIDTH]).T.astype(BF16)
        for h in range(N_HEADS):
            vt_scr[h, ci, 0:HEAD_DIM, :] = vt[h * HEAD_DIM:(h + 1) * HEAD_DIM]
            vt_scr[h, ci, HEAD_DIM:, :] = _ones_row_block(Q_TILE)
        return carry

    lax.fori_loop(0, n_tiles, prep, 0, unroll=4)

    def score_chunks(qi, h):
        qt = qt_scr[qi, h * 128:(h + 1) * 128, :]
        return [_dot(k_scr[h, c * Q_TILE:(c + 1) * Q_TILE, :], qt) for c in range(n_tiles)]

    def value_chunk(qi, h, c):
        return vt_scr[h, c]

    _pipelined_softmax_pv(n_tiles, N_HEADS, n_tiles, score_chunks, value_chunk, _normalised_heads, s_scr, acc_scr,
                          o_ref)


def _mla_attention(proj, gq, gkv, wuq, wukv, cos, sin):
    b = proj.shape[0]
    tab = _const_spec((SEQ, 128))
    return pl.pallas_call(
        _mla_kernel,
        out_shape=jax.ShapeDtypeStruct((b, SEQ, GROUP_WIDTH), BF16),
        grid=(b,),
        in_specs=[
            _proj_spec(256, B256_ML_CQ), _proj_spec(128, B128_ML_CKV), _proj_spec(128, B128_ML_KR),
            _const_spec((1, GROUP_WIDTH)), _const_spec((1, MLA_KV_LORA)),
            _const_spec(wuq.shape), _const_spec(wukv.shape), tab, tab,
        ],
        out_specs=pl.BlockSpec((1, SEQ, GROUP_WIDTH), lambda i: (i, 0, 0)),
        scratch_shapes=[pltpu.VMEM((SEQ // Q_TILE, N_HEADS * 128, Q_TILE), BF16),
                        pltpu.VMEM((N_HEADS, SEQ, 128), BF16),
                        pltpu.VMEM((N_HEADS, SEQ // Q_TILE, HEAD_DIM + 16, Q_TILE), BF16),
                        _score_scratch(SEQ),
                        pltpu.VMEM((N_HEADS, HEAD_DIM + 16, Q_TILE), F32)],
        compiler_params=_compiler_params(),
        name="mla_attention",
    )(proj, proj, proj, gq, gkv, wuq, wukv, cos, sin)


def _post_kernel(x_ref, a_ref, b_ref, c_ref, d_ref, wo_ref, g_mix_ref, g_pre_ref, wg_ref, wu_ref, wd_ref,
                 g_ffn_ref, o_ref, acc_ref):
    mix = (_dot(a_ref[...], wo_ref[0]) + _dot(b_ref[...], wo_ref[1])
           + _dot(c_ref[...], wo_ref[2]) + _dot(d_ref[...], wo_ref[3]))
    x = x_ref[...] + mix * _rms(mix) * g_mix_ref[...]
    h = (x * _rms(x) * g_pre_ref[...]).astype(BF16)
    acc_ref[...] = jnp.zeros_like(acc_ref)

    def body(ci, carry):
        gate = _dot(h, wg_ref[ci])
        up = _dot(h, wu_ref[ci])
        act = (gate * jax.nn.sigmoid(gate) * up).astype(BF16)
        acc_ref[...] += _dot(act, wd_ref[ci])
        return carry

    lax.fori_loop(0, D_FF // FF_CHUNK, body, 0, unroll=True)
    f = acc_ref[...]
    o_ref[...] = x + f * _rms(f) * g_ffn_ref[...]


def _post(x2d, a, b, c, d, wo, g_mix, g_pre, wg, wu, wd, g_ffn):
    t = x2d.shape[0]
    tok = lambda w: pl.BlockSpec((TOKEN_TILE, w), lambda i: (i, 0))
    vec = _const_spec((1, D_MODEL))
    return pl.pallas_call(
        _post_kernel,
        out_shape=jax.ShapeDtypeStruct((t, D_MODEL), F32),
        grid=(t // TOKEN_TILE,),
        in_specs=[
            tok(D_MODEL), tok(GROUP_WIDTH), tok(GROUP_WIDTH), tok(GROUP_WIDTH), tok(GROUP_WIDTH),
            _const_spec(wo.shape), vec, vec, _const_spec(wg.shape), _const_spec(wu.shape), _const_spec(wd.shape),
            vec,
        ],
        out_specs=tok(D_MODEL),
        scratch_shapes=[pltpu.VMEM((TOKEN_TILE, D_MODEL), F32)],
        compiler_params=_compiler_params(),
        name="outproj_swiglu",
    )(x2d, a, b, c, d, wo, g_mix, g_pre, wg, wu, wd, g_ffn)


def _layout_w_in(w):
    src32, sign32 = _rot_perm(MLA_ROPE_DIM, MLA_ROPE_DIM)
    k_rope = w[:, O_ML_KR:O_ML_KR + MLA_ROPE_DIM]
    cols = [
        w[:, :O_GQ_K],
        w[:, O_ML_CQ:O_ML_CKV],
        w[:, O_GQ_K:O_ML_CQ],
        w[:, O_ML_CKV:O_ML_KR],
        jnp.zeros((D_MODEL, MLA_NOPE_DIM), w.dtype), k_rope, k_rope[:, src32] * sign32,
    ]
    return jnp.concatenate(cols, axis=1).astype(BF16)


def _layout_mla_weights(w_uq, w_ukv):
    src32, sign32 = _rot_perm(MLA_ROPE_DIM, MLA_ROPE_DIM)
    wq = w_uq.reshape(GROUP_WIDTH, N_HEADS, MLA_NOPE_DIM + MLA_ROPE_DIM)
    rope = wq[:, :, MLA_NOPE_DIM:]
    tail = 128 - MLA_NOPE_DIM - MLA_ROPE_DIM
    q_main = jnp.pad(wq, ((0, 0), (0, 0), (0, tail)))
    q_rot = jnp.pad(rope[:, :, src32] * sign32, ((0, 0), (0, 0), (MLA_NOPE_DIM, tail)))
    wuq = jnp.concatenate([q_main.reshape(GROUP_WIDTH, -1), q_rot.reshape(GROUP_WIDTH, -1)], axis=1)
    wkv = w_ukv.reshape(MLA_KV_LORA, N_HEADS, MLA_NOPE_DIM + HEAD_DIM)
    k_nope = jnp.pad(wkv[:, :, :MLA_NOPE_DIM], ((0, 0), (0, 0), (0, 128 - MLA_NOPE_DIM)))
    vals = wkv[:, :, MLA_NOPE_DIM:]
    wukv = jnp.concatenate([k_nope.reshape(MLA_KV_LORA, -1), vals.reshape(MLA_KV_LORA, -1)], axis=1)
    return wuq.astype(BF16), wukv.astype(BF16)


def _rotary_tables():
    pos = jnp.arange(SEQ)
    half = HEAD_DIM // 2
    inv = ROPE_THETA ** (-jnp.arange(0, half, 2, dtype=F32) / half)

    def angles(p):
        ang = p.astype(F32)[:, None] * inv[None, :]
        return jnp.concatenate([ang, ang], axis=-1)

    axial = jnp.concatenate([angles(pos // GRID_W), angles(pos % GRID_W)], axis=-1)
    seq = angles(pos)
    tail = 128 - MLA_NOPE_DIM - MLA_ROPE_DIM
    mla_cos = jnp.concatenate([jnp.ones((SEQ, MLA_NOPE_DIM), F32), jnp.cos(seq), jnp.zeros((SEQ, tail), F32)], -1)
    mla_sin = jnp.pad(jnp.sin(seq), ((0, 0), (MLA_NOPE_DIM, tail)))
    pair = lambda t: jnp.tile(t, (1, 2))
    return pair(jnp.cos(axial)), pair(jnp.sin(axial)), mla_cos, mla_sin


def kernel(x, pre_mix_norm, w_in, na_rel_bias, diff_lambda_q1, diff_lambda_k1, diff_lambda_q2, diff_lambda_k2,
           diff_subln, gqa_q_norm, gqa_k_norm, mla_q_norm, mla_kv_norm, mla_w_uq, mla_w_ukv, w_o, post_mix_norm,
           pre_ffn_norm, ffn_w_gate_up, ffn_w_down, post_ffn_norm):
    b, s, d = x.shape
    assert (s, d) == (SEQ, D_MODEL)
    depth = w_in.shape[0]
    src64, _ = _rot_perm(HEAD_DIM, HEAD_DIM // 2)
    ax_cos, ax_sin, mla_cos, mla_sin = _rotary_tables()
    row = lambda v: v.reshape(1, -1).astype(F32)
    n_chunks = D_FF // FF_CHUNK

    x2d = x.reshape(b * s, d)
    for l in range(depth):
        lambda_init = 0.8 - 0.6 * math.exp(-0.3 * l)
        proj = _inproj(x2d, row(pre_mix_norm[l]), _layout_w_in(w_in[l])).reshape(b, s, PROJ_WIDTH)

        a_out = _na_attention(proj, _na_tables(na_rel_bias[l]))
        b_out = _diff_attention(proj, row(diff_lambda_q1[l]), row(diff_lambda_k1[l]), row(diff_lambda_q2[l]),
                                row(diff_lambda_k2[l]), row(diff_subln[l]), lambda_init)
        pair_row = lambda v: row(jnp.tile(v, 2))
        c_out = _gqa_attention(proj, pair_row(gqa_q_norm[l]), pair_row(gqa_q_norm[l][src64]),
                               pair_row(gqa_k_norm[l]), pair_row(gqa_k_norm[l][src64]), ax_cos, ax_sin)
        wuq, wukv = _layout_mla_weights(mla_w_uq[l], mla_w_ukv[l])
        d_out = _mla_attention(proj, row(mla_q_norm[l]), row(mla_kv_norm[l]), wuq, wukv, mla_cos, mla_sin)

        gate_up = ffn_w_gate_up[l].astype(BF16)
        wg = gate_up[:, :D_FF].reshape(d, n_chunks, FF_CHUNK).transpose(1, 0, 2)
        wu = gate_up[:, D_FF:].reshape(d, n_chunks, FF_CHUNK).transpose(1, 0, 2)
        wd = ffn_w_down[l].astype(BF16).reshape(n_chunks, FF_CHUNK, d)
        wo = w_o[l].astype(BF16).reshape(N_HEADS, GROUP_WIDTH, d)
        flat = lambda t: t.reshape(b * s, GROUP_WIDTH)
        x2d = _post(x2d, flat(a_out), flat(b_out), flat(c_out), flat(d_out), wo, row(post_mix_norm[l]),
                    row(pre_ffn_norm[l]), wg, wu, wd, row(post_ffn_norm[l]))
    return x2d.reshape(b, s, d)
```

```python
import functools
import math

import numpy as np
import jax
import jax.numpy as jnp
from jax import lax
from jax.experimental import pallas as pl
from jax.experimental.pallas import tpu as pltpu

F32 = jnp.float32
BF16 = jnp.bfloat16

D_MODEL = 1024
SEQ = 2048
GRID_W = 64
GRID_ROWS = SEQ // GRID_W
HEAD_DIM = 64
N_HEADS = 4
GROUP_WIDTH = 256
EPS = 1e-6
ROPE_THETA = 10000.0

NA_WIN_ROWS = 8
NA_WIN_COLS = 16
DIFF_QK_DIM = 32
POS_SPLIT = 64
MLA_NOPE_DIM = 64
MLA_ROPE_DIM = 32
MLA_KV_LORA = 128
D_FF = 2816

O_GQ_Q, O_GQ_K, O_GQ_V, O_ML_CQ, O_ML_CKV, O_ML_KR = 1536, 1792, 1920, 2048, 2304, 2432

PROJ_WIDTH = 2560
B256_NA_Q, B256_NA_K, B256_NA_V, B256_DF_Q, B256_DF_K, B256_DF_V, B256_GQ_Q, B256_ML_CQ = range(8)
B128_GQ_K, B128_GQ_V, B128_ML_CKV, B128_ML_KR = range(16, 20)

VMEM_LIMIT_BYTES = 56 * 1024 * 1024

TOKEN_TILE = 1024
FF_CHUNK = 256
Q_TILE = 256
NA_GROUP_ROWS = 4
NA_BAND_ROWS = 12
NEG_BIG = -1e30
LOG2E = math.log2(math.e)


def _rot_perm(width, group):
    j = np.arange(width)
    jj = j % group
    half = group // 2
    src = (j // group) * group + (jj + half) % group
    sign = np.where(jj < half, -1.0, 1.0).astype(np.float32)
    return src, sign


def _compiler_params():
    return pltpu.CompilerParams(dimension_semantics=("arbitrary",), vmem_limit_bytes=VMEM_LIMIT_BYTES)


def _const_spec(shape):
    zeros = (0,) * len(shape)
    return pl.BlockSpec(shape, lambda i: zeros, pipeline_mode=pl.Buffered(1))


def _rms(x):
    return lax.rsqrt(jnp.mean(x * x, axis=-1, keepdims=True) + EPS)


def _dot(a, b):
    return jnp.dot(a, b, preferred_element_type=F32)


def _inproj_kernel(x_ref, g_ref, w_ref, o_ref):
    x = x_ref[...]
    h = (x * _rms(x) * g_ref[...]).astype(BF16)
    o_ref[...] = _dot(h, w_ref[...]).astype(o_ref.dtype)


def _inproj(x2d, gain, w):
    t = x2d.shape[0]
    return pl.pallas_call(
        _inproj_kernel,
        out_shape=jax.ShapeDtypeStruct((t, PROJ_WIDTH), BF16),
        grid=(t // TOKEN_TILE,),
        in_specs=[
            pl.BlockSpec((TOKEN_TILE, D_MODEL), lambda i: (i, 0)),
            _const_spec((1, D_MODEL)),
            _const_spec((D_MODEL, PROJ_WIDTH)),
        ],
        out_specs=pl.BlockSpec((TOKEN_TILE, PROJ_WIDTH), lambda i: (i, 0)),
        compiler_params=_compiler_params(),
        name="inproj",
    )(x2d, gain, w)


def _segment_mean_square(x, seg):
    w = x.shape[-1]
    same = (lax.broadcasted_iota(jnp.int32, (w, w), 0) // seg) == (lax.broadcasted_iota(jnp.int32, (w, w), 1) // seg)
    ones = jnp.where(same, 1.0, 0.0).astype(BF16)
    sq = x * x
    hi = sq.astype(BF16)
    lo = (sq - hi.astype(F32)).astype(BF16)
    return (_dot(hi, ones) + _dot(lo, ones)) * (1.0 / seg)


def _ones_row_block(width):
    return jnp.where(lax.broadcasted_iota(jnp.int32, (16, width), 0) == 0, 1.0, 0.0).astype(BF16)


def _normalised_heads(accs):
    return jnp.concatenate([acc[:HEAD_DIM] / acc[HEAD_DIM:HEAD_DIM + 1] for acc in accs], axis=0)


def _pipelined_softmax_pv(n_tiles, n_heads, n_chunks, score_chunks, value_chunk, combine, s_scr, acc_scr, o_ref,
                          exp_fn=jnp.exp2):
    assert n_heads % 2 == 0
    rows = s_scr.shape[1] // n_chunks

    def scores_into(slot, qi, h):
        m8 = None
        for c, st in enumerate(score_chunks(qi, h)):
            s_scr[slot, c * rows:(c + 1) * rows, :] = st
            cm = jnp.max(st.reshape(-1, 8, st.shape[-1]), axis=0)
            m8 = cm if m8 is None else jnp.maximum(m8, cm)
        return jnp.max(m8, axis=0, keepdims=True)

    def write_tile(qi):
        r0 = qi * Q_TILE if isinstance(qi, int) else pl.multiple_of(qi * Q_TILE, Q_TILE)
        o_ref[0, pl.ds(r0, Q_TILE), :] = combine([acc_scr[h] for h in range(n_heads)]).T.astype(o_ref.dtype)

    acc_scr[...] = jnp.ones_like(acc_scr)

    def body(qi, m):
        next_qi = jnp.minimum(qi + 1, n_tiles - 1)
        for h in range(n_heads):
            slot = h % 2
            nq, nh = (qi, h + 1) if h + 1 < n_heads else (next_qi, 0)
            next_m = scores_into(1 - slot, nq, nh)
            if h == 0:
                write_tile(jnp.maximum(qi - 1, 0))
            acc = None
            for c in range(n_chunks):
                p = exp_fn(s_scr[slot, c * rows:(c + 1) * rows, :] - m).astype(BF16)
                part = _dot(value_chunk(qi, h, c), p)
                acc = part if acc is None else acc + part
            acc_scr[h] = acc
            m = next_m
        return m

    lax.fori_loop(0, n_tiles, body, scores_into(0, 0, 0), unroll=2)
    write_tile(n_tiles - 1)


def _proj_spec(width, block):
    return pl.BlockSpec((1, SEQ, width), lambda b: (b, 0, block))


def _na_kernel(q_ref, k_ref, v_ref, tab_ref, o_ref, qt_scr, k_scr, vt_scr, s_scr, acc_scr):
    assert NA_GROUP_ROWS * GRID_W == Q_TILE and NA_BAND_ROWS % NA_GROUP_ROWS == 0
    n_tiles = SEQ // Q_TILE
    band_chunks = NA_BAND_ROWS // NA_GROUP_ROWS
    pair = 2 * HEAD_DIM
    q_scale = HEAD_DIM ** -0.5 * LOG2E

    def prep(ci, carry):
        r0 = pl.multiple_of(ci * Q_TILE, Q_TILE)
        rows = pl.ds(r0, Q_TILE)
        qt = (q_ref[0, rows, :].astype(F32) * q_scale).T
        row = lax.broadcasted_iota(jnp.int32, (pair, Q_TILE), 0)
        for h in range(N_HEADS):
            own = (row >= (h % 2) * HEAD_DIM) & (row < (h % 2 + 1) * HEAD_DIM)
            qt_scr[ci, h] = jnp.where(own, qt[(h // 2) * pair:(h // 2 + 1) * pair], 0.0).astype(BF16)
        for p in range(N_HEADS // 2):
            k_scr[p, rows, :] = k_ref[0, rows, p * pair:(p + 1) * pair].astype(BF16)
        vt = v_ref[0, rows, :].astype(F32).T.astype(BF16)
        for h in range(N_HEADS):
            vt_scr[h, ci, 0:HEAD_DIM, :] = vt[h * HEAD_DIM:(h + 1) * HEAD_DIM]
            vt_scr[h, ci, HEAD_DIM:, :] = _ones_row_block(Q_TILE)
        return carry

    lax.fori_loop(0, n_tiles, prep, 0, unroll=4)

    def key_chunk(gi, c):
        return jnp.clip(gi - (NA_WIN_ROWS // 2) // NA_GROUP_ROWS, 0, n_tiles - band_chunks) + c

    def score_chunks(gi, h):
        kind = jnp.where(gi == 0, 0, jnp.where(gi == n_tiles - 1, 2, 1))
        k0 = pl.multiple_of(key_chunk(gi, 0) * Q_TILE, Q_TILE)
        return [_dot(k_scr[h // 2, pl.ds(k0 + c * Q_TILE, Q_TILE), :], qt_scr[gi, h])
                + tab_ref[kind, h, c * Q_TILE:(c + 1) * Q_TILE, :] for c in range(band_chunks)]

    def value_chunk(gi, h, c):
        return vt_scr[h, key_chunk(gi, c)]

    _pipelined_softmax_pv(n_tiles, N_HEADS, band_chunks, score_chunks, value_chunk, _normalised_heads, s_scr, acc_scr,
                          o_ref)


def _na_tables(rel_bias):
    a = np.arange(NA_GROUP_ROWS)[:, None, None, None]
    c = np.arange(GRID_W)[None, :, None, None]
    i = np.arange(NA_BAND_ROWS)[None, None, :, None]
    kc = np.arange(GRID_W)[None, None, None, :]
    cs = np.clip(c - NA_WIN_COLS // 2, 0, GRID_W - NA_WIN_COLS)
    col_ok = (kc >= cs) & (kc < cs + NA_WIN_COLS)
    dc = kc - c + NA_WIN_COLS - 1
    col_sel = (dc[..., None] == np.arange(2 * NA_WIN_COLS - 1)) & col_ok[..., None]
    row_sels = []
    last_r0 = GRID_ROWS - NA_GROUP_ROWS
    for r0, band_row in ((0, 0), (NA_GROUP_ROWS, 0), (last_r0, GRID_ROWS - NA_BAND_ROWS)):
        r = r0 + a
        rs = np.clip(r - NA_WIN_ROWS // 2, 0, GRID_ROWS - NA_WIN_ROWS)
        key_row = band_row + i
        row_ok = (key_row >= rs) & (key_row < rs + NA_WIN_ROWS)
        dr = key_row - r + NA_WIN_ROWS - 1
        row_sels.append((dr[..., None] == np.arange(2 * NA_WIN_ROWS - 1)) & row_ok[..., None])
    row_sel = np.stack(row_sels)[:, :, 0, :, 0, :].astype(np.float32)
    col_sel = col_sel[0, :, 0, :, :].astype(np.float32)
    vals = jnp.einsum("taiu,huv,ckv->thikac", row_sel, rel_bias.astype(F32), col_sel,
                      precision=lax.Precision.HIGHEST)
    inside = np.einsum("taiu,ckv->tikac", row_sel, col_sel) > 0
    tab = jnp.where(inside[:, None], vals * LOG2E, NEG_BIG)
    q_rows, band = NA_GROUP_ROWS * GRID_W, NA_BAND_ROWS * GRID_W
    return tab.reshape(3, N_HEADS, band, q_rows)


def _na_attention(proj, tables):
    b = proj.shape[0]
    return pl.pallas_call(
        _na_kernel,
        out_shape=jax.ShapeDtypeStruct((b, SEQ, GROUP_WIDTH), BF16),
        grid=(b,),
        in_specs=[
            _proj_spec(256, B256_NA_Q), _proj_spec(256, B256_NA_K), _proj_spec(256, B256_NA_V),
            _const_spec(tables.shape),
        ],
        out_specs=pl.BlockSpec((1, SEQ, GROUP_WIDTH), lambda i: (i, 0, 0)),
        scratch_shapes=[pltpu.VMEM((SEQ // Q_TILE, N_HEADS, 2 * HEAD_DIM, Q_TILE), BF16),
                        pltpu.VMEM((N_HEADS // 2, SEQ, 2 * HEAD_DIM), BF16),
                        pltpu.VMEM((N_HEADS, SEQ // Q_TILE, HEAD_DIM + 16, Q_TILE), BF16),
                        pltpu.VMEM((2, NA_BAND_ROWS * GRID_W, Q_TILE), F32),
                        pltpu.VMEM((N_HEADS, HEAD_DIM + 16, Q_TILE), F32)],
        compiler_params=_compiler_params(),
        name="na_attention",
    )(proj, proj, proj, tables)


def _alibi_slope(h):
    return 2.0 ** (-8.0 * (h + 1) / N_HEADS)


def _diff_kernel(lambda_init, q_ref, k_ref, v_ref, lq1_ref, lk1_ref, lq2_ref, lk2_ref, subln_ref, diag_ref, o_ref,
                 qt_scr, qf_scr, k_scr, vt_scr, s_scr, acc_scr):
    scale = DIFF_QK_DIM ** -0.5
    n_tiles = SEQ // Q_TILE
    pair = 2 * HEAD_DIM
    n_feat = 16
    lam = (jnp.exp(jnp.sum(lq1_ref[...] * lk1_ref[...], axis=-1, keepdims=True))
           - jnp.exp(jnp.sum(lq2_ref[...] * lk2_ref[...], axis=-1, keepdims=True)) + lambda_init)

    def prep(ci, carry):
        r0 = pl.multiple_of(ci * Q_TILE, Q_TILE)
        rows = pl.ds(r0, Q_TILE)
        qt = (q_ref[0, rows, :].astype(F32) * scale).T
        row = lax.broadcasted_iota(jnp.int32, (pair, Q_TILE), 0)
        feat_row = lax.broadcasted_iota(jnp.int32, (n_feat, Q_TILE), 0)
        i = r0 + lax.broadcasted_iota(jnp.int32, (n_feat, Q_TILE), 1)
        i_lo = i % POS_SPLIT
        i_hi = i - i_lo
        for h in range(N_HEADS):
            group = qt[(h // 2) * pair:(h // 2 + 1) * pair]
            for mp in range(2):
                lo = (h % 2) * HEAD_DIM + mp * DIFF_QK_DIM
                keep = (row >= lo) & (row < lo + DIFF_QK_DIM)
                qt_scr[ci, 2 * h + mp] = jnp.where(keep, group, 0.0).astype(BF16)
            slope = _alibi_slope(h)
            feat = jnp.where(feat_row < 2, slope,
                             jnp.where(feat_row == 2, -slope * i_hi.astype(F32),
                                       jnp.where(feat_row == 3, -slope * i_lo.astype(F32), 0.0)))
            qf_scr[ci, h, 0:n_feat, :] = feat.astype(BF16)
            qf_scr[ci, h, n_feat:, :] = (-feat).astype(BF16)
        lane = lax.broadcasted_iota(jnp.int32, (Q_TILE, pair), 1)
        j = r0 + lax.broadcasted_iota(jnp.int32, (Q_TILE, pair), 0)
        j_lo = j % POS_SPLIT
        k_feat = jnp.where(lane == 0, (j - j_lo).astype(F32),
                           jnp.where(lane == 1, j_lo.astype(F32), jnp.where(lane < 4, 1.0, 0.0))).astype(BF16)
        for p in range(N_HEADS // 2):
            k_scr[p, rows, 0:pair] = k_ref[0, rows, p * pair:(p + 1) * pair].astype(BF16)
            k_scr[p, rows, pair:] = k_feat
        vt = v_ref[0, rows, :].astype(F32).T.astype(BF16)
        for h in range(N_HEADS):
            vt_scr[h, ci, 0:HEAD_DIM, :] = vt[h * HEAD_DIM:(h + 1) * HEAD_DIM]
            vt_scr[h, ci, HEAD_DIM:, :] = _ones_row_block(Q_TILE)
        return carry

    lax.fori_loop(0, n_tiles, prep, 0, unroll=4)

    def key_chunk(qi, c):
        wrapped = qi + c >= n_tiles
        return jnp.where(wrapped, qi + c - n_tiles, qi + c), wrapped

    def score_chunk(qi, item, c):
        h = item // 2
        kc, left = key_chunk(qi, c)
        form = 0 if c == 0 else pl.multiple_of(jnp.where(left, 0, n_feat), n_feat)
        feat = qf_scr[qi, h, pl.ds(form, n_feat), :]
        rhs = jnp.concatenate([qt_scr[qi, item], feat, jnp.zeros((pair - n_feat, Q_TILE), BF16)], axis=0)
        st = _dot(k_scr[h // 2, pl.ds(pl.multiple_of(kc * Q_TILE, Q_TILE), Q_TILE), :], rhs)
        return st + diag_ref[h] if c == 0 else st

    def score_chunks(qi, item):
        return [score_chunk(qi, item, c) for c in range(n_tiles)]

    def value_chunk(qi, item, c):
        return vt_scr[item // 2, key_chunk(qi, c)[0]]

    def combine(accs):
        outs = []
        for h in range(N_HEADS):
            a1, a2 = accs[2 * h], accs[2 * h + 1]
            o = a1[:HEAD_DIM] / a1[HEAD_DIM:HEAD_DIM + 1] - lam * (a2[:HEAD_DIM] / a2[HEAD_DIM:HEAD_DIM + 1])
            r = lax.rsqrt(jnp.mean(o * o, axis=0, keepdims=True) + EPS)
            outs.append(o * r * subln_ref[...] * (1.0 - lambda_init))
        return jnp.concatenate(outs, axis=0)

    _pipelined_softmax_pv(n_tiles, 2 * N_HEADS, n_tiles, score_chunks, value_chunk, combine, s_scr, acc_scr, o_ref,
                          exp_fn=jnp.exp)


def _diff_attention(proj, lq1, lk1, lq2, lk2, subln, lambda_init):
    b = proj.shape[0]
    vec = _const_spec((1, DIFF_QK_DIM))
    n_tiles = SEQ // Q_TILE
    rel = np.arange(Q_TILE)
    over = -2.0 * np.maximum(rel[:, None] - rel[None, :], 0).astype(np.float32)
    diag = jnp.asarray(np.stack([_alibi_slope(h) * over for h in range(N_HEADS)]))
    subln_cols = jnp.broadcast_to(subln.reshape(HEAD_DIM, 1), (HEAD_DIM, Q_TILE))
    return pl.pallas_call(
        functools.partial(_diff_kernel, lambda_init),
        out_shape=jax.ShapeDtypeStruct((b, SEQ, GROUP_WIDTH), BF16),
        grid=(b,),
        in_specs=[
            _proj_spec(256, B256_DF_Q), _proj_spec(256, B256_DF_K), _proj_spec(256, B256_DF_V),
            vec, vec, vec, vec, _const_spec((HEAD_DIM, Q_TILE)), _const_spec(diag.shape),
        ],
        out_specs=pl.BlockSpec((1, SEQ, GROUP_WIDTH), lambda i: (i, 0, 0)),
        scratch_shapes=[pltpu.VMEM((n_tiles, 2 * N_HEADS, 2 * HEAD_DIM, Q_TILE), BF16),
                        pltpu.VMEM((n_tiles, N_HEADS, 32, Q_TILE), BF16),
                        pltpu.VMEM((N_HEADS // 2, SEQ, 4 * HEAD_DIM), BF16),
                        pltpu.VMEM((N_HEADS, n_tiles, HEAD_DIM + 16, Q_TILE), BF16),
                        pltpu.VMEM((2, SEQ, Q_TILE), F32),
                        pltpu.VMEM((2 * N_HEADS, HEAD_DIM + 16, Q_TILE), F32)],
        compiler_params=_compiler_params(),
        name="diff_attention",
    )(proj, proj, proj, lq1, lk1, lq2, lk2, subln_cols, diag)


def _swap_rotary_halves(x):
    quarter = HEAD_DIM // 4
    first = (lax.broadcasted_iota(jnp.int32, x.shape, 1) % (2 * quarter)) < quarter
    lanes = x.shape[1]
    return jnp.where(first, pltpu.roll(x, lanes - quarter, 1), pltpu.roll(x, quarter, 1))


def _gqa_kernel(q_ref, k_ref, v_ref, qc_ref, qs_ref, kc_ref, ks_ref, o_ref,
                qt_scr, k_scr, vt_scr, s_scr, acc_scr):
    kv_heads = N_HEADS // 2
    n_tiles = SEQ // Q_TILE
    pair = 2 * HEAD_DIM

    def normed_rotary(x, gain_cos, gain_sin):
        rotated = x * gain_cos + _swap_rotary_halves(x) * gain_sin
        return rotated * lax.rsqrt(_segment_mean_square(x, HEAD_DIM) + EPS)

    def prep(ci, carry):
        r0 = pl.multiple_of(ci * Q_TILE, Q_TILE)
        rows = pl.ds(r0, Q_TILE)
        zeros = jnp.zeros((HEAD_DIM, Q_TILE), BF16)
        for half in range(2):
            lanes = slice(half * pair, (half + 1) * pair)
            y = normed_rotary(q_ref[0, rows, lanes].astype(F32), qc_ref[rows, :], qs_ref[rows, :])
            yt = y.T.astype(BF16)
            for j in range(2):
                h = 2 * half + j
                kv = h // (N_HEADS // kv_heads)
                for part in range(kv_heads):
                    block = yt[j * HEAD_DIM:(j + 1) * HEAD_DIM] if part == kv else zeros
                    qt_scr[ci, h * pair + part * HEAD_DIM:h * pair + (part + 1) * HEAD_DIM, :] = block
        k_scr[rows, :] = normed_rotary(k_ref[0, rows, :].astype(F32), kc_ref[rows, :],
                                       ks_ref[rows, :]).astype(BF16)
        vt = v_ref[0, rows, :].astype(F32).T.astype(BF16)
        for kv in range(kv_heads):
            vt_scr[kv, ci, 0:HEAD_DIM, :] = vt[kv * HEAD_DIM:(kv + 1) * HEAD_DIM]
            vt_scr[kv, ci, HEAD_DIM:, :] = _ones_row_block(Q_TILE)
        return carry

    lax.fori_loop(0, n_tiles, prep, 0, unroll=4)

    def score_chunks(qi, h):
        qt = qt_scr[qi, h * pair:(h + 1) * pair, :]
        return [_dot(k_scr[c * Q_TILE:(c + 1) * Q_TILE, :], qt) for c in range(n_tiles)]

    def value_chunk(qi, h, c):
        return vt_scr[h // (N_HEADS // kv_heads), c]

    _pipelined_softmax_pv(n_tiles, N_HEADS, n_tiles, score_chunks, value_chunk, _normalised_heads, s_scr, acc_scr,
                          o_ref)


def _gqa_attention(proj, gq, gqp, gk, gkp, cos, sin):
    b = proj.shape[0]
    tab = _const_spec((SEQ, 2 * HEAD_DIM))
    q_scale = HEAD_DIM ** -0.5 * LOG2E
    _, sign = _rot_perm(2 * HEAD_DIM, HEAD_DIM // 2)
    tables = (gq * cos * q_scale, gqp * (sign * q_scale) * sin, gk * cos, gkp * sign * sin)
    return pl.pallas_call(
        _gqa_kernel,
        out_shape=jax.ShapeDtypeStruct((b, SEQ, GROUP_WIDTH), BF16),
        grid=(b,),
        in_specs=[
            _proj_spec(256, B256_GQ_Q), _proj_spec(128, B128_GQ_K), _proj_spec(128, B128_GQ_V),
            tab, tab, tab, tab,
        ],
        out_specs=pl.BlockSpec((1, SEQ, GROUP_WIDTH), lambda i: (i, 0, 0)),
        scratch_shapes=[pltpu.VMEM((SEQ // Q_TILE, N_HEADS * 2 * HEAD_DIM, Q_TILE), BF16),
                        pltpu.VMEM((SEQ, 2 * HEAD_DIM), BF16),
                        pltpu.VMEM((N_HEADS // 2, SEQ // Q_TILE, HEAD_DIM + 16, Q_TILE), BF16),
                        pltpu.VMEM((2, SEQ, Q_TILE), F32),
                        pltpu.VMEM((N_HEADS, HEAD_DIM + 16, Q_TILE), F32)],
        compiler_params=_compiler_params(),
        name="gqa_attention",
    )(proj, proj, proj, *tables)


def _mla_kernel(cq_ref, ckv_ref, kr_ref, gq_ref, gkv_ref, wuq_ref, wukv_ref, cos_ref, sin_ref, o_ref,
                qt_scr, k_scr, vt_scr, s_scr, acc_scr):
    q_scale = (MLA_NOPE_DIM + MLA_ROPE_DIM) ** -0.5 * LOG2E
    n_tiles = SEQ // Q_TILE
    rot0 = N_HEADS * 128
    v0 = N_HEADS * 128

    def prep(ci, carry):
        r0 = pl.multiple_of(ci * Q_TILE, Q_TILE)
        rows = pl.ds(r0, Q_TILE)
        cos, sin = cos_ref[rows, :], sin_ref[rows, :]
        cq = cq_ref[0, rows, :].astype(F32)
        cqn = (cq * _rms(cq) * gq_ref[...]).astype(BF16)
        ckv = ckv_ref[0, rows, :].astype(F32)
        ckvn = (ckv * _rms(ckv) * gkv_ref[...]).astype(BF16)
        kr = kr_ref[0, rows, :].astype(F32)
        k_rope = kr * cos + pltpu.roll(kr, 128 - MLA_ROPE_DIM, 1) * sin
        two = lambda t: jnp.concatenate([t, t], axis=1)
        q_cos, q_sin, k_rope2 = two(cos * q_scale), two(sin * q_scale), two(k_rope)
        for p in range(N_HEADS // 2):
            cols = slice(p * 256, (p + 1) * 256)
            rot_cols = slice(rot0 + p * 256, rot0 + (p + 1) * 256)
            q2 = _dot(cqn, wuq_ref[:, cols]) * q_cos + _dot(cqn, wuq_ref[:, rot_cols]) * q_sin
            qt_scr[ci, p * 256:(p + 1) * 256, :] = q2.T.astype(BF16)
            k2 = (_dot(ckvn, wukv_ref[:, cols]) + k_rope2).astype(BF16)
            k_scr[2 * p, rows, :] = k2[:, :128]
            k_scr[2 * p + 1, rows, :] = k2[:, 128:]
        vt = _dot(ckvn, wukv_ref[:, v0:v0 + GROUP_WIDTH]).T.astype(BF16)
        for h in range(N_HEADS):
            vt_scr[h, ci, 0:HEAD_DIM, :] = vt[h * HEAD_DIM:(h + 1) * HEAD_DIM]
            vt_scr[h, ci, HEAD_DIM:, :] = _ones_row_block(Q_TILE)
        return carry

    lax.fori_loop(0, n_tiles, prep, 0, unroll=4)

    def score_chunks(qi, h):
        qt = qt_scr[qi, h * 128:(h + 1) * 128, :]
        return [_dot(k_scr[h, c * Q_TILE:(c + 1) * Q_TILE, :], qt) for c in range(n_tiles)]

    def value_chunk(qi, h, c):
        return vt_scr[h, c]

    _pipelined_softmax_pv(n_tiles, N_HEADS, n_tiles, score_chunks, value_chunk, _normalised_heads, s_scr, acc_scr,
                          o_ref)


def _mla_attention(proj, gq, gkv, wuq, wukv, cos, sin):
    b = proj.shape[0]
    tab = _const_spec((SEQ, 128))
    return pl.pallas_call(
        _mla_kernel,
        out_shape=jax.ShapeDtypeStruct((b, SEQ, GROUP_WIDTH), BF16),
        grid=(b,),
        in_specs=[
            _proj_spec(256, B256_ML_CQ), _proj_spec(128, B128_ML_CKV), _proj_spec(128, B128_ML_KR),
            _const_spec((1, GROUP_WIDTH)), _const_spec((1, MLA_KV_LORA)),
            _const_spec(wuq.shape), _const_spec(wukv.shape), tab, tab,
        ],
        out_specs=pl.BlockSpec((1, SEQ, GROUP_WIDTH), lambda i: (i, 0, 0)),
        scratch_shapes=[pltpu.VMEM((SEQ // Q_TILE, N_HEADS * 128, Q_TILE), BF16),
                        pltpu.VMEM((N_HEADS, SEQ, 128), BF16),
                        pltpu.VMEM((N_HEADS, SEQ // Q_TILE, HEAD_DIM + 16, Q_TILE), BF16),
                        pltpu.VMEM((2, SEQ, Q_TILE), F32),
                        pltpu.VMEM((N_HEADS, HEAD_DIM + 16, Q_TILE), F32)],
        compiler_params=_compiler_params(),
        name="mla_attention",
    )(proj, proj, proj, gq, gkv, wuq, wukv, cos, sin)


def _post_kernel(x_ref, a_ref, b_ref, c_ref, d_ref, wo_ref, g_mix_ref, g_pre_ref, wg_ref, wu_ref, wd_ref,
                 g_ffn_ref, o_ref, acc_ref):
    mix = (_dot(a_ref[...], wo_ref[0]) + _dot(b_ref[...], wo_ref[1])
           + _dot(c_ref[...], wo_ref[2]) + _dot(d_ref[...], wo_ref[3]))
    x = x_ref[...] + mix * _rms(mix) * g_mix_ref[...]
    h = (x * _rms(x) * g_pre_ref[...]).astype(BF16)
    acc_ref[...] = jnp.zeros_like(acc_ref)

    def body(ci, carry):
        gate = _dot(h, wg_ref[ci])
        up = _dot(h, wu_ref[ci])
        act = (gate * jax.nn.sigmoid(gate) * up).astype(BF16)
        acc_ref[...] += _dot(act, wd_ref[ci])
        return carry

    lax.fori_loop(0, D_FF // FF_CHUNK, body, 0, unroll=True)
    f = acc_ref[...]
    o_ref[...] = x + f * _rms(f) * g_ffn_ref[...]


def _post(x2d, a, b, c, d, wo, g_mix, g_pre, wg, wu, wd, g_ffn):
    t = x2d.shape[0]
    tok = lambda w: pl.BlockSpec((TOKEN_TILE, w), lambda i: (i, 0))
    vec = _const_spec((1, D_MODEL))
    return pl.pallas_call(
        _post_kernel,
        out_shape=jax.ShapeDtypeStruct((t, D_MODEL), F32),
        grid=(t // TOKEN_TILE,),
        in_specs=[
            tok(D_MODEL), tok(GROUP_WIDTH), tok(GROUP_WIDTH), tok(GROUP_WIDTH), tok(GROUP_WIDTH),
            _const_spec(wo.shape), vec, vec, _const_spec(wg.shape), _const_spec(wu.shape), _const_spec(wd.shape),
            vec,
        ],
        out_specs=tok(D_MODEL),
        scratch_shapes=[pltpu.VMEM((TOKEN_TILE, D_MODEL), F32)],
        compiler_params=_compiler_params(),
        name="outproj_swiglu",
    )(x2d, a, b, c, d, wo, g_mix, g_pre, wg, wu, wd, g_ffn)


def _layout_w_in(w):
    src32, sign32 = _rot_perm(MLA_ROPE_DIM, MLA_ROPE_DIM)
    k_rope = w[:, O_ML_KR:O_ML_KR + MLA_ROPE_DIM]
    cols = [
        w[:, :O_GQ_K],
        w[:, O_ML_CQ:O_ML_CKV],
        w[:, O_GQ_K:O_ML_CQ],
        w[:, O_ML_CKV:O_ML_KR],
        jnp.zeros((D_MODEL, MLA_NOPE_DIM), w.dtype), k_rope, k_rope[:, src32] * sign32,
    ]
    return jnp.concatenate(cols, axis=1).astype(BF16)


def _layout_mla_weights(w_uq, w_ukv):
    src32, sign32 = _rot_perm(MLA_ROPE_DIM, MLA_ROPE_DIM)
    wq = w_uq.reshape(GROUP_WIDTH, N_HEADS, MLA_NOPE_DIM + MLA_ROPE_DIM)
    rope = wq[:, :, MLA_NOPE_DIM:]
    tail = 128 - MLA_NOPE_DIM - MLA_ROPE_DIM
    q_main = jnp.pad(wq, ((0, 0), (0, 0), (0, tail)))
    q_rot = jnp.pad(rope[:, :, src32] * sign32, ((0, 0), (0, 0), (MLA_NOPE_DIM, tail)))
    wuq = jnp.concatenate([q_main.reshape(GROUP_WIDTH, -1), q_rot.reshape(GROUP_WIDTH, -1)], axis=1)
    wkv = w_ukv.reshape(MLA_KV_LORA, N_HEADS, MLA_NOPE_DIM + HEAD_DIM)
    k_nope = jnp.pad(wkv[:, :, :MLA_NOPE_DIM], ((0, 0), (0, 0), (0, 128 - MLA_NOPE_DIM)))
    vals = wkv[:, :, MLA_NOPE_DIM:]
    wukv = jnp.concatenate([k_nope.reshape(MLA_KV_LORA, -1), vals.reshape(MLA_KV_LORA, -1)], axis=1)
    return wuq.astype(BF16), wukv.astype(BF16)


def _rotary_tables():
    pos = jnp.arange(SEQ)
    half = HEAD_DIM // 2
    inv = ROPE_THETA ** (-jnp.arange(0, half, 2, dtype=F32) / half)

    def angles(p):
        ang = p.astype(F32)[:, None] * inv[None, :]
        return jnp.concatenate([ang, ang], axis=-1)

    axial = jnp.concatenate([angles(pos // GRID_W), angles(pos % GRID_W)], axis=-1)
    seq = angles(pos)
    tail = 128 - MLA_NOPE_DIM - MLA_ROPE_DIM
    mla_cos = jnp.concatenate([jnp.ones((SEQ, MLA_NOPE_DIM), F32), jnp.cos(seq), jnp.zeros((SEQ, tail), F32)], -1)
    mla_sin = jnp.pad(jnp.sin(seq), ((0, 0), (MLA_NOPE_DIM, tail)))
    pair = lambda t: jnp.tile(t, (1, 2))
    return pair(jnp.cos(axial)), pair(jnp.sin(axial)), mla_cos, mla_sin


def kernel(x, pre_mix_norm, w_in, na_rel_bias, diff_lambda_q1, diff_lambda_k1, diff_lambda_q2, diff_lambda_k2,
           diff_subln, gqa_q_norm, gqa_k_norm, mla_q_norm, mla_kv_norm, mla_w_uq, mla_w_ukv, w_o, post_mix_norm,
           pre_ffn_norm, ffn_w_gate_up, ffn_w_down, post_ffn_norm):
    b, s, d = x.shape
    assert (s, d) == (SEQ, D_MODEL)
    depth = w_in.shape[0]
    src64, _ = _rot_perm(HEAD_DIM, HEAD_DIM // 2)
    ax_cos, ax_sin, mla_cos, mla_sin = _rotary_tables()
    row = lambda v: v.reshape(1, -1).astype(F32)
    n_chunks = D_FF // FF_CHUNK

    x2d = x.reshape(b * s, d)
    for l in range(depth):
        lambda_init = 0.8 - 0.6 * math.exp(-0.3 * l)
        proj = _inproj(x2d, row(pre_mix_norm[l]), _layout_w_in(w_in[l])).reshape(b, s, PROJ_WIDTH)

        a_out = _na_attention(proj, _na_tables(na_rel_bias[l]))
        b_out = _diff_attention(proj, row(diff_lambda_q1[l]), row(diff_lambda_k1[l]), row(diff_lambda_q2[l]),
                                row(diff_lambda_k2[l]), row(diff_subln[l]), lambda_init)
        pair_row = lambda v: row(jnp.tile(v, 2))
        c_out = _gqa_attention(proj, pair_row(gqa_q_norm[l]), pair_row(gqa_q_norm[l][src64]),
                               pair_row(gqa_k_norm[l]), pair_row(gqa_k_norm[l][src64]), ax_cos, ax_sin)
        wuq, wukv = _layout_mla_weights(mla_w_uq[l], mla_w_ukv[l])
        d_out = _mla_attention(proj, row(mla_q_norm[l]), row(mla_kv_norm[l]), wuq, wukv, mla_cos, mla_sin)

        gate_up = ffn_w_gate_up[l].astype(BF16)
        wg = gate_up[:, :D_FF].reshape(d, n_chunks, FF_CHUNK).transpose(1, 0, 2)
        wu = gate_up[:, D_FF:].reshape(d, n_chunks, FF_CHUNK).transpose(1, 0, 2)
        wd = ffn_w_down[l].astype(BF16).reshape(n_chunks, FF_CHUNK, d)
        wo = w_o[l].astype(BF16).reshape(N_HEADS, GROUP_WIDTH, d)
        flat = lambda t: t.reshape(b * s, GROUP_WIDTH)
        x2d = _post(x2d, flat(a_out), flat(b_out), flat(c_out), flat(d_out), wo, row(post_mix_norm[l]),
                    row(pre_ffn_norm[l]), wg, wu, wd, row(post_ffn_norm[l]))
    return x2d.reshape(b, s, d)
```

```python
import functools
import math

import numpy as np
import jax
import jax.numpy as jnp
from jax import lax
from jax.experimental import pallas as pl
from jax.experimental.pallas import tpu as pltpu

F32 = jnp.float32
BF16 = jnp.bfloat16

D_MODEL = 1024
SEQ = 2048
GRID_W = 64
GRID_ROWS = SEQ // GRID_W
HEAD_DIM = 64
N_HEADS = 4
GROUP_WIDTH = 256
EPS = 1e-6
ROPE_THETA = 10000.0

NA_WIN_ROWS = 8
NA_WIN_COLS = 16
DIFF_QK_DIM = 32
POS_SPLIT = 64
MLA_NOPE_DIM = 64
MLA_ROPE_DIM = 32
MLA_KV_LORA = 128
D_FF = 2816

O_GQ_Q, O_GQ_K, O_GQ_V, O_ML_CQ, O_ML_CKV, O_ML_KR = 1536, 1792, 1920, 2048, 2304, 2432

PROJ_WIDTH = 2560
B256_NA_Q, B256_NA_K, B256_NA_V, B256_DF_Q, B256_DF_K, B256_DF_V, B256_GQ_Q, B256_ML_CQ = range(8)
B128_GQ_K, B128_GQ_V, B128_ML_CKV, B128_ML_KR = range(16, 20)

VMEM_LIMIT_BYTES = 56 * 1024 * 1024

TOKEN_TILE = 1024
FF_CHUNK = 256
Q_TILE = 256
NA_GROUP_ROWS = 4
NA_BAND_ROWS = 12
NEG_BIG = -1e30
LOG2E = math.log2(math.e)


def _rot_perm(width, group):
    j = np.arange(width)
    jj = j % group
    half = group // 2
    src = (j // group) * group + (jj + half) % group
    sign = np.where(jj < half, -1.0, 1.0).astype(np.float32)
    return src, sign


def _compiler_params():
    return pltpu.CompilerParams(dimension_semantics=("arbitrary",), vmem_limit_bytes=VMEM_LIMIT_BYTES)


def _const_spec(shape):
    zeros = (0,) * len(shape)
    return pl.BlockSpec(shape, lambda i: zeros, pipeline_mode=pl.Buffered(1))


def _rms(x):
    return lax.rsqrt(jnp.mean(x * x, axis=-1, keepdims=True) + EPS)


def _dot(a, b):
    return jnp.dot(a, b, preferred_element_type=F32)


def _inproj_kernel(x_ref, g_ref, w_ref, o_ref):
    x = x_ref[...]
    h = (x * _rms(x) * g_ref[...]).astype(BF16)
    o_ref[...] = _dot(h, w_ref[...])


def _inproj(x2d, gain, w):
    t = x2d.shape[0]
    return pl.pallas_call(
        _inproj_kernel,
        out_shape=jax.ShapeDtypeStruct((t, PROJ_WIDTH), F32),
        grid=(t // TOKEN_TILE,),
        in_specs=[
            pl.BlockSpec((TOKEN_TILE, D_MODEL), lambda i: (i, 0)),
            _const_spec((1, D_MODEL)),
            _const_spec((D_MODEL, PROJ_WIDTH)),
        ],
        out_specs=pl.BlockSpec((TOKEN_TILE, PROJ_WIDTH), lambda i: (i, 0)),
        compiler_params=_compiler_params(),
        name="inproj",
    )(x2d, gain, w)


def _segment_mean_square(x, seg):
    w = x.shape[-1]
    same = (lax.broadcasted_iota(jnp.int32, (w, w), 0) // seg) == (lax.broadcasted_iota(jnp.int32, (w, w), 1) // seg)
    ones = jnp.where(same, 1.0, 0.0).astype(BF16)
    sq = x * x
    hi = sq.astype(BF16)
    lo = (sq - hi.astype(F32)).astype(BF16)
    return (_dot(hi, ones) + _dot(lo, ones)) * (1.0 / seg)


def _ones_row_block(width):
    return jnp.where(lax.broadcasted_iota(jnp.int32, (16, width), 0) == 0, 1.0, 0.0).astype(BF16)


def _normalised_heads(accs):
    return jnp.concatenate([acc[:HEAD_DIM] / acc[HEAD_DIM:HEAD_DIM + 1] for acc in accs], axis=0)


def _pipelined_softmax_pv(n_tiles, n_heads, n_chunks, score_chunks, value_chunk, combine, s_scr, acc_scr, o_ref,
                          exp_fn=jnp.exp2):
    assert n_heads % 2 == 0
    rows = s_scr.shape[1] // n_chunks

    def scores_into(slot, qi, h):
        m8 = None
        for c, st in enumerate(score_chunks(qi, h)):
            s_scr[slot, c * rows:(c + 1) * rows, :] = st
            cm = jnp.max(st.reshape(-1, 8, st.shape[-1]), axis=0)
            m8 = cm if m8 is None else jnp.maximum(m8, cm)
        return jnp.max(m8, axis=0, keepdims=True)

    def write_tile(qi):
        r0 = qi * Q_TILE if isinstance(qi, int) else pl.multiple_of(qi * Q_TILE, Q_TILE)
        o_ref[0, pl.ds(r0, Q_TILE), :] = combine([acc_scr[h] for h in range(n_heads)]).T.astype(o_ref.dtype)

    acc_scr[...] = jnp.ones_like(acc_scr)

    def body(qi, m):
        next_qi = jnp.minimum(qi + 1, n_tiles - 1)
        for h in range(n_heads):
            slot = h % 2
            nq, nh = (qi, h + 1) if h + 1 < n_heads else (next_qi, 0)
            next_m = scores_into(1 - slot, nq, nh)
            if h == 0:
                write_tile(jnp.maximum(qi - 1, 0))
            acc = None
            for c in range(n_chunks):
                p = exp_fn(s_scr[slot, c * rows:(c + 1) * rows, :] - m).astype(BF16)
                part = _dot(value_chunk(qi, h, c), p)
                acc = part if acc is None else acc + part
            acc_scr[h] = acc
            m = next_m
        return m

    lax.fori_loop(0, n_tiles, body, scores_into(0, 0, 0), unroll=2)
    write_tile(n_tiles - 1)


def _proj_spec(width, block):
    return pl.BlockSpec((1, SEQ, width), lambda b: (b, 0, block))


def _na_kernel(q_ref, k_ref, v_ref, tab_ref, o_ref, qt_scr, k_scr, vt_scr, s_scr, acc_scr):
    assert NA_GROUP_ROWS * GRID_W == Q_TILE and NA_BAND_ROWS % NA_GROUP_ROWS == 0
    n_tiles = SEQ // Q_TILE
    band_chunks = NA_BAND_ROWS // NA_GROUP_ROWS
    pair = 2 * HEAD_DIM
    q_scale = HEAD_DIM ** -0.5 * LOG2E

    def prep(ci, carry):
        r0 = pl.multiple_of(ci * Q_TILE, Q_TILE)
        rows = pl.ds(r0, Q_TILE)
        qt = (q_ref[0, rows, :] * q_scale).T
        row = lax.broadcasted_iota(jnp.int32, (pair, Q_TILE), 0)
        for h in range(N_HEADS):
            own = (row >= (h % 2) * HEAD_DIM) & (row < (h % 2 + 1) * HEAD_DIM)
            qt_scr[ci, h] = jnp.where(own, qt[(h // 2) * pair:(h // 2 + 1) * pair], 0.0).astype(BF16)
        for p in range(N_HEADS // 2):
            k_scr[p, rows, :] = k_ref[0, rows, p * pair:(p + 1) * pair].astype(BF16)
        vt = v_ref[0, rows, :].T.astype(BF16)
        for h in range(N_HEADS):
            vt_scr[h, ci, 0:HEAD_DIM, :] = vt[h * HEAD_DIM:(h + 1) * HEAD_DIM]
            vt_scr[h, ci, HEAD_DIM:, :] = _ones_row_block(Q_TILE)
        return carry

    lax.fori_loop(0, n_tiles, prep, 0, unroll=4)

    def key_chunk(gi, c):
        return jnp.clip(gi - (NA_WIN_ROWS // 2) // NA_GROUP_ROWS, 0, n_tiles - band_chunks) + c

    def score_chunks(gi, h):
        kind = jnp.where(gi == 0, 0, jnp.where(gi == n_tiles - 1, 2, 1))
        k0 = pl.multiple_of(key_chunk(gi, 0) * Q_TILE, Q_TILE)
        return [_dot(k_scr[h // 2, pl.ds(k0 + c * Q_TILE, Q_TILE), :], qt_scr[gi, h])
                + tab_ref[kind, h, c * Q_TILE:(c + 1) * Q_TILE, :] for c in range(band_chunks)]

    def value_chunk(gi, h, c):
        return vt_scr[h, key_chunk(gi, c)]

    _pipelined_softmax_pv(n_tiles, N_HEADS, band_chunks, score_chunks, value_chunk, _normalised_heads, s_scr, acc_scr,
                          o_ref)


def _na_tables(rel_bias):
    a = np.arange(NA_GROUP_ROWS)[:, None, None, None]
    c = np.arange(GRID_W)[None, :, None, None]
    i = np.arange(NA_BAND_ROWS)[None, None, :, None]
    kc = np.arange(GRID_W)[None, None, None, :]
    cs = np.clip(c - NA_WIN_COLS // 2, 0, GRID_W - NA_WIN_COLS)
    col_ok = (kc >= cs) & (kc < cs + NA_WIN_COLS)
    dc = kc - c + NA_WIN_COLS - 1
    col_sel = (dc[..., None] == np.arange(2 * NA_WIN_COLS - 1)) & col_ok[..., None]
    row_sels = []
    last_r0 = GRID_ROWS - NA_GROUP_ROWS
    for r0, band_row in ((0, 0), (NA_GROUP_ROWS, 0), (last_r0, GRID_ROWS - NA_BAND_ROWS)):
        r = r0 + a
        rs = np.clip(r - NA_WIN_ROWS // 2, 0, GRID_ROWS - NA_WIN_ROWS)
        key_row = band_row + i
        row_ok = (key_row >= rs) & (key_row < rs + NA_WIN_ROWS)
        dr = key_row - r + NA_WIN_ROWS - 1
        row_sels.append((dr[..., None] == np.arange(2 * NA_WIN_ROWS - 1)) & row_ok[..., None])
    row_sel = np.stack(row_sels)[:, :, 0, :, 0, :].astype(np.float32)
    col_sel = col_sel[0, :, 0, :, :].astype(np.float32)
    vals = jnp.einsum("taiu,huv,ckv->thikac", row_sel, rel_bias.astype(F32), col_sel,
                      precision=lax.Precision.HIGHEST)
    inside = np.einsum("taiu,ckv->tikac", row_sel, col_sel) > 0
    tab = jnp.where(inside[:, None], vals * LOG2E, NEG_BIG)
    q_rows, band = NA_GROUP_ROWS * GRID_W, NA_BAND_ROWS * GRID_W
    return tab.reshape(3, N_HEADS, band, q_rows)


def _na_attention(proj, tables):
    b = proj.shape[0]
    return pl.pallas_call(
        _na_kernel,
        out_shape=jax.ShapeDtypeStruct((b, SEQ, GROUP_WIDTH), BF16),
        grid=(b,),
        in_specs=[
            _proj_spec(256, B256_NA_Q), _proj_spec(256, B256_NA_K), _proj_spec(256, B256_NA_V),
            _const_spec(tables.shape),
        ],
        out_specs=pl.BlockSpec((1, SEQ, GROUP_WIDTH), lambda i: (i, 0, 0)),
        scratch_shapes=[pltpu.VMEM((SEQ // Q_TILE, N_HEADS, 2 * HEAD_DIM, Q_TILE), BF16),
                        pltpu.VMEM((N_HEADS // 2, SEQ, 2 * HEAD_DIM), BF16),
                        pltpu.VMEM((N_HEADS, SEQ // Q_TILE, HEAD_DIM + 16, Q_TILE), BF16),
                        pltpu.VMEM((2, NA_BAND_ROWS * GRID_W, Q_TILE), F32),
                        pltpu.VMEM((N_HEADS, HEAD_DIM + 16, Q_TILE), F32)],
        compiler_params=_compiler_params(),
        name="na_attention",
    )(proj, proj, proj, tables)


def _alibi_slope(h):
    return 2.0 ** (-8.0 * (h + 1) / N_HEADS)


def _diff_kernel(lambda_init, q_ref, k_ref, v_ref, lq1_ref, lk1_ref, lq2_ref, lk2_ref, subln_ref, diag_ref, o_ref,
                 qt_scr, qf_scr, k_scr, vt_scr, s_scr, acc_scr):
    scale = DIFF_QK_DIM ** -0.5
    n_tiles = SEQ // Q_TILE
    pair = 2 * HEAD_DIM
    n_feat = 16
    lam = (jnp.exp(jnp.sum(lq1_ref[...] * lk1_ref[...], axis=-1, keepdims=True))
           - jnp.exp(jnp.sum(lq2_ref[...] * lk2_ref[...], axis=-1, keepdims=True)) + lambda_init)

    def prep(ci, carry):
        r0 = pl.multiple_of(ci * Q_TILE, Q_TILE)
        rows = pl.ds(r0, Q_TILE)
        qt = (q_ref[0, rows, :] * scale).T
        row = lax.broadcasted_iota(jnp.int32, (pair, Q_TILE), 0)
        feat_row = lax.broadcasted_iota(jnp.int32, (n_feat, Q_TILE), 0)
        i = r0 + lax.broadcasted_iota(jnp.int32, (n_feat, Q_TILE), 1)
        i_lo = i % POS_SPLIT
        i_hi = i - i_lo
        for h in range(N_HEADS):
            group = qt[(h // 2) * pair:(h // 2 + 1) * pair]
            for mp in range(2):
                lo = (h % 2) * HEAD_DIM + mp * DIFF_QK_DIM
                keep = (row >= lo) & (row < lo + DIFF_QK_DIM)
                qt_scr[ci, 2 * h + mp] = jnp.where(keep, group, 0.0).astype(BF16)
            slope = _alibi_slope(h)
            feat = jnp.where(feat_row < 2, slope,
                             jnp.where(feat_row == 2, -slope * i_hi.astype(F32),
                                       jnp.where(feat_row == 3, -slope * i_lo.astype(F32), 0.0)))
            qf_scr[ci, h, 0:n_feat, :] = feat.astype(BF16)
            qf_scr[ci, h, n_feat:, :] = (-feat).astype(BF16)
        lane = lax.broadcasted_iota(jnp.int32, (Q_TILE, pair), 1)
        j = r0 + lax.broadcasted_iota(jnp.int32, (Q_TILE, pair), 0)
        j_lo = j % POS_SPLIT
        k_feat = jnp.where(lane == 0, (j - j_lo).astype(F32),
                           jnp.where(lane == 1, j_lo.astype(F32), jnp.where(lane < 4, 1.0, 0.0))).astype(BF16)
        for p in range(N_HEADS // 2):
            k_scr[p, rows, 0:pair] = k_ref[0, rows, p * pair:(p + 1) * pair].astype(BF16)
            k_scr[p, rows, pair:] = k_feat
        vt = v_ref[0, rows, :].T.astype(BF16)
        for h in range(N_HEADS):
            vt_scr[h, ci, 0:HEAD_DIM, :] = vt[h * HEAD_DIM:(h + 1) * HEAD_DIM]
            vt_scr[h, ci, HEAD_DIM:, :] = _ones_row_block(Q_TILE)
        return carry

    lax.fori_loop(0, n_tiles, prep, 0, unroll=4)

    def key_chunk(qi, c):
        wrapped = qi + c >= n_tiles
        return jnp.where(wrapped, qi + c - n_tiles, qi + c), wrapped

    def score_chunk(qi, item, c):
        h = item // 2
        kc, left = key_chunk(qi, c)
        form = 0 if c == 0 else pl.multiple_of(jnp.where(left, 0, n_feat), n_feat)
        feat = qf_scr[qi, h, pl.ds(form, n_feat), :]
        rhs = jnp.concatenate([qt_scr[qi, item], feat, jnp.zeros((pair - n_feat, Q_TILE), BF16)], axis=0)
        st = _dot(k_scr[h // 2, pl.ds(pl.multiple_of(kc * Q_TILE, Q_TILE), Q_TILE), :], rhs)
        return st + diag_ref[h] if c == 0 else st

    def score_chunks(qi, item):
        return [score_chunk(qi, item, c) for c in range(n_tiles)]

    def value_chunk(qi, item, c):
        return vt_scr[item // 2, key_chunk(qi, c)[0]]

    def combine(accs):
        outs = []
        for h in range(N_HEADS):
            a1, a2 = accs[2 * h], accs[2 * h + 1]
            o = a1[:HEAD_DIM] / a1[HEAD_DIM:HEAD_DIM + 1] - lam * (a2[:HEAD_DIM] / a2[HEAD_DIM:HEAD_DIM + 1])
            r = lax.rsqrt(jnp.mean(o * o, axis=0, keepdims=True) + EPS)
            outs.append(o * r * subln_ref[...] * (1.0 - lambda_init))
        return jnp.concatenate(outs, axis=0)

    _pipelined_softmax_pv(n_tiles, 2 * N_HEADS, n_tiles, score_chunks, value_chunk, combine, s_scr, acc_scr, o_ref,
                          exp_fn=jnp.exp)


def _diff_attention(proj, lq1, lk1, lq2, lk2, subln, lambda_init):
    b = proj.shape[0]
    vec = _const_spec((1, DIFF_QK_DIM))
    n_tiles = SEQ // Q_TILE
    rel = np.arange(Q_TILE)
    over = -2.0 * np.maximum(rel[:, None] - rel[None, :], 0).astype(np.float32)
    diag = jnp.asarray(np.stack([_alibi_slope(h) * over for h in range(N_HEADS)]))
    subln_cols = jnp.broadcast_to(subln.reshape(HEAD_DIM, 1), (HEAD_DIM, Q_TILE))
    return pl.pallas_call(
        functools.partial(_diff_kernel, lambda_init),
        out_shape=jax.ShapeDtypeStruct((b, SEQ, GROUP_WIDTH), BF16),
        grid=(b,),
        in_specs=[
            _proj_spec(256, B256_DF_Q), _proj_spec(256, B256_DF_K), _proj_spec(256, B256_DF_V),
            vec, vec, vec, vec, _const_spec((HEAD_DIM, Q_TILE)), _const_spec(diag.shape),
        ],
        out_specs=pl.BlockSpec((1, SEQ, GROUP_WIDTH), lambda i: (i, 0, 0)),
        scratch_shapes=[pltpu.VMEM((n_tiles, 2 * N_HEADS, 2 * HEAD_DIM, Q_TILE), BF16),
                        pltpu.VMEM((n_tiles, N_HEADS, 32, Q_TILE), BF16),
                        pltpu.VMEM((N_HEADS // 2, SEQ, 4 * HEAD_DIM), BF16),
                        pltpu.VMEM((N_HEADS, n_tiles, HEAD_DIM + 16, Q_TILE), BF16),
                        pltpu.VMEM((2, SEQ, Q_TILE), F32),
                        pltpu.VMEM((2 * N_HEADS, HEAD_DIM + 16, Q_TILE), F32)],
        compiler_params=_compiler_params(),
        name="diff_attention",
    )(proj, proj, proj, lq1, lk1, lq2, lk2, subln_cols, diag)


def _swap_rotary_halves(x):
    quarter = HEAD_DIM // 4
    first = (lax.broadcasted_iota(jnp.int32, x.shape, 1) % (2 * quarter)) < quarter
    lanes = x.shape[1]
    return jnp.where(first, pltpu.roll(x, lanes - quarter, 1), pltpu.roll(x, quarter, 1))


def _gqa_kernel(q_ref, k_ref, v_ref, qc_ref, qs_ref, kc_ref, ks_ref, o_ref,
                qt_scr, k_scr, vt_scr, s_scr, acc_scr):
    kv_heads = N_HEADS // 2
    n_tiles = SEQ // Q_TILE
    pair = 2 * HEAD_DIM
    quarter = HEAD_DIM // 4

    def normed_rotary(x, gain_cos, gain_sin):
        rotated = x * gain_cos + _swap_rotary_halves(x) * gain_sin
        return rotated * lax.rsqrt(_segment_mean_square(x, HEAD_DIM) + EPS)

    def normed_rotary_transposed(xt, gain_cos, gain_sin):
        swapped = jnp.concatenate([xt[(b ^ 1) * quarter:((b ^ 1) + 1) * quarter] for b in range(4)], axis=0)
        rotated = xt * gain_cos + swapped * gain_sin
        return rotated * lax.rsqrt(jnp.mean(xt * xt, axis=0, keepdims=True) + EPS)

    def prep(ci, carry):
        r0 = pl.multiple_of(ci * Q_TILE, Q_TILE)
        rows = pl.ds(r0, Q_TILE)
        zeros = jnp.zeros((HEAD_DIM, Q_TILE), BF16)
        for half in range(2):
            xt = q_ref[0, rows, half * pair:(half + 1) * pair].T
            for j in range(2):
                h = 2 * half + j
                kv = h // (N_HEADS // kv_heads)
                head_rows = slice(j * HEAD_DIM, (j + 1) * HEAD_DIM)
                yt = normed_rotary_transposed(xt[head_rows], qc_ref[ci, head_rows, :],
                                              qs_ref[ci, head_rows, :]).astype(BF16)
                for part in range(kv_heads):
                    block = yt if part == kv else zeros
                    qt_scr[ci, h * pair + part * HEAD_DIM:h * pair + (part + 1) * HEAD_DIM, :] = block
        k_scr[rows, :] = normed_rotary(k_ref[0, rows, :], kc_ref[rows, :], ks_ref[rows, :]).astype(BF16)
        vt = v_ref[0, rows, :].T.astype(BF16)
        for kv in range(kv_heads):
            vt_scr[kv, ci, 0:HEAD_DIM, :] = vt[kv * HEAD_DIM:(kv + 1) * HEAD_DIM]
            vt_scr[kv, ci, HEAD_DIM:, :] = _ones_row_block(Q_TILE)
        return carry

    lax.fori_loop(0, n_tiles, prep, 0, unroll=4)

    def score_chunks(qi, h):
        qt = qt_scr[qi, h * pair:(h + 1) * pair, :]
        return [_dot(k_scr[c * Q_TILE:(c + 1) * Q_TILE, :], qt) for c in range(n_tiles)]

    def value_chunk(qi, h, c):
        return vt_scr[h // (N_HEADS // kv_heads), c]

    _pipelined_softmax_pv(n_tiles, N_HEADS, n_tiles, score_chunks, value_chunk, _normalised_heads, s_scr, acc_scr,
                          o_ref)


def _gqa_attention(proj, gq, gqp, gk, gkp, cos, sin):
    b = proj.shape[0]
    tab = _const_spec((SEQ, 2 * HEAD_DIM))
    q_scale = HEAD_DIM ** -0.5 * LOG2E
    _, sign = _rot_perm(2 * HEAD_DIM, HEAD_DIM // 2)
    per_tile = lambda t: t.T.reshape(2 * HEAD_DIM, SEQ // Q_TILE, Q_TILE).transpose(1, 0, 2)
    tables = (per_tile(gq * cos * q_scale), per_tile(gqp * (sign * q_scale) * sin), gk * cos, gkp * sign * sin)
    tab_t = _const_spec((SEQ // Q_TILE, 2 * HEAD_DIM, Q_TILE))
    return pl.pallas_call(
        _gqa_kernel,
        out_shape=jax.ShapeDtypeStruct((b, SEQ, GROUP_WIDTH), BF16),
        grid=(b,),
        in_specs=[
            _proj_spec(256, B256_GQ_Q), _proj_spec(128, B128_GQ_K), _proj_spec(128, B128_GQ_V),
            tab_t, tab_t, tab, tab,
        ],
        out_specs=pl.BlockSpec((1, SEQ, GROUP_WIDTH), lambda i: (i, 0, 0)),
        scratch_shapes=[pltpu.VMEM((SEQ // Q_TILE, N_HEADS * 2 * HEAD_DIM, Q_TILE), BF16),
                        pltpu.VMEM((SEQ, 2 * HEAD_DIM), BF16),
                        pltpu.VMEM((N_HEADS // 2, SEQ // Q_TILE, HEAD_DIM + 16, Q_TILE), BF16),
                        pltpu.VMEM((2, SEQ, Q_TILE), F32),
                        pltpu.VMEM((N_HEADS, HEAD_DIM + 16, Q_TILE), F32)],
        compiler_params=_compiler_params(),
        name="gqa_attention",
    )(proj, proj, proj, *tables)


def _mla_kernel(cq_ref, ckv_ref, kr_ref, gq_ref, gkv_ref, wuq_ref, wukv_ref, cos_ref, sin_ref, o_ref,
                qt_scr, k_scr, vt_scr, s_scr, acc_scr):
    q_scale = (MLA_NOPE_DIM + MLA_ROPE_DIM) ** -0.5 * LOG2E
    n_tiles = SEQ // Q_TILE
    rot0 = N_HEADS * 128
    v0 = N_HEADS * 128

    def prep(ci, carry):
        r0 = pl.multiple_of(ci * Q_TILE, Q_TILE)
        rows = pl.ds(r0, Q_TILE)
        cos, sin = cos_ref[rows, :], sin_ref[rows, :]
        cq = cq_ref[0, rows, :]
        cqn = (cq * _rms(cq) * gq_ref[...]).astype(BF16)
        ckv = ckv_ref[0, rows, :]
        ckvn = (ckv * _rms(ckv) * gkv_ref[...]).astype(BF16)
        kr = kr_ref[0, rows, :]
        k_rope = kr * cos + pltpu.roll(kr, 128 - MLA_ROPE_DIM, 1) * sin
        two = lambda t: jnp.concatenate([t, t], axis=1)
        q_cos, q_sin, k_rope2 = two(cos * q_scale), two(sin * q_scale), two(k_rope)
        for p in range(N_HEADS // 2):
            cols = slice(p * 256, (p + 1) * 256)
            rot_cols = slice(rot0 + p * 256, rot0 + (p + 1) * 256)
            q2 = _dot(cqn, wuq_ref[:, cols]) * q_cos + _dot(cqn, wuq_ref[:, rot_cols]) * q_sin
            qt_scr[ci, p * 256:(p + 1) * 256, :] = q2.T.astype(BF16)
            k2 = (_dot(ckvn, wukv_ref[:, cols]) + k_rope2).astype(BF16)
            k_scr[2 * p, rows, :] = k2[:, :128]
            k_scr[2 * p + 1, rows, :] = k2[:, 128:]
        vt = _dot(ckvn, wukv_ref[:, v0:v0 + GROUP_WIDTH]).T.astype(BF16)
        for h in range(N_HEADS):
            vt_scr[h, ci, 0:HEAD_DIM, :] = vt[h * HEAD_DIM:(h + 1) * HEAD_DIM]
            vt_scr[h, ci, HEAD_DIM:, :] = _ones_row_block(Q_TILE)
        return carry

    lax.fori_loop(0, n_tiles, prep, 0, unroll=4)

    def score_chunks(qi, h):
        qt = qt_scr[qi, h * 128:(h + 1) * 128, :]
        return [_dot(k_scr[h, c * Q_TILE:(c + 1) * Q_TILE, :], qt) for c in range(n_tiles)]

    def value_chunk(qi, h, c):
        return vt_scr[h, c]

    _pipelined_softmax_pv(n_tiles, N_HEADS, n_tiles, score_chunks, value_chunk, _normalised_heads, s_scr, acc_scr,
                          o_ref)


def _mla_attention(proj, gq, gkv, wuq, wukv, cos, sin):
    b = proj.shape[0]
    tab = _const_spec((SEQ, 128))
    return pl.pallas_call(
        _mla_kernel,
        out_shape=jax.ShapeDtypeStruct((b, SEQ, GROUP_WIDTH), BF16),
        grid=(b,),
        in_specs=[
            _proj_spec(256, B256_ML_CQ), _proj_spec(128, B128_ML_CKV), _proj_spec(128, B128_ML_KR),
            _const_spec((1, GROUP_WIDTH)), _const_spec((1, MLA_KV_LORA)),
            _const_spec(wuq.shape), _const_spec(wukv.shape), tab, tab,
        ],
        out_specs=pl.BlockSpec((1, SEQ, GROUP_WIDTH), lambda i: (i, 0, 0)),
        scratch_shapes=[pltpu.VMEM((SEQ // Q_TILE, N_HEADS * 128, Q_TILE), BF16),
                        pltpu.VMEM((N_HEADS, SEQ, 128), BF16),
                        pltpu.VMEM((N_HEADS, SEQ // Q_TILE, HEAD_DIM + 16, Q_TILE), BF16),
                        pltpu.VMEM((2, SEQ, Q_TILE), F32),
                        pltpu.VMEM((N_HEADS, HEAD_DIM + 16, Q_TILE), F32)],
        compiler_params=_compiler_params(),
        name="mla_attention",
    )(proj, proj, proj, gq, gkv, wuq, wukv, cos, sin)


def _post_kernel(x_ref, a_ref, b_ref, c_ref, d_ref, wo_ref, g_mix_ref, g_pre_ref, wg_ref, wu_ref, wd_ref,
                 g_ffn_ref, o_ref, acc_ref):
    mix = (_dot(a_ref[...], wo_ref[0]) + _dot(b_ref[...], wo_ref[1])
           + _dot(c_ref[...], wo_ref[2]) + _dot(d_ref[...], wo_ref[3]))
    x = x_ref[...] + mix * _rms(mix) * g_mix_ref[...]
    h = (x * _rms(x) * g_pre_ref[...]).astype(BF16)
    acc_ref[...] = jnp.zeros_like(acc_ref)

    def body(ci, carry):
        gate = _dot(h, wg_ref[ci])
        up = _dot(h, wu_ref[ci])
        act = (gate * jax.nn.sigmoid(gate) * up).astype(BF16)
        acc_ref[...] += _dot(act, wd_ref[ci])
        return carry

    lax.fori_loop(0, D_FF // FF_CHUNK, body, 0, unroll=True)
    f = acc_ref[...]
    o_ref[...] = x + f * _rms(f) * g_ffn_ref[...]


def _post(x2d, a, b, c, d, wo, g_mix, g_pre, wg, wu, wd, g_ffn):
    t = x2d.shape[0]
    tok = lambda w: pl.BlockSpec((TOKEN_TILE, w), lambda i: (i, 0))
    vec = _const_spec((1, D_MODEL))
    return pl.pallas_call(
        _post_kernel,
        out_shape=jax.ShapeDtypeStruct((t, D_MODEL), F32),
        grid=(t // TOKEN_TILE,),
        in_specs=[
            tok(D_MODEL), tok(GROUP_WIDTH), tok(GROUP_WIDTH), tok(GROUP_WIDTH), tok(GROUP_WIDTH),
            _const_spec(wo.shape), vec, vec, _const_spec(wg.shape), _const_spec(wu.shape), _const_spec(wd.shape),
            vec,
        ],
        out_specs=tok(D_MODEL),
        scratch_shapes=[pltpu.VMEM((TOKEN_TILE, D_MODEL), F32)],
        compiler_params=_compiler_params(),
        name="outproj_swiglu",
    )(x2d, a, b, c, d, wo, g_mix, g_pre, wg, wu, wd, g_ffn)


def _layout_w_in(w):
    src32, sign32 = _rot_perm(MLA_ROPE_DIM, MLA_ROPE_DIM)
    k_rope = w[:, O_ML_KR:O_ML_KR + MLA_ROPE_DIM]
    cols = [
        w[:, :O_GQ_K],
        w[:, O_ML_CQ:O_ML_CKV],
        w[:, O_GQ_K:O_ML_CQ],
        w[:, O_ML_CKV:O_ML_KR],
        jnp.zeros((D_MODEL, MLA_NOPE_DIM), w.dtype), k_rope, k_rope[:, src32] * sign32,
    ]
    return jnp.concatenate(cols, axis=1).astype(BF16)


def _layout_mla_weights(w_uq, w_ukv):
    src32, sign32 = _rot_perm(MLA_ROPE_DIM, MLA_ROPE_DIM)
    wq = w_uq.reshape(GROUP_WIDTH, N_HEADS, MLA_NOPE_DIM + MLA_ROPE_DIM)
    rope = wq[:, :, MLA_NOPE_DIM:]
    tail = 128 - MLA_NOPE_DIM - MLA_ROPE_DIM
    q_main = jnp.pad(wq, ((0, 0), (0, 0), (0, tail)))
    q_rot = jnp.pad(rope[:, :, src32] * sign32, ((0, 0), (0, 0), (MLA_NOPE_DIM, tail)))
    wuq = jnp.concatenate([q_main.reshape(GROUP_WIDTH, -1), q_rot.reshape(GROUP_WIDTH, -1)], axis=1)
    wkv = w_ukv.reshape(MLA_KV_LORA, N_HEADS, MLA_NOPE_DIM + HEAD_DIM)
    k_nope = jnp.pad(wkv[:, :, :MLA_NOPE_DIM], ((0, 0), (0, 0), (0, 128 - MLA_NOPE_DIM)))
    vals = wkv[:, :, MLA_NOPE_DIM:]
    wukv = jnp.concatenate([k_nope.reshape(MLA_KV_LORA, -1), vals.reshape(MLA_KV_LORA, -1)], axis=1)
    return wuq.astype(BF16), wukv.astype(BF16)


def _rotary_tables():
    pos = jnp.arange(SEQ)
    half = HEAD_DIM // 2
    inv = ROPE_THETA ** (-jnp.arange(0, half, 2, dtype=F32) / half)

    def angles(p):
        ang = p.astype(F32)[:, None] * inv[None, :]
        return jnp.concatenate([ang, ang], axis=-1)

    axial = jnp.concatenate([angles(pos // GRID_W), angles(pos % GRID_W)], axis=-1)
    seq = angles(pos)
    tail = 128 - MLA_NOPE_DIM - MLA_ROPE_DIM
    mla_cos = jnp.concatenate([jnp.ones((SEQ, MLA_NOPE_DIM), F32), jnp.cos(seq), jnp.zeros((SEQ, tail), F32)], -1)
    mla_sin = jnp.pad(jnp.sin(seq), ((0, 0), (MLA_NOPE_DIM, tail)))
    pair = lambda t: jnp.tile(t, (1, 2))
    return pair(jnp.cos(axial)), pair(jnp.sin(axial)), mla_cos, mla_sin


def kernel(x, pre_mix_norm, w_in, na_rel_bias, diff_lambda_q1, diff_lambda_k1, diff_lambda_q2, diff_lambda_k2,
           diff_subln, gqa_q_norm, gqa_k_norm, mla_q_norm, mla_kv_norm, mla_w_uq, mla_w_ukv, w_o, post_mix_norm,
           pre_ffn_norm, ffn_w_gate_up, ffn_w_down, post_ffn_norm):
    b, s, d = x.shape
    assert (s, d) == (SEQ, D_MODEL)
    depth = w_in.shape[0]
    src64, _ = _rot_perm(HEAD_DIM, HEAD_DIM // 2)
    ax_cos, ax_sin, mla_cos, mla_sin = _rotary_tables()
    row = lambda v: v.reshape(1, -1).astype(F32)
    n_chunks = D_FF // FF_CHUNK

    x2d = x.reshape(b * s, d)
    for l in range(depth):
        lambda_init = 0.8 - 0.6 * math.exp(-0.3 * l)
        proj = _inproj(x2d, row(pre_mix_norm[l]), _layout_w_in(w_in[l])).reshape(b, s, PROJ_WIDTH)

        a_out = _na_attention(proj, _na_tables(na_rel_bias[l]))
        b_out = _diff_attention(proj, row(diff_lambda_q1[l]), row(diff_lambda_k1[l]), row(diff_lambda_q2[l]),
                                row(diff_lambda_k2[l]), row(diff_subln[l]), lambda_init)
        pair_row = lambda v: row(jnp.tile(v, 2))
        c_out = _gqa_attention(proj, pair_row(gqa_q_norm[l]), pair_row(gqa_q_norm[l][src64]),
                               pair_row(gqa_k_norm[l]), pair_row(gqa_k_norm[l][src64]), ax_cos, ax_sin)
        wuq, wukv = _layout_mla_weights(mla_w_uq[l], mla_w_ukv[l])
        d_out = _mla_attention(proj, row(mla_q_norm[l]), row(mla_kv_norm[l]), wuq, wukv, mla_cos, mla_sin)

        gate_up = ffn_w_gate_up[l].astype(BF16)
        wg = gate_up[:, :D_FF].reshape(d, n_chunks, FF_CHUNK).transpose(1, 0, 2)
        wu = gate_up[:, D_FF:].reshape(d, n_chunks, FF_CHUNK).transpose(1, 0, 2)
        wd = ffn_w_down[l].astype(BF16).reshape(n_chunks, FF_CHUNK, d)
        wo = w_o[l].astype(BF16).reshape(N_HEADS, GROUP_WIDTH, d)
        flat = lambda t: t.reshape(b * s, GROUP_WIDTH)
        x2d = _post(x2d, flat(a_out), flat(b_out), flat(c_out), flat(d_out), wo, row(post_mix_norm[l]),
                    row(pre_ffn_norm[l]), wg, wu, wd, row(post_ffn_norm[l]))
    return x2d.reshape(b, s, d)
```

```python
import functools
import math

import numpy as np
import jax
import jax.numpy as jnp
from jax import lax
from jax.experimental import pallas as pl
from jax.experimental.pallas import tpu as pltpu

F32 = jnp.float32
BF16 = jnp.bfloat16

D_MODEL = 1024
SEQ = 2048
GRID_W = 64
GRID_ROWS = SEQ // GRID_W
HEAD_DIM = 64
N_HEADS = 4
GROUP_WIDTH = 256
EPS = 1e-6
ROPE_THETA = 10000.0

NA_WIN_ROWS = 8
NA_WIN_COLS = 16
DIFF_QK_DIM = 32
POS_SPLIT = 64
MLA_NOPE_DIM = 64
MLA_ROPE_DIM = 32
MLA_KV_LORA = 128
D_FF = 2816

O_GQ_Q, O_GQ_K, O_GQ_V, O_ML_CQ, O_ML_CKV, O_ML_KR = 1536, 1792, 1920, 2048, 2304, 2432

PROJ_WIDTH = 2560
B256_NA_Q, B256_NA_K, B256_NA_V, B256_DF_Q, B256_DF_K, B256_DF_V, B256_GQ_Q, B256_ML_CQ = range(8)
B128_GQ_K, B128_GQ_V, B128_ML_CKV, B128_ML_KR = range(16, 20)

VMEM_LIMIT_BYTES = 56 * 1024 * 1024

TOKEN_TILE = 1024
FF_CHUNK = 256
Q_TILE = 256
NA_GROUP_ROWS = 4
NA_BAND_ROWS = 12
NEG_BIG = -1e30
LOG2E = math.log2(math.e)


def _rot_perm(width, group):
    j = np.arange(width)
    jj = j % group
    half = group // 2
    src = (j // group) * group + (jj + half) % group
    sign = np.where(jj < half, -1.0, 1.0).astype(np.float32)
    return src, sign


def _compiler_params():
    return pltpu.CompilerParams(dimension_semantics=("arbitrary",), vmem_limit_bytes=VMEM_LIMIT_BYTES)


def _const_spec(shape):
    zeros = (0,) * len(shape)
    return pl.BlockSpec(shape, lambda i: zeros, pipeline_mode=pl.Buffered(1))


def _rms(x):
    return lax.rsqrt(jnp.mean(x * x, axis=-1, keepdims=True) + EPS)


def _dot(a, b):
    return jnp.dot(a, b, preferred_element_type=F32)


def _inproj_kernel(x_ref, g_ref, w_ref, o_ref):
    x = x_ref[...]
    h = (x * _rms(x) * g_ref[...]).astype(BF16)
    o_ref[...] = _dot(h, w_ref[...])


def _inproj(x2d, gain, w):
    t = x2d.shape[0]
    return pl.pallas_call(
        _inproj_kernel,
        out_shape=jax.ShapeDtypeStruct((t, PROJ_WIDTH), F32),
        grid=(t // TOKEN_TILE,),
        in_specs=[
            pl.BlockSpec((TOKEN_TILE, D_MODEL), lambda i: (i, 0)),
            _const_spec((1, D_MODEL)),
            _const_spec((D_MODEL, PROJ_WIDTH)),
        ],
        out_specs=pl.BlockSpec((TOKEN_TILE, PROJ_WIDTH), lambda i: (i, 0)),
        compiler_params=_compiler_params(),
        name="inproj",
    )(x2d, gain, w)


def _segment_mean_square(x, seg):
    w = x.shape[-1]
    same = (lax.broadcasted_iota(jnp.int32, (w, w), 0) // seg) == (lax.broadcasted_iota(jnp.int32, (w, w), 1) // seg)
    ones = jnp.where(same, 1.0, 0.0).astype(BF16)
    sq = x * x
    hi = sq.astype(BF16)
    lo = (sq - hi.astype(F32)).astype(BF16)
    return (_dot(hi, ones) + _dot(lo, ones)) * (1.0 / seg)


def _ones_row_block(width):
    return jnp.where(lax.broadcasted_iota(jnp.int32, (16, width), 0) == 0, 1.0, 0.0).astype(BF16)


def _normalised_heads(accs):
    return jnp.concatenate([acc[:HEAD_DIM] / acc[HEAD_DIM:HEAD_DIM + 1] for acc in accs], axis=0)


def _pipelined_softmax_pv(n_tiles, n_heads, n_chunks, score_chunks, value_chunk, combine, s_scr, acc_scr, o_ref,
                          exp_fn=jnp.exp2):
    assert n_heads % 2 == 0
    rows = s_scr.shape[1] // n_chunks

    def scores_into(slot, qi, h):
        m8 = None
        for c, st in enumerate(score_chunks(qi, h)):
            s_scr[slot, c * rows:(c + 1) * rows, :] = st
            cm = jnp.max(st.reshape(-1, 8, st.shape[-1]), axis=0)
            m8 = cm if m8 is None else jnp.maximum(m8, cm)
        return jnp.max(m8, axis=0, keepdims=True)

    def write_tile(qi):
        r0 = qi * Q_TILE if isinstance(qi, int) else pl.multiple_of(qi * Q_TILE, Q_TILE)
        o_ref[0, pl.ds(r0, Q_TILE), :] = combine([acc_scr[h] for h in range(n_heads)]).T.astype(o_ref.dtype)

    acc_scr[...] = jnp.ones_like(acc_scr)

    def body(qi, m):
        next_qi = jnp.minimum(qi + 1, n_tiles - 1)
        for h in range(n_heads):
            slot = h % 2
            nq, nh = (qi, h + 1) if h + 1 < n_heads else (next_qi, 0)
            next_m = scores_into(1 - slot, nq, nh)
            if h == 0:
                write_tile(jnp.maximum(qi - 1, 0))
            acc = None
            for c in range(n_chunks):
                p = exp_fn(s_scr[slot, c * rows:(c + 1) * rows, :] - m).astype(BF16)
                part = _dot(value_chunk(qi, h, c), p)
                acc = part if acc is None else acc + part
            acc_scr[h] = acc
            m = next_m
        return m

    lax.fori_loop(0, n_tiles, body, scores_into(0, 0, 0), unroll=2)
    write_tile(n_tiles - 1)


def _proj_spec(width, block):
    return pl.BlockSpec((1, SEQ, width), lambda b: (b, 0, block))


def _na_kernel(q_ref, k_ref, v_ref, tab_ref, o_ref, qt_scr, k_scr, vt_scr, s_scr, acc_scr):
    assert NA_GROUP_ROWS * GRID_W == Q_TILE and NA_BAND_ROWS % NA_GROUP_ROWS == 0
    n_tiles = SEQ // Q_TILE
    band_chunks = NA_BAND_ROWS // NA_GROUP_ROWS
    pair = 2 * HEAD_DIM
    q_scale = HEAD_DIM ** -0.5 * LOG2E

    def prep(ci, carry):
        r0 = pl.multiple_of(ci * Q_TILE, Q_TILE)
        rows = pl.ds(r0, Q_TILE)
        qt = (q_ref[0, rows, :] * q_scale).T
        row = lax.broadcasted_iota(jnp.int32, (pair, Q_TILE), 0)
        for h in range(N_HEADS):
            own = (row >= (h % 2) * HEAD_DIM) & (row < (h % 2 + 1) * HEAD_DIM)
            qt_scr[ci, h] = jnp.where(own, qt[(h // 2) * pair:(h // 2 + 1) * pair], 0.0).astype(BF16)
        for p in range(N_HEADS // 2):
            k_scr[p, rows, :] = k_ref[0, rows, p * pair:(p + 1) * pair].astype(BF16)
        vt = v_ref[0, rows, :].T.astype(BF16)
        for h in range(N_HEADS):
            vt_scr[h, ci, 0:HEAD_DIM, :] = vt[h * HEAD_DIM:(h + 1) * HEAD_DIM]
            vt_scr[h, ci, HEAD_DIM:, :] = _ones_row_block(Q_TILE)
        return carry

    lax.fori_loop(0, n_tiles, prep, 0, unroll=4)

    def key_chunk(gi, c):
        return jnp.clip(gi - (NA_WIN_ROWS // 2) // NA_GROUP_ROWS, 0, n_tiles - band_chunks) + c

    def score_chunks(gi, h):
        kind = jnp.where(gi == 0, 0, jnp.where(gi == n_tiles - 1, 2, 1))
        k0 = pl.multiple_of(key_chunk(gi, 0) * Q_TILE, Q_TILE)
        return [_dot(k_scr[h // 2, pl.ds(k0 + c * Q_TILE, Q_TILE), :], qt_scr[gi, h])
                + tab_ref[kind, h, c * Q_TILE:(c + 1) * Q_TILE, :] for c in range(band_chunks)]

    def value_chunk(gi, h, c):
        return vt_scr[h, key_chunk(gi, c)]

    _pipelined_softmax_pv(n_tiles, N_HEADS, band_chunks, score_chunks, value_chunk, _normalised_heads, s_scr, acc_scr,
                          o_ref)


def _na_tables(rel_bias):
    a = np.arange(NA_GROUP_ROWS)[:, None, None, None]
    c = np.arange(GRID_W)[None, :, None, None]
    i = np.arange(NA_BAND_ROWS)[None, None, :, None]
    kc = np.arange(GRID_W)[None, None, None, :]
    cs = np.clip(c - NA_WIN_COLS // 2, 0, GRID_W - NA_WIN_COLS)
    col_ok = (kc >= cs) & (kc < cs + NA_WIN_COLS)
    dc = kc - c + NA_WIN_COLS - 1
    col_sel = (dc[..., None] == np.arange(2 * NA_WIN_COLS - 1)) & col_ok[..., None]
    row_sels = []
    last_r0 = GRID_ROWS - NA_GROUP_ROWS
    for r0, band_row in ((0, 0), (NA_GROUP_ROWS, 0), (last_r0, GRID_ROWS - NA_BAND_ROWS)):
        r = r0 + a
        rs = np.clip(r - NA_WIN_ROWS // 2, 0, GRID_ROWS - NA_WIN_ROWS)
        key_row = band_row + i
        row_ok = (key_row >= rs) & (key_row < rs + NA_WIN_ROWS)
        dr = key_row - r + NA_WIN_ROWS - 1
        row_sels.append((dr[..., None] == np.arange(2 * NA_WIN_ROWS - 1)) & row_ok[..., None])
    row_sel = np.stack(row_sels)[:, :, 0, :, 0, :].astype(np.float32)
    col_sel = col_sel[0, :, 0, :, :].astype(np.float32)
    vals = jnp.einsum("taiu,huv,ckv->thikac", row_sel, rel_bias.astype(F32), col_sel,
                      precision=lax.Precision.HIGHEST)
    inside = np.einsum("taiu,ckv->tikac", row_sel, col_sel) > 0
    tab = jnp.where(inside[:, None], vals * LOG2E, NEG_BIG)
    q_rows, band = NA_GROUP_ROWS * GRID_W, NA_BAND_ROWS * GRID_W
    return tab.reshape(3, N_HEADS, band, q_rows)


def _na_attention(proj, tables):
    b = proj.shape[0]
    return pl.pallas_call(
        _na_kernel,
        out_shape=jax.ShapeDtypeStruct((b, SEQ, GROUP_WIDTH), BF16),
        grid=(b,),
        in_specs=[
            _proj_spec(256, B256_NA_Q), _proj_spec(256, B256_NA_K), _proj_spec(256, B256_NA_V),
            _const_spec(tables.shape),
        ],
        out_specs=pl.BlockSpec((1, SEQ, GROUP_WIDTH), lambda i: (i, 0, 0)),
        scratch_shapes=[pltpu.VMEM((SEQ // Q_TILE, N_HEADS, 2 * HEAD_DIM, Q_TILE), BF16),
                        pltpu.VMEM((N_HEADS // 2, SEQ, 2 * HEAD_DIM), BF16),
                        pltpu.VMEM((N_HEADS, SEQ // Q_TILE, HEAD_DIM + 16, Q_TILE), BF16),
                        pltpu.VMEM((2, NA_BAND_ROWS * GRID_W, Q_TILE), F32),
                        pltpu.VMEM((N_HEADS, HEAD_DIM + 16, Q_TILE), F32)],
        compiler_params=_compiler_params(),
        name="na_attention",
    )(proj, proj, proj, tables)


def _alibi_slope(h):
    return 2.0 ** (-8.0 * (h + 1) / N_HEADS)


def _diff_kernel(lambda_init, q_ref, k_ref, v_ref, lq1_ref, lk1_ref, lq2_ref, lk2_ref, subln_ref, diag_ref, o_ref,
                 qt_scr, qf_scr, k_scr, vt_scr, s_scr, acc_scr):
    scale = DIFF_QK_DIM ** -0.5
    n_tiles = SEQ // Q_TILE
    pair = 2 * HEAD_DIM
    n_feat = 16
    lam = (jnp.exp(jnp.sum(lq1_ref[...] * lk1_ref[...], axis=-1, keepdims=True))
           - jnp.exp(jnp.sum(lq2_ref[...] * lk2_ref[...], axis=-1, keepdims=True)) + lambda_init)

    def prep(ci, carry):
        r0 = pl.multiple_of(ci * Q_TILE, Q_TILE)
        rows = pl.ds(r0, Q_TILE)
        qt = (q_ref[0, rows, :] * scale).T
        row = lax.broadcasted_iota(jnp.int32, (pair, Q_TILE), 0)
        feat_row = lax.broadcasted_iota(jnp.int32, (n_feat, Q_TILE), 0)
        i = r0 + lax.broadcasted_iota(jnp.int32, (n_feat, Q_TILE), 1)
        i_lo = i % POS_SPLIT
        i_hi = i - i_lo
        for h in range(N_HEADS):
            group = qt[(h // 2) * pair:(h // 2 + 1) * pair]
            for mp in range(2):
                lo = (h % 2) * HEAD_DIM + mp * DIFF_QK_DIM
                keep = (row >= lo) & (row < lo + DIFF_QK_DIM)
                qt_scr[ci, 2 * h + mp] = jnp.where(keep, group, 0.0).astype(BF16)
            slope = _alibi_slope(h)
            feat = jnp.where(feat_row < 2, slope,
                             jnp.where(feat_row == 2, -slope * i_hi.astype(F32),
                                       jnp.where(feat_row == 3, -slope * i_lo.astype(F32), 0.0)))
            qf_scr[ci, h, 0:n_feat, :] = feat.astype(BF16)
            qf_scr[ci, h, n_feat:, :] = (-feat).astype(BF16)
        lane = lax.broadcasted_iota(jnp.int32, (Q_TILE, pair), 1)
        j = r0 + lax.broadcasted_iota(jnp.int32, (Q_TILE, pair), 0)
        j_lo = j % POS_SPLIT
        k_feat = jnp.where(lane == 0, (j - j_lo).astype(F32),
                           jnp.where(lane == 1, j_lo.astype(F32), jnp.where(lane < 4, 1.0, 0.0))).astype(BF16)
        for p in range(N_HEADS // 2):
            k_scr[p, rows, 0:pair] = k_ref[0, rows, p * pair:(p + 1) * pair].astype(BF16)
            k_scr[p, rows, pair:] = k_feat
        vt = v_ref[0, rows, :].T.astype(BF16)
        for h in range(N_HEADS):
            vt_scr[h, ci, 0:HEAD_DIM, :] = vt[h * HEAD_DIM:(h + 1) * HEAD_DIM]
            vt_scr[h, ci, HEAD_DIM:, :] = _ones_row_block(Q_TILE)
        return carry

    lax.fori_loop(0, n_tiles, prep, 0, unroll=4)

    def key_chunk(qi, c):
        wrapped = qi + c >= n_tiles
        return jnp.where(wrapped, qi + c - n_tiles, qi + c), wrapped

    def score_chunk(qi, item, c):
        h = item // 2
        kc, left = key_chunk(qi, c)
        form = 0 if c == 0 else pl.multiple_of(jnp.where(left, 0, n_feat), n_feat)
        feat = qf_scr[qi, h, pl.ds(form, n_feat), :]
        rhs = jnp.concatenate([qt_scr[qi, item], feat, jnp.zeros((pair - n_feat, Q_TILE), BF16)], axis=0)
        st = _dot(k_scr[h // 2, pl.ds(pl.multiple_of(kc * Q_TILE, Q_TILE), Q_TILE), :], rhs)
        return st + diag_ref[h] if c == 0 else st

    def score_chunks(qi, item):
        return [score_chunk(qi, item, c) for c in range(n_tiles)]

    def value_chunk(qi, item, c):
        return vt_scr[item // 2, key_chunk(qi, c)[0]]

    def combine(accs):
        outs = []
        for h in range(N_HEADS):
            a1, a2 = accs[2 * h], accs[2 * h + 1]
            o = a1[:HEAD_DIM] / a1[HEAD_DIM:HEAD_DIM + 1] - lam * (a2[:HEAD_DIM] / a2[HEAD_DIM:HEAD_DIM + 1])
            r = lax.rsqrt(jnp.mean(o * o, axis=0, keepdims=True) + EPS)
            outs.append(o * r * subln_ref[...] * (1.0 - lambda_init))
        return jnp.concatenate(outs, axis=0)

    _pipelined_softmax_pv(n_tiles, 2 * N_HEADS, n_tiles, score_chunks, value_chunk, combine, s_scr, acc_scr, o_ref,
                          exp_fn=jnp.exp)


def _diff_attention(proj, lq1, lk1, lq2, lk2, subln, lambda_init):
    b = proj.shape[0]
    vec = _const_spec((1, DIFF_QK_DIM))
    n_tiles = SEQ // Q_TILE
    rel = np.arange(Q_TILE)
    over = -2.0 * np.maximum(rel[:, None] - rel[None, :], 0).astype(np.float32)
    diag = jnp.asarray(np.stack([_alibi_slope(h) * over for h in range(N_HEADS)]))
    subln_cols = jnp.broadcast_to(subln.reshape(HEAD_DIM, 1), (HEAD_DIM, Q_TILE))
    return pl.pallas_call(
        functools.partial(_diff_kernel, lambda_init),
        out_shape=jax.ShapeDtypeStruct((b, SEQ, GROUP_WIDTH), BF16),
        grid=(b,),
        in_specs=[
            _proj_spec(256, B256_DF_Q), _proj_spec(256, B256_DF_K), _proj_spec(256, B256_DF_V),
            vec, vec, vec, vec, _const_spec((HEAD_DIM, Q_TILE)), _const_spec(diag.shape),
        ],
        out_specs=pl.BlockSpec((1, SEQ, GROUP_WIDTH), lambda i: (i, 0, 0)),
        scratch_shapes=[pltpu.VMEM((n_tiles, 2 * N_HEADS, 2 * HEAD_DIM, Q_TILE), BF16),
                        pltpu.VMEM((n_tiles, N_HEADS, 32, Q_TILE), BF16),
                        pltpu.VMEM((N_HEADS // 2, SEQ, 4 * HEAD_DIM), BF16),
                        pltpu.VMEM((N_HEADS, n_tiles, HEAD_DIM + 16, Q_TILE), BF16),
                        pltpu.VMEM((2, SEQ, Q_TILE), F32),
                        pltpu.VMEM((2 * N_HEADS, HEAD_DIM + 16, Q_TILE), F32)],
        compiler_params=_compiler_params(),
        name="diff_attention",
    )(proj, proj, proj, lq1, lk1, lq2, lk2, subln_cols, diag)


def _swap_rotary_halves(x):
    quarter = HEAD_DIM // 4
    first = (lax.broadcasted_iota(jnp.int32, x.shape, 1) % (2 * quarter)) < quarter
    lanes = x.shape[1]
    return jnp.where(first, pltpu.roll(x, lanes - quarter, 1), pltpu.roll(x, quarter, 1))


def _gqa_kernel(q_ref, k_ref, v_ref, qc_ref, qs_ref, kc_ref, ks_ref, o_ref,
                qt_scr, k_scr, vt_scr, s_scr, acc_scr):
    kv_heads = N_HEADS // 2
    n_tiles = SEQ // Q_TILE
    pair = 2 * HEAD_DIM
    quarter = HEAD_DIM // 4

    def normed_rotary(x, gain_cos, gain_sin):
        rotated = x * gain_cos + _swap_rotary_halves(x) * gain_sin
        return rotated * lax.rsqrt(_segment_mean_square(x, HEAD_DIM) + EPS)

    def normed_rotary_transposed(xt, gain_cos, gain_sin):
        swapped = jnp.concatenate([xt[(b ^ 1) * quarter:((b ^ 1) + 1) * quarter] for b in range(4)], axis=0)
        rotated = xt * gain_cos + swapped * gain_sin
        return rotated * lax.rsqrt(jnp.mean(xt * xt, axis=0, keepdims=True) + EPS)

    def prep(ci, carry):
        r0 = pl.multiple_of(ci * Q_TILE, Q_TILE)
        rows = pl.ds(r0, Q_TILE)
        zeros = jnp.zeros((HEAD_DIM, Q_TILE), BF16)
        for half in range(2):
            xt = q_ref[0, rows, half * pair:(half + 1) * pair].T
            for j in range(2):
                h = 2 * half + j
                kv = h // (N_HEADS // kv_heads)
                head_rows = slice(j * HEAD_DIM, (j + 1) * HEAD_DIM)
                yt = normed_rotary_transposed(xt[head_rows], qc_ref[ci, head_rows, :],
                                              qs_ref[ci, head_rows, :]).astype(BF16)
                for part in range(kv_heads):
                    block = yt if part == kv else zeros
                    qt_scr[ci, h * pair + part * HEAD_DIM:h * pair + (part + 1) * HEAD_DIM, :] = block
        k_scr[rows, :] = normed_rotary(k_ref[0, rows, :], kc_ref[rows, :], ks_ref[rows, :]).astype(BF16)
        vt = v_ref[0, rows, :].T.astype(BF16)
        for kv in range(kv_heads):
            vt_scr[kv, ci, 0:HEAD_DIM, :] = vt[kv * HEAD_DIM:(kv + 1) * HEAD_DIM]
            vt_scr[kv, ci, HEAD_DIM:, :] = _ones_row_block(Q_TILE)
        return carry

    lax.fori_loop(0, n_tiles, prep, 0, unroll=4)

    def score_chunks(qi, h):
        qt = qt_scr[qi, h * pair:(h + 1) * pair, :]
        return [_dot(k_scr[c * Q_TILE:(c + 1) * Q_TILE, :], qt) for c in range(n_tiles)]

    def value_chunk(qi, h, c):
        return vt_scr[h // (N_HEADS // kv_heads), c]

    _pipelined_softmax_pv(n_tiles, N_HEADS, n_tiles, score_chunks, value_chunk, _normalised_heads, s_scr, acc_scr,
                          o_ref)


def _gqa_attention(proj, gq, gqp, gk, gkp, cos, sin):
    b = proj.shape[0]
    tab = _const_spec((SEQ, 2 * HEAD_DIM))
    q_scale = HEAD_DIM ** -0.5 * LOG2E
    _, sign = _rot_perm(2 * HEAD_DIM, HEAD_DIM // 2)
    per_tile = lambda t: t.T.reshape(2 * HEAD_DIM, SEQ // Q_TILE, Q_TILE).transpose(1, 0, 2)
    tables = (per_tile(gq * cos * q_scale), per_tile(gqp * (sign * q_scale) * sin), gk * cos, gkp * sign * sin)
    tab_t = _const_spec((SEQ // Q_TILE, 2 * HEAD_DIM, Q_TILE))
    return pl.pallas_call(
        _gqa_kernel,
        out_shape=jax.ShapeDtypeStruct((b, SEQ, GROUP_WIDTH), BF16),
        grid=(b,),
        in_specs=[
            _proj_spec(256, B256_GQ_Q), _proj_spec(128, B128_GQ_K), _proj_spec(128, B128_GQ_V),
            tab_t, tab_t, tab, tab,
        ],
        out_specs=pl.BlockSpec((1, SEQ, GROUP_WIDTH), lambda i: (i, 0, 0)),
        scratch_shapes=[pltpu.VMEM((SEQ // Q_TILE, N_HEADS * 2 * HEAD_DIM, Q_TILE), BF16),
                        pltpu.VMEM((SEQ, 2 * HEAD_DIM), BF16),
                        pltpu.VMEM((N_HEADS // 2, SEQ // Q_TILE, HEAD_DIM + 16, Q_TILE), BF16),
                        pltpu.VMEM((2, SEQ, Q_TILE), F32),
                        pltpu.VMEM((N_HEADS, HEAD_DIM + 16, Q_TILE), F32)],
        compiler_params=_compiler_params(),
        name="gqa_attention",
    )(proj, proj, proj, *tables)


def _mla_kernel(cq_ref, ckv_ref, kr_ref, gq_ref, gkv_ref, wuq_ref, wukv_ref, wv_ref, cos_ref, sin_ref, qc_ref, qs_ref,
                o_ref, qt_scr, k_scr, vt_scr, s_scr, acc_scr):
    q_scale = (MLA_NOPE_DIM + MLA_ROPE_DIM) ** -0.5 * LOG2E
    n_tiles = SEQ // Q_TILE
    half_rope = MLA_ROPE_DIM // 2
    rope_end = MLA_NOPE_DIM + MLA_ROPE_DIM

    def prep(ci, carry):
        r0 = pl.multiple_of(ci * Q_TILE, Q_TILE)
        rows = pl.ds(r0, Q_TILE)
        cos, sin = cos_ref[rows, :], sin_ref[rows, :]
        cq = cq_ref[0, rows, :]
        cqn_t = (cq * _rms(cq) * gq_ref[...]).T.astype(BF16)
        ckv = ckv_ref[0, rows, :]
        ckvn_f32 = ckv * _rms(ckv) * gkv_ref[...]
        ckvn, ckvn_t = ckvn_f32.astype(BF16), ckvn_f32.T.astype(BF16)
        kr = kr_ref[0, rows, :]
        k_rope = kr * cos + pltpu.roll(kr, 128 - MLA_ROPE_DIM, 1) * sin
        k_rope2 = jnp.concatenate([k_rope, k_rope], axis=1)
        q_cos, q_sin = qc_ref[ci], qs_ref[ci]
        for p in range(N_HEADS // 2):
            cols = slice(p * 256, (p + 1) * 256)
            qt = _dot(wuq_ref[cols, :], cqn_t)
            for j in range(2):
                base = (2 * p + j) * 128
                rope = qt[j * 128 + MLA_NOPE_DIM:j * 128 + rope_end]
                swapped = jnp.concatenate([rope[half_rope:], rope[:half_rope]], axis=0)
                qt_scr[ci, base:base + MLA_NOPE_DIM, :] = (qt[j * 128:j * 128 + MLA_NOPE_DIM] * q_scale).astype(BF16)
                qt_scr[ci, base + MLA_NOPE_DIM:base + rope_end, :] = (rope * q_cos + swapped * q_sin).astype(BF16)
                qt_scr[ci, base + rope_end:base + 128, :] = jnp.zeros((128 - rope_end, Q_TILE), BF16)
            k2 = (_dot(ckvn, wukv_ref[:, cols]) + k_rope2).astype(BF16)
            k_scr[2 * p, rows, :] = k2[:, :128]
            k_scr[2 * p + 1, rows, :] = k2[:, 128:]
        vt = _dot(wv_ref[...], ckvn_t).astype(BF16)
        for h in range(N_HEADS):
            vt_scr[h, ci, 0:HEAD_DIM, :] = vt[h * HEAD_DIM:(h + 1) * HEAD_DIM]
            vt_scr[h, ci, HEAD_DIM:, :] = _ones_row_block(Q_TILE)
        return carry

    lax.fori_loop(0, n_tiles, prep, 0, unroll=4)

    def score_chunks(qi, h):
        qt = qt_scr[qi, h * 128:(h + 1) * 128, :]
        return [_dot(k_scr[h, c * Q_TILE:(c + 1) * Q_TILE, :], qt) for c in range(n_tiles)]

    def value_chunk(qi, h, c):
        return vt_scr[h, c]

    _pipelined_softmax_pv(n_tiles, N_HEADS, n_tiles, score_chunks, value_chunk, _normalised_heads, s_scr, acc_scr,
                          o_ref)


def _mla_attention(proj, gq, gkv, weights, cos, sin):
    b = proj.shape[0]
    tab = _const_spec((SEQ, 128))
    n_tiles = SEQ // Q_TILE
    q_scale = (MLA_NOPE_DIM + MLA_ROPE_DIM) ** -0.5 * LOG2E
    _, sign = _rot_perm(MLA_ROPE_DIM, MLA_ROPE_DIM)
    rope_lanes = slice(MLA_NOPE_DIM, MLA_NOPE_DIM + MLA_ROPE_DIM)
    per_tile = lambda t: t.T.reshape(MLA_ROPE_DIM, n_tiles, Q_TILE).transpose(1, 0, 2)
    q_cos = per_tile(cos[:, rope_lanes] * q_scale)
    q_sin = per_tile(sin[:, rope_lanes] * (sign * q_scale))
    tab_t = _const_spec((n_tiles, MLA_ROPE_DIM, Q_TILE))
    return pl.pallas_call(
        _mla_kernel,
        out_shape=jax.ShapeDtypeStruct((b, SEQ, GROUP_WIDTH), BF16),
        grid=(b,),
        in_specs=[
            _proj_spec(256, B256_ML_CQ), _proj_spec(128, B128_ML_CKV), _proj_spec(128, B128_ML_KR),
            _const_spec((1, GROUP_WIDTH)), _const_spec((1, MLA_KV_LORA)),
            *[_const_spec(w.shape) for w in weights], tab, tab, tab_t, tab_t,
        ],
        out_specs=pl.BlockSpec((1, SEQ, GROUP_WIDTH), lambda i: (i, 0, 0)),
        scratch_shapes=[pltpu.VMEM((SEQ // Q_TILE, N_HEADS * 128, Q_TILE), BF16),
                        pltpu.VMEM((N_HEADS, SEQ, 128), BF16),
                        pltpu.VMEM((N_HEADS, SEQ // Q_TILE, HEAD_DIM + 16, Q_TILE), BF16),
                        pltpu.VMEM((2, SEQ, Q_TILE), F32),
                        pltpu.VMEM((N_HEADS, HEAD_DIM + 16, Q_TILE), F32)],
        compiler_params=_compiler_params(),
        name="mla_attention",
    )(proj, proj, proj, gq, gkv, *weights, cos, sin, q_cos, q_sin)


def _post_kernel(x_ref, a_ref, b_ref, c_ref, d_ref, wo_ref, g_mix_ref, g_pre_ref, wg_ref, wu_ref, wd_ref,
                 g_ffn_ref, o_ref, acc_ref):
    mix = (_dot(a_ref[...], wo_ref[0]) + _dot(b_ref[...], wo_ref[1])
           + _dot(c_ref[...], wo_ref[2]) + _dot(d_ref[...], wo_ref[3]))
    x = x_ref[...] + mix * _rms(mix) * g_mix_ref[...]
    h = (x * _rms(x) * g_pre_ref[...]).astype(BF16)
    acc_ref[...] = jnp.zeros_like(acc_ref)

    def body(ci, carry):
        gate = _dot(h, wg_ref[ci])
        up = _dot(h, wu_ref[ci])
        act = (gate * jax.nn.sigmoid(gate) * up).astype(BF16)
        acc_ref[...] += _dot(act, wd_ref[ci])
        return carry

    lax.fori_loop(0, D_FF // FF_CHUNK, body, 0, unroll=True)
    f = acc_ref[...]
    o_ref[...] = x + f * _rms(f) * g_ffn_ref[...]


def _post(x2d, a, b, c, d, wo, g_mix, g_pre, wg, wu, wd, g_ffn):
    t = x2d.shape[0]
    tok = lambda w: pl.BlockSpec((TOKEN_TILE, w), lambda i: (i, 0))
    vec = _const_spec((1, D_MODEL))
    return pl.pallas_call(
        _post_kernel,
        out_shape=jax.ShapeDtypeStruct((t, D_MODEL), F32),
        grid=(t // TOKEN_TILE,),
        in_specs=[
            tok(D_MODEL), tok(GROUP_WIDTH), tok(GROUP_WIDTH), tok(GROUP_WIDTH), tok(GROUP_WIDTH),
            _const_spec(wo.shape), vec, vec, _const_spec(wg.shape), _const_spec(wu.shape), _const_spec(wd.shape),
            vec,
        ],
        out_specs=tok(D_MODEL),
        scratch_shapes=[pltpu.VMEM((TOKEN_TILE, D_MODEL), F32)],
        compiler_params=_compiler_params(),
        name="outproj_swiglu",
    )(x2d, a, b, c, d, wo, g_mix, g_pre, wg, wu, wd, g_ffn)


def _layout_w_in(w):
    src32, sign32 = _rot_perm(MLA_ROPE_DIM, MLA_ROPE_DIM)
    k_rope = w[:, O_ML_KR:O_ML_KR + MLA_ROPE_DIM]
    cols = [
        w[:, :O_GQ_K],
        w[:, O_ML_CQ:O_ML_CKV],
        w[:, O_GQ_K:O_ML_CQ],
        w[:, O_ML_CKV:O_ML_KR],
        jnp.zeros((D_MODEL, MLA_NOPE_DIM), w.dtype), k_rope, k_rope[:, src32] * sign32,
    ]
    return jnp.concatenate(cols, axis=1).astype(BF16)


def _layout_mla_weights(w_uq, w_ukv):
    wq = w_uq.reshape(GROUP_WIDTH, N_HEADS, MLA_NOPE_DIM + MLA_ROPE_DIM)
    tail = 128 - MLA_NOPE_DIM - MLA_ROPE_DIM
    wuq_t = jnp.pad(wq, ((0, 0), (0, 0), (0, tail))).reshape(GROUP_WIDTH, -1).T
    wkv = w_ukv.reshape(MLA_KV_LORA, N_HEADS, MLA_NOPE_DIM + HEAD_DIM)
    w_k = jnp.pad(wkv[:, :, :MLA_NOPE_DIM], ((0, 0), (0, 0), (0, 128 - MLA_NOPE_DIM))).reshape(MLA_KV_LORA, -1)
    w_v_t = wkv[:, :, MLA_NOPE_DIM:].reshape(MLA_KV_LORA, -1).T
    return wuq_t.astype(BF16), w_k.astype(BF16), w_v_t.astype(BF16)


def _rotary_tables():
    pos = jnp.arange(SEQ)
    half = HEAD_DIM // 2
    inv = ROPE_THETA ** (-jnp.arange(0, half, 2, dtype=F32) / half)

    def angles(p):
        ang = p.astype(F32)[:, None] * inv[None, :]
        return jnp.concatenate([ang, ang], axis=-1)

    axial = jnp.concatenate([angles(pos // GRID_W), angles(pos % GRID_W)], axis=-1)
    seq = angles(pos)
    tail = 128 - MLA_NOPE_DIM - MLA_ROPE_DIM
    mla_cos = jnp.concatenate([jnp.ones((SEQ, MLA_NOPE_DIM), F32), jnp.cos(seq), jnp.zeros((SEQ, tail), F32)], -1)
    mla_sin = jnp.pad(jnp.sin(seq), ((0, 0), (MLA_NOPE_DIM, tail)))
    pair = lambda t: jnp.tile(t, (1, 2))
    return pair(jnp.cos(axial)), pair(jnp.sin(axial)), mla_cos, mla_sin


def kernel(x, pre_mix_norm, w_in, na_rel_bias, diff_lambda_q1, diff_lambda_k1, diff_lambda_q2, diff_lambda_k2,
           diff_subln, gqa_q_norm, gqa_k_norm, mla_q_norm, mla_kv_norm, mla_w_uq, mla_w_ukv, w_o, post_mix_norm,
           pre_ffn_norm, ffn_w_gate_up, ffn_w_down, post_ffn_norm):
    b, s, d = x.shape
    assert (s, d) == (SEQ, D_MODEL)
    depth = w_in.shape[0]
    src64, _ = _rot_perm(HEAD_DIM, HEAD_DIM // 2)
    ax_cos, ax_sin, mla_cos, mla_sin = _rotary_tables()
    row = lambda v: v.reshape(1, -1).astype(F32)
    n_chunks = D_FF // FF_CHUNK

    x2d = x.reshape(b * s, d)
    for l in range(depth):
        lambda_init = 0.8 - 0.6 * math.exp(-0.3 * l)
        proj = _inproj(x2d, row(pre_mix_norm[l]), _layout_w_in(w_in[l])).reshape(b, s, PROJ_WIDTH)

        a_out = _na_attention(proj, _na_tables(na_rel_bias[l]))
        b_out = _diff_attention(proj, row(diff_lambda_q1[l]), row(diff_lambda_k1[l]), row(diff_lambda_q2[l]),
                                row(diff_lambda_k2[l]), row(diff_subln[l]), lambda_init)
        pair_row = lambda v: row(jnp.tile(v, 2))
        c_out = _gqa_attention(proj, pair_row(gqa_q_norm[l]), pair_row(gqa_q_norm[l][src64]),
                               pair_row(gqa_k_norm[l]), pair_row(gqa_k_norm[l][src64]), ax_cos, ax_sin)
        mla_weights = _layout_mla_weights(mla_w_uq[l], mla_w_ukv[l])
        d_out = _mla_attention(proj, row(mla_q_norm[l]), row(mla_kv_norm[l]), mla_weights, mla_cos, mla_sin)

        gate_up = ffn_w_gate_up[l].astype(BF16)
        wg = gate_up[:, :D_FF].reshape(d, n_chunks, FF_CHUNK).transpose(1, 0, 2)
        wu = gate_up[:, D_FF:].reshape(d, n_chunks, FF_CHUNK).transpose(1, 0, 2)
        wd = ffn_w_down[l].astype(BF16).reshape(n_chunks, FF_CHUNK, d)
        wo = w_o[l].astype(BF16).reshape(N_HEADS, GROUP_WIDTH, d)
        flat = lambda t: t.reshape(b * s, GROUP_WIDTH)
        x2d = _post(x2d, flat(a_out), flat(b_out), flat(c_out), flat(d_out), wo, row(post_mix_norm[l]),
                    row(pre_ffn_norm[l]), wg, wu, wd, row(post_ffn_norm[l]))
    return x2d.reshape(b, s, d)
```

```python
import functools
import math

import numpy as np
import jax
import jax.numpy as jnp
from jax import lax
from jax.experimental import pallas as pl
from jax.experimental.pallas import tpu as pltpu

F32 = jnp.float32
BF16 = jnp.bfloat16

D_MODEL = 1024
SEQ = 2048
GRID_W = 64
GRID_ROWS = SEQ // GRID_W
HEAD_DIM = 64
N_HEADS = 4
GROUP_WIDTH = 256
EPS = 1e-6
ROPE_THETA = 10000.0

NA_WIN_ROWS = 8
NA_WIN_COLS = 16
DIFF_QK_DIM = 32
POS_SPLIT = 64
MLA_NOPE_DIM = 64
MLA_ROPE_DIM = 32
MLA_KV_LORA = 128
D_FF = 2816

O_GQ_Q, O_GQ_K, O_GQ_V, O_ML_CQ, O_ML_CKV, O_ML_KR = 1536, 1792, 1920, 2048, 2304, 2432

PROJ_WIDTH = 2560
B256_NA_Q, B256_NA_K, B256_NA_V, B256_DF_Q, B256_DF_K, B256_DF_V, B256_GQ_Q, B256_ML_CQ = range(8)
B128_GQ_K, B128_GQ_V, B128_ML_CKV, B128_ML_KR = range(16, 20)

VMEM_LIMIT_BYTES = 56 * 1024 * 1024

TOKEN_TILE = 1024
FF_CHUNK = 256
Q_TILE = 256
NA_GROUP_ROWS = 4
NA_BAND_ROWS = 12
NEG_BIG = -1e30
LOG2E = math.log2(math.e)


def _rot_perm(width, group):
    j = np.arange(width)
    jj = j % group
    half = group // 2
    src = (j // group) * group + (jj + half) % group
    sign = np.where(jj < half, -1.0, 1.0).astype(np.float32)
    return src, sign


def _compiler_params():
    return pltpu.CompilerParams(dimension_semantics=("arbitrary",), vmem_limit_bytes=VMEM_LIMIT_BYTES)


def _const_spec(shape):
    zeros = (0,) * len(shape)
    return pl.BlockSpec(shape, lambda i: zeros, pipeline_mode=pl.Buffered(1))


def _rms(x):
    return lax.rsqrt(jnp.mean(x * x, axis=-1, keepdims=True) + EPS)


def _dot(a, b):
    return jnp.dot(a, b, preferred_element_type=F32)


def _inproj_kernel(x_ref, g_ref, w_ref, o_ref):
    x = x_ref[...]
    h = (x * _rms(x) * g_ref[...]).astype(BF16)
    o_ref[...] = _dot(h, w_ref[...])


def _inproj(x2d, gain, w):
    t = x2d.shape[0]
    return pl.pallas_call(
        _inproj_kernel,
        out_shape=jax.ShapeDtypeStruct((t, PROJ_WIDTH), F32),
        grid=(t // TOKEN_TILE,),
        in_specs=[
            pl.BlockSpec((TOKEN_TILE, D_MODEL), lambda i: (i, 0)),
            _const_spec((1, D_MODEL)),
            _const_spec((D_MODEL, PROJ_WIDTH)),
        ],
        out_specs=pl.BlockSpec((TOKEN_TILE, PROJ_WIDTH), lambda i: (i, 0)),
        compiler_params=_compiler_params(),
        name="inproj",
    )(x2d, gain, w)


def _segment_mean_square(x, seg):
    w = x.shape[-1]
    same = (lax.broadcasted_iota(jnp.int32, (w, w), 0) // seg) == (lax.broadcasted_iota(jnp.int32, (w, w), 1) // seg)
    ones = jnp.where(same, 1.0, 0.0).astype(BF16)
    sq = x * x
    hi = sq.astype(BF16)
    lo = (sq - hi.astype(F32)).astype(BF16)
    return (_dot(hi, ones) + _dot(lo, ones)) * (1.0 / seg)


def _dot_row_halves(a, b):
    half = a.shape[0] // 2
    return jnp.concatenate([_dot(a[:half], b), _dot(a[half:], b)], axis=0)


def _ones_row_block(width):
    return jnp.where(lax.broadcasted_iota(jnp.int32, (16, width), 0) == 0, 1.0, 0.0).astype(BF16)


def _normalised_heads(accs):
    return jnp.concatenate([acc[:HEAD_DIM] / acc[HEAD_DIM:HEAD_DIM + 1] for acc in accs], axis=0)


def _pipelined_softmax_pv(n_tiles, n_heads, n_chunks, score_chunks, value_chunk, combine, s_scr, acc_scr, o_ref,
                          exp_fn=jnp.exp2):
    assert n_heads % 2 == 0
    rows = s_scr.shape[1] // n_chunks

    def scores_into(slot, qi, h):
        m8 = None
        for c, st in enumerate(score_chunks(qi, h)):
            s_scr[slot, c * rows:(c + 1) * rows, :] = st
            cm = jnp.max(st.reshape(-1, 8, st.shape[-1]), axis=0)
            m8 = cm if m8 is None else jnp.maximum(m8, cm)
        return jnp.max(m8, axis=0, keepdims=True)

    def write_tile(qi):
        r0 = qi * Q_TILE if isinstance(qi, int) else pl.multiple_of(qi * Q_TILE, Q_TILE)
        o_ref[0, pl.ds(r0, Q_TILE), :] = combine([acc_scr[h] for h in range(n_heads)]).T.astype(o_ref.dtype)

    acc_scr[...] = jnp.ones_like(acc_scr)

    def body(qi, m):
        next_qi = jnp.minimum(qi + 1, n_tiles - 1)
        for h in range(n_heads):
            slot = h % 2
            nq, nh = (qi, h + 1) if h + 1 < n_heads else (next_qi, 0)
            next_m = scores_into(1 - slot, nq, nh)
            if h == 0:
                write_tile(jnp.maximum(qi - 1, 0))
            acc = None
            for c in range(n_chunks):
                p = exp_fn(s_scr[slot, c * rows:(c + 1) * rows, :] - m).astype(BF16)
                part = _dot(value_chunk(qi, h, c), p)
                acc = part if acc is None else acc + part
            acc_scr[h] = acc
            m = next_m
        return m

    lax.fori_loop(0, n_tiles, body, scores_into(0, 0, 0), unroll=2)
    write_tile(n_tiles - 1)


def _proj_spec(width, block):
    return pl.BlockSpec((1, SEQ, width), lambda b: (b, 0, block))


def _na_kernel(q_ref, k_ref, v_ref, tab_ref, o_ref, qt_scr, k_scr, vt_scr, s_scr, acc_scr):
    assert NA_GROUP_ROWS * GRID_W == Q_TILE and NA_BAND_ROWS % NA_GROUP_ROWS == 0
    n_tiles = SEQ // Q_TILE
    band_chunks = NA_BAND_ROWS // NA_GROUP_ROWS
    pair = 2 * HEAD_DIM
    q_scale = HEAD_DIM ** -0.5 * LOG2E

    def prep(ci, carry):
        r0 = pl.multiple_of(ci * Q_TILE, Q_TILE)
        rows = pl.ds(r0, Q_TILE)
        qt = (q_ref[0, rows, :] * q_scale).T
        row = lax.broadcasted_iota(jnp.int32, (pair, Q_TILE), 0)
        for h in range(N_HEADS):
            own = (row >= (h % 2) * HEAD_DIM) & (row < (h % 2 + 1) * HEAD_DIM)
            qt_scr[ci, h] = jnp.where(own, qt[(h // 2) * pair:(h // 2 + 1) * pair], 0.0).astype(BF16)
        for p in range(N_HEADS // 2):
            k_scr[p, rows, :] = k_ref[0, rows, p * pair:(p + 1) * pair].astype(BF16)
        vt = v_ref[0, rows, :].T.astype(BF16)
        for h in range(N_HEADS):
            vt_scr[h, ci, 0:HEAD_DIM, :] = vt[h * HEAD_DIM:(h + 1) * HEAD_DIM]
            vt_scr[h, ci, HEAD_DIM:, :] = _ones_row_block(Q_TILE)
        return carry

    lax.fori_loop(0, n_tiles, prep, 0, unroll=4)

    def key_chunk(gi, c):
        return jnp.clip(gi - (NA_WIN_ROWS // 2) // NA_GROUP_ROWS, 0, n_tiles - band_chunks) + c

    def score_chunks(gi, h):
        kind = jnp.where(gi == 0, 0, jnp.where(gi == n_tiles - 1, 2, 1))
        k0 = pl.multiple_of(key_chunk(gi, 0) * Q_TILE, Q_TILE)
        return [_dot_row_halves(k_scr[h // 2, pl.ds(k0 + c * Q_TILE, Q_TILE), :], qt_scr[gi, h])
                + tab_ref[kind, h, c * Q_TILE:(c + 1) * Q_TILE, :] for c in range(band_chunks)]

    def value_chunk(gi, h, c):
        return vt_scr[h, key_chunk(gi, c)]

    _pipelined_softmax_pv(n_tiles, N_HEADS, band_chunks, score_chunks, value_chunk, _normalised_heads, s_scr, acc_scr,
                          o_ref)


def _na_tables(rel_bias):
    a = np.arange(NA_GROUP_ROWS)[:, None, None, None]
    c = np.arange(GRID_W)[None, :, None, None]
    i = np.arange(NA_BAND_ROWS)[None, None, :, None]
    kc = np.arange(GRID_W)[None, None, None, :]
    cs = np.clip(c - NA_WIN_COLS // 2, 0, GRID_W - NA_WIN_COLS)
    col_ok = (kc >= cs) & (kc < cs + NA_WIN_COLS)
    dc = kc - c + NA_WIN_COLS - 1
    col_sel = (dc[..., None] == np.arange(2 * NA_WIN_COLS - 1)) & col_ok[..., None]
    row_sels = []
    last_r0 = GRID_ROWS - NA_GROUP_ROWS
    for r0, band_row in ((0, 0), (NA_GROUP_ROWS, 0), (last_r0, GRID_ROWS - NA_BAND_ROWS)):
        r = r0 + a
        rs = np.clip(r - NA_WIN_ROWS // 2, 0, GRID_ROWS - NA_WIN_ROWS)
        key_row = band_row + i
        row_ok = (key_row >= rs) & (key_row < rs + NA_WIN_ROWS)
        dr = key_row - r + NA_WIN_ROWS - 1
        row_sels.append((dr[..., None] == np.arange(2 * NA_WIN_ROWS - 1)) & row_ok[..., None])
    row_sel = np.stack(row_sels)[:, :, 0, :, 0, :].astype(np.float32)
    col_sel = col_sel[0, :, 0, :, :].astype(np.float32)
    vals = jnp.einsum("taiu,huv,ckv->thikac", row_sel, rel_bias.astype(F32), col_sel,
                      precision=lax.Precision.HIGHEST)
    inside = np.einsum("taiu,ckv->tikac", row_sel, col_sel) > 0
    tab = jnp.where(inside[:, None], vals * LOG2E, NEG_BIG)
    q_rows, band = NA_GROUP_ROWS * GRID_W, NA_BAND_ROWS * GRID_W
    return tab.reshape(3, N_HEADS, band, q_rows)


def _na_attention(proj, tables):
    b = proj.shape[0]
    return pl.pallas_call(
        _na_kernel,
        out_shape=jax.ShapeDtypeStruct((b, SEQ, GROUP_WIDTH), BF16),
        grid=(b,),
        in_specs=[
            _proj_spec(256, B256_NA_Q), _proj_spec(256, B256_NA_K), _proj_spec(256, B256_NA_V),
            _const_spec(tables.shape),
        ],
        out_specs=pl.BlockSpec((1, SEQ, GROUP_WIDTH), lambda i: (i, 0, 0)),
        scratch_shapes=[pltpu.VMEM((SEQ // Q_TILE, N_HEADS, 2 * HEAD_DIM, Q_TILE), BF16),
                        pltpu.VMEM((N_HEADS // 2, SEQ, 2 * HEAD_DIM), BF16),
                        pltpu.VMEM((N_HEADS, SEQ // Q_TILE, HEAD_DIM + 16, Q_TILE), BF16),
                        pltpu.VMEM((2, NA_BAND_ROWS * GRID_W, Q_TILE), F32),
                        pltpu.VMEM((N_HEADS, HEAD_DIM + 16, Q_TILE), F32)],
        compiler_params=_compiler_params(),
        name="na_attention",
    )(proj, proj, proj, tables)


def _alibi_slope(h):
    return 2.0 ** (-8.0 * (h + 1) / N_HEADS)


def _diff_kernel(lambda_init, q_ref, k_ref, v_ref, lq1_ref, lk1_ref, lq2_ref, lk2_ref, subln_ref, diag_ref, o_ref,
                 qt_scr, qf_scr, k_scr, vt_scr, s_scr, acc_scr):
    scale = DIFF_QK_DIM ** -0.5
    n_tiles = SEQ // Q_TILE
    pair = 2 * HEAD_DIM
    n_feat = 16
    lam = (jnp.exp(jnp.sum(lq1_ref[...] * lk1_ref[...], axis=-1, keepdims=True))
           - jnp.exp(jnp.sum(lq2_ref[...] * lk2_ref[...], axis=-1, keepdims=True)) + lambda_init)

    def prep(ci, carry):
        r0 = pl.multiple_of(ci * Q_TILE, Q_TILE)
        rows = pl.ds(r0, Q_TILE)
        qt = (q_ref[0, rows, :] * scale).T
        row = lax.broadcasted_iota(jnp.int32, (pair, Q_TILE), 0)
        feat_row = lax.broadcasted_iota(jnp.int32, (n_feat, Q_TILE), 0)
        i = r0 + lax.broadcasted_iota(jnp.int32, (n_feat, Q_TILE), 1)
        i_lo = i % POS_SPLIT
        i_hi = i - i_lo
        for h in range(N_HEADS):
            group = qt[(h // 2) * pair:(h // 2 + 1) * pair]
            for mp in range(2):
                lo = (h % 2) * HEAD_DIM + mp * DIFF_QK_DIM
                keep = (row >= lo) & (row < lo + DIFF_QK_DIM)
                qt_scr[ci, 2 * h + mp] = jnp.where(keep, group, 0.0).astype(BF16)
            slope = _alibi_slope(h)
            feat = jnp.where(feat_row < 2, slope,
                             jnp.where(feat_row == 2, -slope * i_hi.astype(F32),
                                       jnp.where(feat_row == 3, -slope * i_lo.astype(F32), 0.0)))
            qf_scr[ci, h, 0:n_feat, :] = feat.astype(BF16)
            qf_scr[ci, h, n_feat:, :] = (-feat).astype(BF16)
        lane = lax.broadcasted_iota(jnp.int32, (Q_TILE, pair), 1)
        j = r0 + lax.broadcasted_iota(jnp.int32, (Q_TILE, pair), 0)
        j_lo = j % POS_SPLIT
        k_feat = jnp.where(lane == 0, (j - j_lo).astype(F32),
                           jnp.where(lane == 1, j_lo.astype(F32), jnp.where(lane < 4, 1.0, 0.0))).astype(BF16)
        for p in range(N_HEADS // 2):
            k_scr[p, rows, 0:pair] = k_ref[0, rows, p * pair:(p + 1) * pair].astype(BF16)
            k_scr[p, rows, pair:] = k_feat
        vt = v_ref[0, rows, :].T.astype(BF16)
        for h in range(N_HEADS):
            vt_scr[h, ci, 0:HEAD_DIM, :] = vt[h * HEAD_DIM:(h + 1) * HEAD_DIM]
            vt_scr[h, ci, HEAD_DIM:, :] = _ones_row_block(Q_TILE)
        return carry

    lax.fori_loop(0, n_tiles, prep, 0, unroll=4)

    def key_chunk(qi, c):
        wrapped = qi + c >= n_tiles
        return jnp.where(wrapped, qi + c - n_tiles, qi + c), wrapped

    def score_chunk(qi, item, c):
        h = item // 2
        kc, left = key_chunk(qi, c)
        form = 0 if c == 0 else pl.multiple_of(jnp.where(left, 0, n_feat), n_feat)
        feat = qf_scr[qi, h, pl.ds(form, n_feat), :]
        rhs = jnp.concatenate([qt_scr[qi, item], feat, jnp.zeros((pair - n_feat, Q_TILE), BF16)], axis=0)
        st = _dot_row_halves(k_scr[h // 2, pl.ds(pl.multiple_of(kc * Q_TILE, Q_TILE), Q_TILE), :], rhs)
        return st + diag_ref[h] if c == 0 else st

    def score_chunks(qi, item):
        return [score_chunk(qi, item, c) for c in range(n_tiles)]

    def value_chunk(qi, item, c):
        return vt_scr[item // 2, key_chunk(qi, c)[0]]

    def combine(accs):
        outs = []
        for h in range(N_HEADS):
            a1, a2 = accs[2 * h], accs[2 * h + 1]
            o = a1[:HEAD_DIM] / a1[HEAD_DIM:HEAD_DIM + 1] - lam * (a2[:HEAD_DIM] / a2[HEAD_DIM:HEAD_DIM + 1])
            r = lax.rsqrt(jnp.mean(o * o, axis=0, keepdims=True) + EPS)
            outs.append(o * r * subln_ref[...] * (1.0 - lambda_init))
        return jnp.concatenate(outs, axis=0)

    _pipelined_softmax_pv(n_tiles, 2 * N_HEADS, n_tiles, score_chunks, value_chunk, combine, s_scr, acc_scr, o_ref,
                          exp_fn=jnp.exp)


def _diff_attention(proj, lq1, lk1, lq2, lk2, subln, lambda_init):
    b = proj.shape[0]
    vec = _const_spec((1, DIFF_QK_DIM))
    n_tiles = SEQ // Q_TILE
    rel = np.arange(Q_TILE)
    over = -2.0 * np.maximum(rel[:, None] - rel[None, :], 0).astype(np.float32)
    diag = jnp.asarray(np.stack([_alibi_slope(h) * over for h in range(N_HEADS)]))
    subln_cols = jnp.broadcast_to(subln.reshape(HEAD_DIM, 1), (HEAD_DIM, Q_TILE))
    return pl.pallas_call(
        functools.partial(_diff_kernel, lambda_init),
        out_shape=jax.ShapeDtypeStruct((b, SEQ, GROUP_WIDTH), BF16),
        grid=(b,),
        in_specs=[
            _proj_spec(256, B256_DF_Q), _proj_spec(256, B256_DF_K), _proj_spec(256, B256_DF_V),
            vec, vec, vec, vec, _const_spec((HEAD_DIM, Q_TILE)), _const_spec(diag.shape),
        ],
        out_specs=pl.BlockSpec((1, SEQ, GROUP_WIDTH), lambda i: (i, 0, 0)),
        scratch_shapes=[pltpu.VMEM((n_tiles, 2 * N_HEADS, 2 * HEAD_DIM, Q_TILE), BF16),
                        pltpu.VMEM((n_tiles, N_HEADS, 32, Q_TILE), BF16),
                        pltpu.VMEM((N_HEADS // 2, SEQ, 4 * HEAD_DIM), BF16),
                        pltpu.VMEM((N_HEADS, n_tiles, HEAD_DIM + 16, Q_TILE), BF16),
                        pltpu.VMEM((2, SEQ, Q_TILE), F32),
                        pltpu.VMEM((2 * N_HEADS, HEAD_DIM + 16, Q_TILE), F32)],
        compiler_params=_compiler_params(),
        name="diff_attention",
    )(proj, proj, proj, lq1, lk1, lq2, lk2, subln_cols, diag)


def _swap_rotary_halves(x):
    quarter = HEAD_DIM // 4
    first = (lax.broadcasted_iota(jnp.int32, x.shape, 1) % (2 * quarter)) < quarter
    lanes = x.shape[1]
    return jnp.where(first, pltpu.roll(x, lanes - quarter, 1), pltpu.roll(x, quarter, 1))


def _gqa_kernel(q_ref, k_ref, v_ref, qc_ref, qs_ref, kc_ref, ks_ref, o_ref,
                qt_scr, k_scr, vt_scr, s_scr, acc_scr):
    kv_heads = N_HEADS // 2
    n_tiles = SEQ // Q_TILE
    pair = 2 * HEAD_DIM
    quarter = HEAD_DIM // 4

    def normed_rotary(x, gain_cos, gain_sin):
        rotated = x * gain_cos + _swap_rotary_halves(x) * gain_sin
        return rotated * lax.rsqrt(_segment_mean_square(x, HEAD_DIM) + EPS)

    def normed_rotary_transposed(xt, gain_cos, gain_sin):
        swapped = jnp.concatenate([xt[(b ^ 1) * quarter:((b ^ 1) + 1) * quarter] for b in range(4)], axis=0)
        rotated = xt * gain_cos + swapped * gain_sin
        return rotated * lax.rsqrt(jnp.mean(xt * xt, axis=0, keepdims=True) + EPS)

    def prep(ci, carry):
        r0 = pl.multiple_of(ci * Q_TILE, Q_TILE)
        rows = pl.ds(r0, Q_TILE)
        zeros = jnp.zeros((HEAD_DIM, Q_TILE), BF16)
        for half in range(2):
            xt = q_ref[0, rows, half * pair:(half + 1) * pair].T
            for j in range(2):
                h = 2 * half + j
                kv = h // (N_HEADS // kv_heads)
                head_rows = slice(j * HEAD_DIM, (j + 1) * HEAD_DIM)
                yt = normed_rotary_transposed(xt[head_rows], qc_ref[ci, head_rows, :],
                                              qs_ref[ci, head_rows, :]).astype(BF16)
                for part in range(kv_heads):
                    block = yt if part == kv else zeros
                    qt_scr[ci, h * pair + part * HEAD_DIM:h * pair + (part + 1) * HEAD_DIM, :] = block
        k_scr[rows, :] = normed_rotary(k_ref[0, rows, :], kc_ref[rows, :], ks_ref[rows, :]).astype(BF16)
        vt = v_ref[0, rows, :].T.astype(BF16)
        for kv in range(kv_heads):
            vt_scr[kv, ci, 0:HEAD_DIM, :] = vt[kv * HEAD_DIM:(kv + 1) * HEAD_DIM]
            vt_scr[kv, ci, HEAD_DIM:, :] = _ones_row_block(Q_TILE)
        return carry

    lax.fori_loop(0, n_tiles, prep, 0, unroll=4)

    def score_chunks(qi, h):
        qt = qt_scr[qi, h * pair:(h + 1) * pair, :]
        return [_dot_row_halves(k_scr[c * Q_TILE:(c + 1) * Q_TILE, :], qt) for c in range(n_tiles)]

    def value_chunk(qi, h, c):
        return vt_scr[h // (N_HEADS // kv_heads), c]

    _pipelined_softmax_pv(n_tiles, N_HEADS, n_tiles, score_chunks, value_chunk, _normalised_heads, s_scr, acc_scr,
                          o_ref)


def _gqa_attention(proj, gq, gqp, gk, gkp, cos, sin):
    b = proj.shape[0]
    tab = _const_spec((SEQ, 2 * HEAD_DIM))
    q_scale = HEAD_DIM ** -0.5 * LOG2E
    _, sign = _rot_perm(2 * HEAD_DIM, HEAD_DIM // 2)
    per_tile = lambda t: t.T.reshape(2 * HEAD_DIM, SEQ // Q_TILE, Q_TILE).transpose(1, 0, 2)
    tables = (per_tile(gq * cos * q_scale), per_tile(gqp * (sign * q_scale) * sin), gk * cos, gkp * sign * sin)
    tab_t = _const_spec((SEQ // Q_TILE, 2 * HEAD_DIM, Q_TILE))
    return pl.pallas_call(
        _gqa_kernel,
        out_shape=jax.ShapeDtypeStruct((b, SEQ, GROUP_WIDTH), BF16),
        grid=(b,),
        in_specs=[
            _proj_spec(256, B256_GQ_Q), _proj_spec(128, B128_GQ_K), _proj_spec(128, B128_GQ_V),
            tab_t, tab_t, tab, tab,
        ],
        out_specs=pl.BlockSpec((1, SEQ, GROUP_WIDTH), lambda i: (i, 0, 0)),
        scratch_shapes=[pltpu.VMEM((SEQ // Q_TILE, N_HEADS * 2 * HEAD_DIM, Q_TILE), BF16),
                        pltpu.VMEM((SEQ, 2 * HEAD_DIM), BF16),
                        pltpu.VMEM((N_HEADS // 2, SEQ // Q_TILE, HEAD_DIM + 16, Q_TILE), BF16),
                        pltpu.VMEM((2, SEQ, Q_TILE), F32),
                        pltpu.VMEM((N_HEADS, HEAD_DIM + 16, Q_TILE), F32)],
        compiler_params=_compiler_params(),
        name="gqa_attention",
    )(proj, proj, proj, *tables)


def _mla_kernel(cq_ref, ckv_ref, kr_ref, gq_ref, gkv_ref, wuq_ref, wukv_ref, wv_ref, cos_ref, sin_ref, qc_ref, qs_ref,
                o_ref, qt_scr, k_scr, vt_scr, s_scr, acc_scr):
    q_scale = (MLA_NOPE_DIM + MLA_ROPE_DIM) ** -0.5 * LOG2E
    n_tiles = SEQ // Q_TILE
    half_rope = MLA_ROPE_DIM // 2
    rope_end = MLA_NOPE_DIM + MLA_ROPE_DIM

    def prep(ci, carry):
        r0 = pl.multiple_of(ci * Q_TILE, Q_TILE)
        rows = pl.ds(r0, Q_TILE)
        cos, sin = cos_ref[rows, :], sin_ref[rows, :]
        cq = cq_ref[0, rows, :]
        cqn_t = (cq * _rms(cq) * gq_ref[...]).T.astype(BF16)
        ckv = ckv_ref[0, rows, :]
        ckvn_f32 = ckv * _rms(ckv) * gkv_ref[...]
        ckvn, ckvn_t = ckvn_f32.astype(BF16), ckvn_f32.T.astype(BF16)
        kr = kr_ref[0, rows, :]
        k_rope = kr * cos + pltpu.roll(kr, 128 - MLA_ROPE_DIM, 1) * sin
        k_rope2 = jnp.concatenate([k_rope, k_rope], axis=1)
        q_cos, q_sin = qc_ref[ci], qs_ref[ci]
        for p in range(N_HEADS // 2):
            cols = slice(p * 256, (p + 1) * 256)
            qt = _dot(wuq_ref[cols, :], cqn_t)
            for j in range(2):
                base = (2 * p + j) * 128
                rope = qt[j * 128 + MLA_NOPE_DIM:j * 128 + rope_end]
                swapped = jnp.concatenate([rope[half_rope:], rope[:half_rope]], axis=0)
                qt_scr[ci, base:base + MLA_NOPE_DIM, :] = (qt[j * 128:j * 128 + MLA_NOPE_DIM] * q_scale).astype(BF16)
                qt_scr[ci, base + MLA_NOPE_DIM:base + rope_end, :] = (rope * q_cos + swapped * q_sin).astype(BF16)
                qt_scr[ci, base + rope_end:base + 128, :] = jnp.zeros((128 - rope_end, Q_TILE), BF16)
            k2 = (_dot(ckvn, wukv_ref[:, cols]) + k_rope2).astype(BF16)
            k_scr[2 * p, rows, :] = k2[:, :128]
            k_scr[2 * p + 1, rows, :] = k2[:, 128:]
        vt = _dot(wv_ref[...], ckvn_t).astype(BF16)
        for h in range(N_HEADS):
            vt_scr[h, ci, 0:HEAD_DIM, :] = vt[h * HEAD_DIM:(h + 1) * HEAD_DIM]
            vt_scr[h, ci, HEAD_DIM:, :] = _ones_row_block(Q_TILE)
        return carry

    lax.fori_loop(0, n_tiles, prep, 0, unroll=4)

    def score_chunks(qi, h):
        qt = qt_scr[qi, h * 128:(h + 1) * 128, :]
        return [_dot_row_halves(k_scr[h, c * Q_TILE:(c + 1) * Q_TILE, :], qt) for c in range(n_tiles)]

    def value_chunk(qi, h, c):
        return vt_scr[h, c]

    _pipelined_softmax_pv(n_tiles, N_HEADS, n_tiles, score_chunks, value_chunk, _normalised_heads, s_scr, acc_scr,
                          o_ref)


def _mla_attention(proj, gq, gkv, weights, cos, sin):
    b = proj.shape[0]
    tab = _const_spec((SEQ, 128))
    n_tiles = SEQ // Q_TILE
    q_scale = (MLA_NOPE_DIM + MLA_ROPE_DIM) ** -0.5 * LOG2E
    _, sign = _rot_perm(MLA_ROPE_DIM, MLA_ROPE_DIM)
    rope_lanes = slice(MLA_NOPE_DIM, MLA_NOPE_DIM + MLA_ROPE_DIM)
    per_tile = lambda t: t.T.reshape(MLA_ROPE_DIM, n_tiles, Q_TILE).transpose(1, 0, 2)
    q_cos = per_tile(cos[:, rope_lanes] * q_scale)
    q_sin = per_tile(sin[:, rope_lanes] * (sign * q_scale))
    tab_t = _const_spec((n_tiles, MLA_ROPE_DIM, Q_TILE))
    return pl.pallas_call(
        _mla_kernel,
        out_shape=jax.ShapeDtypeStruct((b, SEQ, GROUP_WIDTH), BF16),
        grid=(b,),
        in_specs=[
            _proj_spec(256, B256_ML_CQ), _proj_spec(128, B128_ML_CKV), _proj_spec(128, B128_ML_KR),
            _const_spec((1, GROUP_WIDTH)), _const_spec((1, MLA_KV_LORA)),
            *[_const_spec(w.shape) for w in weights], tab, tab, tab_t, tab_t,
        ],
        out_specs=pl.BlockSpec((1, SEQ, GROUP_WIDTH), lambda i: (i, 0, 0)),
        scratch_shapes=[pltpu.VMEM((SEQ // Q_TILE, N_HEADS * 128, Q_TILE), BF16),
                        pltpu.VMEM((N_HEADS, SEQ, 128), BF16),
                        pltpu.VMEM((N_HEADS, SEQ // Q_TILE, HEAD_DIM + 16, Q_TILE), BF16),
                        pltpu.VMEM((2, SEQ, Q_TILE), F32),
                        pltpu.VMEM((N_HEADS, HEAD_DIM + 16, Q_TILE), F32)],
        compiler_params=_compiler_params(),
        name="mla_attention",
    )(proj, proj, proj, gq, gkv, *weights, cos, sin, q_cos, q_sin)


def _post_kernel(x_ref, a_ref, b_ref, c_ref, d_ref, wo_ref, g_mix_ref, g_pre_ref, wg_ref, wu_ref, wd_ref,
                 g_ffn_ref, o_ref, acc_ref):
    mix = (_dot(a_ref[...], wo_ref[0]) + _dot(b_ref[...], wo_ref[1])
           + _dot(c_ref[...], wo_ref[2]) + _dot(d_ref[...], wo_ref[3]))
    x = x_ref[...] + mix * _rms(mix) * g_mix_ref[...]
    h = (x * _rms(x) * g_pre_ref[...]).astype(BF16)
    acc_ref[...] = jnp.zeros_like(acc_ref)

    def body(ci, carry):
        gate = _dot(h, wg_ref[ci])
        up = _dot(h, wu_ref[ci])
        act = (gate * jax.nn.sigmoid(gate) * up).astype(BF16)
        acc_ref[...] += _dot(act, wd_ref[ci])
        return carry

    lax.fori_loop(0, D_FF // FF_CHUNK, body, 0, unroll=True)
    f = acc_ref[...]
    o_ref[...] = x + f * _rms(f) * g_ffn_ref[...]


def _post(x2d, a, b, c, d, wo, g_mix, g_pre, wg, wu, wd, g_ffn):
    t = x2d.shape[0]
    tok = lambda w: pl.BlockSpec((TOKEN_TILE, w), lambda i: (i, 0))
    vec = _const_spec((1, D_MODEL))
    return pl.pallas_call(
        _post_kernel,
        out_shape=jax.ShapeDtypeStruct((t, D_MODEL), F32),
        grid=(t // TOKEN_TILE,),
        in_specs=[
            tok(D_MODEL), tok(GROUP_WIDTH), tok(GROUP_WIDTH), tok(GROUP_WIDTH), tok(GROUP_WIDTH),
            _const_spec(wo.shape), vec, vec, _const_spec(wg.shape), _const_spec(wu.shape), _const_spec(wd.shape),
            vec,
        ],
        out_specs=tok(D_MODEL),
        scratch_shapes=[pltpu.VMEM((TOKEN_TILE, D_MODEL), F32)],
        compiler_params=_compiler_params(),
        name="outproj_swiglu",
    )(x2d, a, b, c, d, wo, g_mix, g_pre, wg, wu, wd, g_ffn)


def _layout_w_in(w):
    src32, sign32 = _rot_perm(MLA_ROPE_DIM, MLA_ROPE_DIM)
    k_rope = w[:, O_ML_KR:O_ML_KR + MLA_ROPE_DIM]
    cols = [
        w[:, :O_GQ_K],
        w[:, O_ML_CQ:O_ML_CKV],
        w[:, O_GQ_K:O_ML_CQ],
        w[:, O_ML_CKV:O_ML_KR],
        jnp.zeros((D_MODEL, MLA_NOPE_DIM), w.dtype), k_rope, k_rope[:, src32] * sign32,
    ]
    return jnp.concatenate(cols, axis=1).astype(BF16)


def _layout_mla_weights(w_uq, w_ukv):
    wq = w_uq.reshape(GROUP_WIDTH, N_HEADS, MLA_NOPE_DIM + MLA_ROPE_DIM)
    tail = 128 - MLA_NOPE_DIM - MLA_ROPE_DIM
    wuq_t = jnp.pad(wq, ((0, 0), (0, 0), (0, tail))).reshape(GROUP_WIDTH, -1).T
    wkv = w_ukv.reshape(MLA_KV_LORA, N_HEADS, MLA_NOPE_DIM + HEAD_DIM)
    w_k = jnp.pad(wkv[:, :, :MLA_NOPE_DIM], ((0, 0), (0, 0), (0, 128 - MLA_NOPE_DIM))).reshape(MLA_KV_LORA, -1)
    w_v_t = wkv[:, :, MLA_NOPE_DIM:].reshape(MLA_KV_LORA, -1).T
    return wuq_t.astype(BF16), w_k.astype(BF16), w_v_t.astype(BF16)


def _rotary_tables():
    pos = jnp.arange(SEQ)
    half = HEAD_DIM // 2
    inv = ROPE_THETA ** (-jnp.arange(0, half, 2, dtype=F32) / half)

    def angles(p):
        ang = p.astype(F32)[:, None] * inv[None, :]
        return jnp.concatenate([ang, ang], axis=-1)

    axial = jnp.concatenate([angles(pos // GRID_W), angles(pos % GRID_W)], axis=-1)
    seq = angles(pos)
    tail = 128 - MLA_NOPE_DIM - MLA_ROPE_DIM
    mla_cos = jnp.concatenate([jnp.ones((SEQ, MLA_NOPE_DIM), F32), jnp.cos(seq), jnp.zeros((SEQ, tail), F32)], -1)
    mla_sin = jnp.pad(jnp.sin(seq), ((0, 0), (MLA_NOPE_DIM, tail)))
    pair = lambda t: jnp.tile(t, (1, 2))
    return pair(jnp.cos(axial)), pair(jnp.sin(axial)), mla_cos, mla_sin


def kernel(x, pre_mix_norm, w_in, na_rel_bias, diff_lambda_q1, diff_lambda_k1, diff_lambda_q2, diff_lambda_k2,
           diff_subln, gqa_q_norm, gqa_k_norm, mla_q_norm, mla_kv_norm, mla_w_uq, mla_w_ukv, w_o, post_mix_norm,
           pre_ffn_norm, ffn_w_gate_up, ffn_w_down, post_ffn_norm):
    b, s, d = x.shape
    assert (s, d) == (SEQ, D_MODEL)
    depth = w_in.shape[0]
    src64, _ = _rot_perm(HEAD_DIM, HEAD_DIM // 2)
    ax_cos, ax_sin, mla_cos, mla_sin = _rotary_tables()
    row = lambda v: v.reshape(1, -1).astype(F32)
    n_chunks = D_FF // FF_CHUNK

    x2d = x.reshape(b * s, d)
    for l in range(depth):
        lambda_init = 0.8 - 0.6 * math.exp(-0.3 * l)
        proj = _inproj(x2d, row(pre_mix_norm[l]), _layout_w_in(w_in[l])).reshape(b, s, PROJ_WIDTH)

        a_out = _na_attention(proj, _na_tables(na_rel_bias[l]))
        b_out = _diff_attention(proj, row(diff_lambda_q1[l]), row(diff_lambda_k1[l]), row(diff_lambda_q2[l]),
                                row(diff_lambda_k2[l]), row(diff_subln[l]), lambda_init)
        pair_row = lambda v: row(jnp.tile(v, 2))
        c_out = _gqa_attention(proj, pair_row(gqa_q_norm[l]), pair_row(gqa_q_norm[l][src64]),
                               pair_row(gqa_k_norm[l]), pair_row(gqa_k_norm[l][src64]), ax_cos, ax_sin)
        mla_weights = _layout_mla_weights(mla_w_uq[l], mla_w_ukv[l])
        d_out = _mla_attention(proj, row(mla_q_norm[l]), row(mla_kv_norm[l]), mla_weights, mla_cos, mla_sin)

        gate_up = ffn_w_gate_up[l].astype(BF16)
        wg = gate_up[:, :D_FF].reshape(d, n_chunks, FF_CHUNK).transpose(1, 0, 2)
        wu = gate_up[:, D_FF:].reshape(d, n_chunks, FF_CHUNK).transpose(1, 0, 2)
        wd = ffn_w_down[l].astype(BF16).reshape(n_chunks, FF_CHUNK, d)
        wo = w_o[l].astype(BF16).reshape(N_HEADS, GROUP_WIDTH, d)
        flat = lambda t: t.reshape(b * s, GROUP_WIDTH)
        x2d = _post(x2d, flat(a_out), flat(b_out), flat(c_out), flat(d_out), wo, row(post_mix_norm[l]),
                    row(pre_ffn_norm[l]), wg, wu, wd, row(post_ffn_norm[l]))
    return x2d.reshape(b, s, d)
```

```python
import functools
import math

import numpy as np
import jax
import jax.numpy as jnp
from jax import lax
from jax.experimental import pallas as pl
from jax.experimental.pallas import tpu as pltpu

F32 = jnp.float32
BF16 = jnp.bfloat16

D_MODEL = 1024
SEQ = 2048
GRID_W = 64
GRID_ROWS = SEQ // GRID_W
HEAD_DIM = 64
N_HEADS = 4
GROUP_WIDTH = 256
EPS = 1e-6
ROPE_THETA = 10000.0

NA_WIN_ROWS = 8
NA_WIN_COLS = 16
DIFF_QK_DIM = 32
POS_SPLIT = 64
MLA_NOPE_DIM = 64
MLA_ROPE_DIM = 32
MLA_KV_LORA = 128
D_FF = 2816

O_GQ_Q, O_GQ_K, O_GQ_V, O_ML_CQ, O_ML_CKV, O_ML_KR = 1536, 1792, 1920, 2048, 2304, 2432

PROJ_WIDTH = 2560
B256_NA_Q, B256_NA_K, B256_NA_V, B256_DF_Q, B256_DF_K, B256_DF_V, B256_GQ_Q, B256_ML_CQ = range(8)
B128_GQ_K, B128_GQ_V, B128_ML_CKV, B128_ML_KR = range(16, 20)

VMEM_LIMIT_BYTES = 56 * 1024 * 1024

TOKEN_TILE = 1024
FF_CHUNK = 256
Q_TILE = 256
NA_GROUP_ROWS = 4
NA_BAND_ROWS = 12
NEG_BIG = -1e30
LOG2E = math.log2(math.e)


def _rot_perm(width, group):
    j = np.arange(width)
    jj = j % group
    half = group // 2
    src = (j // group) * group + (jj + half) % group
    sign = np.where(jj < half, -1.0, 1.0).astype(np.float32)
    return src, sign


def _compiler_params():
    return pltpu.CompilerParams(dimension_semantics=("arbitrary",), vmem_limit_bytes=VMEM_LIMIT_BYTES)


def _const_spec(shape):
    zeros = (0,) * len(shape)
    return pl.BlockSpec(shape, lambda i: zeros, pipeline_mode=pl.Buffered(1))


def _rms(x):
    return lax.rsqrt(jnp.mean(x * x, axis=-1, keepdims=True) + EPS)


def _dot(a, b):
    return jnp.dot(a, b, preferred_element_type=F32)


def _inproj_kernel(x_ref, g_ref, w_ref, o_ref):
    x = x_ref[...]
    h = (x * _rms(x) * g_ref[...]).astype(BF16)
    o_ref[...] = _dot(h, w_ref[...])


def _inproj(x2d, gain, w):
    t = x2d.shape[0]
    return pl.pallas_call(
        _inproj_kernel,
        out_shape=jax.ShapeDtypeStruct((t, PROJ_WIDTH), F32),
        grid=(t // TOKEN_TILE,),
        in_specs=[
            pl.BlockSpec((TOKEN_TILE, D_MODEL), lambda i: (i, 0)),
            _const_spec((1, D_MODEL)),
            _const_spec((D_MODEL, PROJ_WIDTH)),
        ],
        out_specs=pl.BlockSpec((TOKEN_TILE, PROJ_WIDTH), lambda i: (i, 0)),
        compiler_params=_compiler_params(),
        name="inproj",
    )(x2d, gain, w)


def _segment_mean_square(x, seg):
    w = x.shape[-1]
    same = (lax.broadcasted_iota(jnp.int32, (w, w), 0) // seg) == (lax.broadcasted_iota(jnp.int32, (w, w), 1) // seg)
    ones = jnp.where(same, 1.0, 0.0).astype(BF16)
    sq = x * x
    hi = sq.astype(BF16)
    lo = (sq - hi.astype(F32)).astype(BF16)
    return (_dot(hi, ones) + _dot(lo, ones)) * (1.0 / seg)


def _ones_row_block(width):
    return jnp.where(lax.broadcasted_iota(jnp.int32, (16, width), 0) == 0, 1.0, 0.0).astype(BF16)


def _normalised_heads(accs):
    return jnp.concatenate([acc[:HEAD_DIM] / acc[HEAD_DIM:HEAD_DIM + 1] for acc in accs], axis=0)


def _pipelined_softmax_pv(n_tiles, n_heads, n_chunks, score_chunks, value_chunk, combine, s_scr, acc_scr, o_ref,
                          exp_fn=jnp.exp2):
    assert n_heads % 2 == 0
    rows = s_scr.shape[1] // n_chunks

    def scores_into(slot, qi, h):
        m8 = None
        for c, st in enumerate(score_chunks(qi, h)):
            s_scr[slot, c * rows:(c + 1) * rows, :] = st
            cm = jnp.max(st.reshape(-1, 8, st.shape[-1]), axis=0)
            m8 = cm if m8 is None else jnp.maximum(m8, cm)
        return jnp.max(m8, axis=0, keepdims=True)

    def write_tile(qi):
        r0 = qi * Q_TILE if isinstance(qi, int) else pl.multiple_of(qi * Q_TILE, Q_TILE)
        o_ref[0, pl.ds(r0, Q_TILE), :] = combine([acc_scr[h] for h in range(n_heads)]).T.astype(o_ref.dtype)

    acc_scr[...] = jnp.ones_like(acc_scr)

    def body(qi, m):
        next_qi = jnp.minimum(qi + 1, n_tiles - 1)
        for h in range(n_heads):
            slot = h % 2
            nq, nh = (qi, h + 1) if h + 1 < n_heads else (next_qi, 0)
            next_m = scores_into(1 - slot, nq, nh)
            if h == 0:
                write_tile(jnp.maximum(qi - 1, 0))
            acc = None
            for c in range(n_chunks):
                p = exp_fn(s_scr[slot, c * rows:(c + 1) * rows, :] - m).astype(BF16)
                part = _dot(value_chunk(qi, h, c), p)
                acc = part if acc is None else acc + part
            acc_scr[h] = acc
            m = next_m
        return m

    lax.fori_loop(0, n_tiles, body, scores_into(0, 0, 0), unroll=2)
    write_tile(n_tiles - 1)


def _proj_spec(width, block):
    return pl.BlockSpec((1, SEQ, width), lambda b: (b, 0, block))


def _na_kernel(q_ref, k_ref, v_ref, tab_ref, o_ref, qt_scr, k_scr, vt_scr, s_scr, acc_scr):
    assert NA_GROUP_ROWS * GRID_W == Q_TILE and NA_BAND_ROWS % NA_GROUP_ROWS == 0
    n_tiles = SEQ // Q_TILE
    band_chunks = NA_BAND_ROWS // NA_GROUP_ROWS
    pair = 2 * HEAD_DIM
    q_scale = HEAD_DIM ** -0.5 * LOG2E

    def prep(ci, carry):
        r0 = pl.multiple_of(ci * Q_TILE, Q_TILE)
        rows = pl.ds(r0, Q_TILE)
        qt = (q_ref[0, rows, :] * q_scale).T
        row = lax.broadcasted_iota(jnp.int32, (pair, Q_TILE), 0)
        for h in range(N_HEADS):
            own = (row >= (h % 2) * HEAD_DIM) & (row < (h % 2 + 1) * HEAD_DIM)
            qt_scr[ci, h] = jnp.where(own, qt[(h // 2) * pair:(h // 2 + 1) * pair], 0.0).astype(BF16)
        for p in range(N_HEADS // 2):
            k_scr[p, rows, :] = k_ref[0, rows, p * pair:(p + 1) * pair].astype(BF16)
        vt = v_ref[0, rows, :].T.astype(BF16)
        for h in range(N_HEADS):
            vt_scr[h, ci, 0:HEAD_DIM, :] = vt[h * HEAD_DIM:(h + 1) * HEAD_DIM]
            vt_scr[h, ci, HEAD_DIM:, :] = _ones_row_block(Q_TILE)
        return carry

    lax.fori_loop(0, n_tiles, prep, 0, unroll=4)

    def key_chunk(gi, c):
        return jnp.clip(gi - (NA_WIN_ROWS // 2) // NA_GROUP_ROWS, 0, n_tiles - band_chunks) + c

    def score_chunks(gi, h):
        kind = jnp.where(gi == 0, 0, jnp.where(gi == n_tiles - 1, 2, 1))
        k0 = pl.multiple_of(key_chunk(gi, 0) * Q_TILE, Q_TILE)
        return [_dot(k_scr[h // 2, pl.ds(k0 + c * Q_TILE, Q_TILE), :], qt_scr[gi, h])
                + tab_ref[kind, h, c * Q_TILE:(c + 1) * Q_TILE, :] for c in range(band_chunks)]

    def value_chunk(gi, h, c):
        return vt_scr[h, key_chunk(gi, c)]

    _pipelined_softmax_pv(n_tiles, N_HEADS, band_chunks, score_chunks, value_chunk, _normalised_heads, s_scr, acc_scr,
                          o_ref)


def _na_tables(rel_bias):
    a = np.arange(NA_GROUP_ROWS)[:, None, None, None]
    c = np.arange(GRID_W)[None, :, None, None]
    i = np.arange(NA_BAND_ROWS)[None, None, :, None]
    kc = np.arange(GRID_W)[None, None, None, :]
    cs = np.clip(c - NA_WIN_COLS // 2, 0, GRID_W - NA_WIN_COLS)
    col_ok = (kc >= cs) & (kc < cs + NA_WIN_COLS)
    dc = kc - c + NA_WIN_COLS - 1
    col_sel = (dc[..., None] == np.arange(2 * NA_WIN_COLS - 1)) & col_ok[..., None]
    row_sels = []
    last_r0 = GRID_ROWS - NA_GROUP_ROWS
    for r0, band_row in ((0, 0), (NA_GROUP_ROWS, 0), (last_r0, GRID_ROWS - NA_BAND_ROWS)):
        r = r0 + a
        rs = np.clip(r - NA_WIN_ROWS // 2, 0, GRID_ROWS - NA_WIN_ROWS)
        key_row = band_row + i
        row_ok = (key_row >= rs) & (key_row < rs + NA_WIN_ROWS)
        dr = key_row - r + NA_WIN_ROWS - 1
        row_sels.append((dr[..., None] == np.arange(2 * NA_WIN_ROWS - 1)) & row_ok[..., None])
    row_sel = np.stack(row_sels)[:, :, 0, :, 0, :].astype(np.float32)
    col_sel = col_sel[0, :, 0, :, :].astype(np.float32)
    vals = jnp.einsum("taiu,huv,ckv->thikac", row_sel, rel_bias.astype(F32), col_sel,
                      precision=lax.Precision.HIGHEST)
    inside = np.einsum("taiu,ckv->tikac", row_sel, col_sel) > 0
    tab = jnp.where(inside[:, None], vals * LOG2E, NEG_BIG)
    q_rows, band = NA_GROUP_ROWS * GRID_W, NA_BAND_ROWS * GRID_W
    return tab.reshape(3, N_HEADS, band, q_rows)


def _na_attention(proj, tables):
    b = proj.shape[0]
    return pl.pallas_call(
        _na_kernel,
        out_shape=jax.ShapeDtypeStruct((b, SEQ, GROUP_WIDTH), BF16),
        grid=(b,),
        in_specs=[
            _proj_spec(256, B256_NA_Q), _proj_spec(256, B256_NA_K), _proj_spec(256, B256_NA_V),
            _const_spec(tables.shape),
        ],
        out_specs=pl.BlockSpec((1, SEQ, GROUP_WIDTH), lambda i: (i, 0, 0)),
        scratch_shapes=[pltpu.VMEM((SEQ // Q_TILE, N_HEADS, 2 * HEAD_DIM, Q_TILE), BF16),
                        pltpu.VMEM((N_HEADS // 2, SEQ, 2 * HEAD_DIM), BF16),
                        pltpu.VMEM((N_HEADS, SEQ // Q_TILE, HEAD_DIM + 16, Q_TILE), BF16),
                        pltpu.VMEM((2, NA_BAND_ROWS * GRID_W, Q_TILE), F32),
                        pltpu.VMEM((N_HEADS, HEAD_DIM + 16, Q_TILE), F32)],
        compiler_params=_compiler_params(),
        name="na_attention",
    )(proj, proj, proj, tables)


def _alibi_slope(h):
    return 2.0 ** (-8.0 * (h + 1) / N_HEADS)


def _diff_kernel(lambda_init, q_ref, k_ref, v_ref, lq1_ref, lk1_ref, lq2_ref, lk2_ref, subln_ref, diag_ref, o_ref,
                 qt_scr, qf_scr, k_scr, vt_scr, s_scr, acc_scr):
    scale = DIFF_QK_DIM ** -0.5
    n_tiles = SEQ // Q_TILE
    pair = 2 * HEAD_DIM
    n_feat = 16
    lam = (jnp.exp(jnp.sum(lq1_ref[...] * lk1_ref[...], axis=-1, keepdims=True))
           - jnp.exp(jnp.sum(lq2_ref[...] * lk2_ref[...], axis=-1, keepdims=True)) + lambda_init)

    def prep(ci, carry):
        r0 = pl.multiple_of(ci * Q_TILE, Q_TILE)
        rows = pl.ds(r0, Q_TILE)
        qt = (q_ref[0, rows, :] * scale).T
        row = lax.broadcasted_iota(jnp.int32, (pair, Q_TILE), 0)
        feat_row = lax.broadcasted_iota(jnp.int32, (n_feat, Q_TILE), 0)
        i = r0 + lax.broadcasted_iota(jnp.int32, (n_feat, Q_TILE), 1)
        i_lo = i % POS_SPLIT
        i_hi = i - i_lo
        for h in range(N_HEADS):
            group = qt[(h // 2) * pair:(h // 2 + 1) * pair]
            for mp in range(2):
                lo = (h % 2) * HEAD_DIM + mp * DIFF_QK_DIM
                keep = (row >= lo) & (row < lo + DIFF_QK_DIM)
                qt_scr[ci, 2 * h + mp] = jnp.where(keep, group, 0.0).astype(BF16)
            slope = _alibi_slope(h)
            feat = jnp.where(feat_row < 2, slope,
                             jnp.where(feat_row == 2, -slope * i_hi.astype(F32),
                                       jnp.where(feat_row == 3, -slope * i_lo.astype(F32), 0.0)))
            qf_scr[ci, h, 0:n_feat, :] = feat.astype(BF16)
            qf_scr[ci, h, n_feat:, :] = (-feat).astype(BF16)
        lane = lax.broadcasted_iota(jnp.int32, (Q_TILE, pair), 1)
        j = r0 + lax.broadcasted_iota(jnp.int32, (Q_TILE, pair), 0)
        j_lo = j % POS_SPLIT
        k_feat = jnp.where(lane == 0, (j - j_lo).astype(F32),
                           jnp.where(lane == 1, j_lo.astype(F32), jnp.where(lane < 4, 1.0, 0.0))).astype(BF16)
        for p in range(N_HEADS // 2):
            k_scr[p, rows, 0:pair] = k_ref[0, rows, p * pair:(p + 1) * pair].astype(BF16)
            k_scr[p, rows, pair:] = k_feat
        vt = v_ref[0, rows, :].T.astype(BF16)
        for h in range(N_HEADS):
            vt_scr[h, ci, 0:HEAD_DIM, :] = vt[h * HEAD_DIM:(h + 1) * HEAD_DIM]
            vt_scr[h, ci, HEAD_DIM:, :] = _ones_row_block(Q_TILE)
        return carry

    lax.fori_loop(0, n_tiles, prep, 0, unroll=4)

    def key_chunk(qi, c):
        wrapped = qi + c >= n_tiles
        return jnp.where(wrapped, qi + c - n_tiles, qi + c), wrapped

    def score_chunk(qi, item, c):
        h = item // 2
        kc, left = key_chunk(qi, c)
        form = 0 if c == 0 else pl.multiple_of(jnp.where(left, 0, n_feat), n_feat)
        feat = qf_scr[qi, h, pl.ds(form, n_feat), :]
        rhs = jnp.concatenate([qt_scr[qi, item], feat, jnp.zeros((pair - n_feat, Q_TILE), BF16)], axis=0)
        st = _dot(k_scr[h // 2, pl.ds(pl.multiple_of(kc * Q_TILE, Q_TILE), Q_TILE), :], rhs)
        return st + diag_ref[h] if c == 0 else st

    def score_chunks(qi, item):
        return [score_chunk(qi, item, c) for c in range(n_tiles)]

    def value_chunk(qi, item, c):
        return vt_scr[item // 2, key_chunk(qi, c)[0]]

    def combine(accs):
        outs = []
        for h in range(N_HEADS):
            a1, a2 = accs[2 * h], accs[2 * h + 1]
            o = a1[:HEAD_DIM] / a1[HEAD_DIM:HEAD_DIM + 1] - lam * (a2[:HEAD_DIM] / a2[HEAD_DIM:HEAD_DIM + 1])
            r = lax.rsqrt(jnp.mean(o * o, axis=0, keepdims=True) + EPS)
            outs.append(o * r * subln_ref[...] * (1.0 - lambda_init))
        return jnp.concatenate(outs, axis=0)

    _pipelined_softmax_pv(n_tiles, 2 * N_HEADS, n_tiles, score_chunks, value_chunk, combine, s_scr, acc_scr, o_ref,
                          exp_fn=jnp.exp)


def _diff_attention(proj, lq1, lk1, lq2, lk2, subln, lambda_init):
    b = proj.shape[0]
    vec = _const_spec((1, DIFF_QK_DIM))
    n_tiles = SEQ // Q_TILE
    rel = np.arange(Q_TILE)
    over = -2.0 * np.maximum(rel[:, None] - rel[None, :], 0).astype(np.float32)
    diag = jnp.asarray(np.stack([_alibi_slope(h) * over for h in range(N_HEADS)]))
    subln_cols = jnp.broadcast_to(subln.reshape(HEAD_DIM, 1), (HEAD_DIM, Q_TILE))
    return pl.pallas_call(
        functools.partial(_diff_kernel, lambda_init),
        out_shape=jax.ShapeDtypeStruct((b, SEQ, GROUP_WIDTH), BF16),
        grid=(b,),
        in_specs=[
            _proj_spec(256, B256_DF_Q), _proj_spec(256, B256_DF_K), _proj_spec(256, B256_DF_V),
            vec, vec, vec, vec, _const_spec((HEAD_DIM, Q_TILE)), _const_spec(diag.shape),
        ],
        out_specs=pl.BlockSpec((1, SEQ, GROUP_WIDTH), lambda i: (i, 0, 0)),
        scratch_shapes=[pltpu.VMEM((n_tiles, 2 * N_HEADS, 2 * HEAD_DIM, Q_TILE), BF16),
                        pltpu.VMEM((n_tiles, N_HEADS, 32, Q_TILE), BF16),
                        pltpu.VMEM((N_HEADS // 2, SEQ, 4 * HEAD_DIM), BF16),
                        pltpu.VMEM((N_HEADS, n_tiles, HEAD_DIM + 16, Q_TILE), BF16),
                        pltpu.VMEM((2, SEQ, Q_TILE), F32),
                        pltpu.VMEM((2 * N_HEADS, HEAD_DIM + 16, Q_TILE), F32)],
        compiler_params=_compiler_params(),
        name="diff_attention",
    )(proj, proj, proj, lq1, lk1, lq2, lk2, subln_cols, diag)


def _swap_rotary_halves(x):
    quarter = HEAD_DIM // 4
    first = (lax.broadcasted_iota(jnp.int32, x.shape, 1) % (2 * quarter)) < quarter
    lanes = x.shape[1]
    return jnp.where(first, pltpu.roll(x, lanes - quarter, 1), pltpu.roll(x, quarter, 1))


def _gqa_kernel(q_ref, k_ref, v_ref, qc_ref, qs_ref, kc_ref, ks_ref, o_ref,
                qt_scr, k_scr, vt_scr, s_scr, acc_scr):
    kv_heads = N_HEADS // 2
    n_tiles = SEQ // Q_TILE
    pair = 2 * HEAD_DIM
    quarter = HEAD_DIM // 4

    def normed_rotary(x, gain_cos, gain_sin):
        rotated = x * gain_cos + _swap_rotary_halves(x) * gain_sin
        return rotated * lax.rsqrt(_segment_mean_square(x, HEAD_DIM) + EPS)

    def normed_rotary_transposed(xt, gain_cos, gain_sin):
        swapped = jnp.concatenate([xt[(b ^ 1) * quarter:((b ^ 1) + 1) * quarter] for b in range(4)], axis=0)
        rotated = xt * gain_cos + swapped * gain_sin
        return rotated * lax.rsqrt(jnp.mean(xt * xt, axis=0, keepdims=True) + EPS)

    def prep(ci, carry):
        r0 = pl.multiple_of(ci * Q_TILE, Q_TILE)
        rows = pl.ds(r0, Q_TILE)
        zeros = jnp.zeros((HEAD_DIM, Q_TILE), BF16)
        for half in range(2):
            xt = q_ref[0, rows, half * pair:(half + 1) * pair].T
            for j in range(2):
                h = 2 * half + j
                kv = h // (N_HEADS // kv_heads)
                head_rows = slice(j * HEAD_DIM, (j + 1) * HEAD_DIM)
                yt = normed_rotary_transposed(xt[head_rows], qc_ref[ci, head_rows, :],
                                              qs_ref[ci, head_rows, :]).astype(BF16)
                for part in range(kv_heads):
                    block = yt if part == kv else zeros
                    qt_scr[ci, h * pair + part * HEAD_DIM:h * pair + (part + 1) * HEAD_DIM, :] = block
        k_scr[rows, :] = normed_rotary(k_ref[0, rows, :], kc_ref[rows, :], ks_ref[rows, :]).astype(BF16)
        vt = v_ref[0, rows, :].T.astype(BF16)
        for kv in range(kv_heads):
            vt_scr[kv, ci, 0:HEAD_DIM, :] = vt[kv * HEAD_DIM:(kv + 1) * HEAD_DIM]
            vt_scr[kv, ci, HEAD_DIM:, :] = _ones_row_block(Q_TILE)
        return carry

    lax.fori_loop(0, n_tiles, prep, 0, unroll=4)

    def score_chunks(qi, h):
        qt = qt_scr[qi, h * pair:(h + 1) * pair, :]
        return [_dot(k_scr[c * Q_TILE:(c + 1) * Q_TILE, :], qt) for c in range(n_tiles)]

    def value_chunk(qi, h, c):
        return vt_scr[h // (N_HEADS // kv_heads), c]

    _pipelined_softmax_pv(n_tiles, N_HEADS, n_tiles, score_chunks, value_chunk, _normalised_heads, s_scr, acc_scr,
                          o_ref)


def _gqa_attention(proj, gq, gqp, gk, gkp, cos, sin):
    b = proj.shape[0]
    tab = _const_spec((SEQ, 2 * HEAD_DIM))
    q_scale = HEAD_DIM ** -0.5 * LOG2E
    _, sign = _rot_perm(2 * HEAD_DIM, HEAD_DIM // 2)
    per_tile = lambda t: t.T.reshape(2 * HEAD_DIM, SEQ // Q_TILE, Q_TILE).transpose(1, 0, 2)
    tables = (per_tile(gq * cos * q_scale), per_tile(gqp * (sign * q_scale) * sin), gk * cos, gkp * sign * sin)
    tab_t = _const_spec((SEQ // Q_TILE, 2 * HEAD_DIM, Q_TILE))
    return pl.pallas_call(
        _gqa_kernel,
        out_shape=jax.ShapeDtypeStruct((b, SEQ, GROUP_WIDTH), BF16),
        grid=(b,),
        in_specs=[
            _proj_spec(256, B256_GQ_Q), _proj_spec(128, B128_GQ_K), _proj_spec(128, B128_GQ_V),
            tab_t, tab_t, tab, tab,
        ],
        out_specs=pl.BlockSpec((1, SEQ, GROUP_WIDTH), lambda i: (i, 0, 0)),
        scratch_shapes=[pltpu.VMEM((SEQ // Q_TILE, N_HEADS * 2 * HEAD_DIM, Q_TILE), BF16),
                        pltpu.VMEM((SEQ, 2 * HEAD_DIM), BF16),
                        pltpu.VMEM((N_HEADS // 2, SEQ // Q_TILE, HEAD_DIM + 16, Q_TILE), BF16),
                        pltpu.VMEM((2, SEQ, Q_TILE), F32),
                        pltpu.VMEM((N_HEADS, HEAD_DIM + 16, Q_TILE), F32)],
        compiler_params=_compiler_params(),
        name="gqa_attention",
    )(proj, proj, proj, *tables)


def _mla_kernel(cq_ref, ckv_ref, kr_ref, gq_ref, gkv_ref, wuq_ref, wukv_ref, wv_ref, cos_ref, sin_ref, qc_ref, qs_ref,
                o_ref, qt_scr, k_scr, vt_scr, s_scr, acc_scr):
    q_scale = (MLA_NOPE_DIM + MLA_ROPE_DIM) ** -0.5 * LOG2E
    n_tiles = SEQ // Q_TILE
    half_rope = MLA_ROPE_DIM // 2
    rope_end = MLA_NOPE_DIM + MLA_ROPE_DIM

    def prep(ci, carry):
        r0 = pl.multiple_of(ci * Q_TILE, Q_TILE)
        rows = pl.ds(r0, Q_TILE)
        cos, sin = cos_ref[rows, :], sin_ref[rows, :]
        cq = cq_ref[0, rows, :]
        cqn_t = (cq * _rms(cq) * gq_ref[...]).T.astype(BF16)
        ckv = ckv_ref[0, rows, :]
        ckvn_f32 = ckv * _rms(ckv) * gkv_ref[...]
        ckvn, ckvn_t = ckvn_f32.astype(BF16), ckvn_f32.T.astype(BF16)
        kr = kr_ref[0, rows, :]
        k_rope = kr * cos + pltpu.roll(kr, 128 - MLA_ROPE_DIM, 1) * sin
        k_rope2 = jnp.concatenate([k_rope, k_rope], axis=1)
        q_cos, q_sin = qc_ref[ci], qs_ref[ci]
        for p in range(N_HEADS // 2):
            cols = slice(p * 256, (p + 1) * 256)
            qt = _dot(wuq_ref[cols, :], cqn_t)
            for j in range(2):
                base = (2 * p + j) * 128
                rope = qt[j * 128 + MLA_NOPE_DIM:j * 128 + rope_end]
                swapped = jnp.concatenate([rope[half_rope:], rope[:half_rope]], axis=0)
                qt_scr[ci, base:base + MLA_NOPE_DIM, :] = (qt[j * 128:j * 128 + MLA_NOPE_DIM] * q_scale).astype(BF16)
                qt_scr[ci, base + MLA_NOPE_DIM:base + rope_end, :] = (rope * q_cos + swapped * q_sin).astype(BF16)
                qt_scr[ci, base + rope_end:base + 128, :] = jnp.zeros((128 - rope_end, Q_TILE), BF16)
            k2 = (_dot(ckvn, wukv_ref[:, cols]) + k_rope2).astype(BF16)
            k_scr[2 * p, rows, :] = k2[:, :128]
            k_scr[2 * p + 1, rows, :] = k2[:, 128:]
        vt = _dot(wv_ref[...], ckvn_t).astype(BF16)
        for h in range(N_HEADS):
            vt_scr[h, ci, 0:HEAD_DIM, :] = vt[h * HEAD_DIM:(h + 1) * HEAD_DIM]
            vt_scr[h, ci, HEAD_DIM:, :] = _ones_row_block(Q_TILE)
        return carry

    lax.fori_loop(0, n_tiles, prep, 0, unroll=4)

    def score_chunks(qi, h):
        qt = qt_scr[qi, h * 128:(h + 1) * 128, :]
        return [_dot(k_scr[h, c * Q_TILE:(c + 1) * Q_TILE, :], qt) for c in range(n_tiles)]

    def value_chunk(qi, h, c):
        return vt_scr[h, c]

    _pipelined_softmax_pv(n_tiles, N_HEADS, n_tiles, score_chunks, value_chunk, _normalised_heads, s_scr, acc_scr,
                          o_ref)


def _mla_attention(proj, gq, gkv, weights, cos, sin):
    b = proj.shape[0]
    tab = _const_spec((SEQ, 128))
    n_tiles = SEQ // Q_TILE
    q_scale = (MLA_NOPE_DIM + MLA_ROPE_DIM) ** -0.5 * LOG2E
    _, sign = _rot_perm(MLA_ROPE_DIM, MLA_ROPE_DIM)
    rope_lanes = slice(MLA_NOPE_DIM, MLA_NOPE_DIM + MLA_ROPE_DIM)
    per_tile = lambda t: t.T.reshape(MLA_ROPE_DIM, n_tiles, Q_TILE).transpose(1, 0, 2)
    q_cos = per_tile(cos[:, rope_lanes] * q_scale)
    q_sin = per_tile(sin[:, rope_lanes] * (sign * q_scale))
    tab_t = _const_spec((n_tiles, MLA_ROPE_DIM, Q_TILE))
    return pl.pallas_call(
        _mla_kernel,
        out_shape=jax.ShapeDtypeStruct((b, SEQ, GROUP_WIDTH), BF16),
        grid=(b,),
        in_specs=[
            _proj_spec(256, B256_ML_CQ), _proj_spec(128, B128_ML_CKV), _proj_spec(128, B128_ML_KR),
            _const_spec((1, GROUP_WIDTH)), _const_spec((1, MLA_KV_LORA)),
            *[_const_spec(w.shape) for w in weights], tab, tab, tab_t, tab_t,
        ],
        out_specs=pl.BlockSpec((1, SEQ, GROUP_WIDTH), lambda i: (i, 0, 0)),
        scratch_shapes=[pltpu.VMEM((SEQ // Q_TILE, N_HEADS * 128, Q_TILE), BF16),
                        pltpu.VMEM((N_HEADS, SEQ, 128), BF16),
                        pltpu.VMEM((N_HEADS, SEQ // Q_TILE, HEAD_DIM + 16, Q_TILE), BF16),
                        pltpu.VMEM((2, SEQ, Q_TILE), F32),
                        pltpu.VMEM((N_HEADS, HEAD_DIM + 16, Q_TILE), F32)],
        compiler_params=_compiler_params(),
        name="mla_attention",
    )(proj, proj, proj, gq, gkv, *weights, cos, sin, q_cos, q_sin)


def _post_kernel(x_ref, a_ref, b_ref, c_ref, d_ref, wo_ref, g_mix_ref, g_pre_ref, wg_ref, wu_ref, wd_ref,
                 g_ffn_ref, o_ref, acc_ref):
    mix = (_dot(a_ref[...], wo_ref[0]) + _dot(b_ref[...], wo_ref[1])
           + _dot(c_ref[...], wo_ref[2]) + _dot(d_ref[...], wo_ref[3]))
    x = x_ref[...] + mix * _rms(mix) * g_mix_ref[...]
    h = (x * _rms(x) * g_pre_ref[...]).astype(BF16)
    acc_ref[...] = jnp.zeros_like(acc_ref)

    def body(ci, carry):
        gate = _dot(h, wg_ref[ci])
        up = _dot(h, wu_ref[ci])
        act = (gate * jax.nn.sigmoid(gate) * up).astype(BF16)
        acc_ref[...] += _dot(act, wd_ref[ci])
        return carry

    lax.fori_loop(0, D_FF // FF_CHUNK, body, 0, unroll=True)
    f = acc_ref[...]
    o_ref[...] = x + f * _rms(f) * g_ffn_ref[...]


def _post(x2d, a, b, c, d, wo, g_mix, g_pre, wg, wu, wd, g_ffn):
    t = x2d.shape[0]
    tok = lambda w: pl.BlockSpec((TOKEN_TILE, w), lambda i: (i, 0))
    vec = _const_spec((1, D_MODEL))
    return pl.pallas_call(
        _post_kernel,
        out_shape=jax.ShapeDtypeStruct((t, D_MODEL), F32),
        grid=(t // TOKEN_TILE,),
        in_specs=[
            tok(D_MODEL), tok(GROUP_WIDTH), tok(GROUP_WIDTH), tok(GROUP_WIDTH), tok(GROUP_WIDTH),
            _const_spec(wo.shape), vec, vec, _const_spec(wg.shape), _const_spec(wu.shape), _const_spec(wd.shape),
            vec,
        ],
        out_specs=tok(D_MODEL),
        scratch_shapes=[pltpu.VMEM((TOKEN_TILE, D_MODEL), F32)],
        compiler_params=_compiler_params(),
        name="outproj_swiglu",
    )(x2d, a, b, c, d, wo, g_mix, g_pre, wg, wu, wd, g_ffn)


def _layout_w_in(w):
    src32, sign32 = _rot_perm(MLA_ROPE_DIM, MLA_ROPE_DIM)
    k_rope = w[:, O_ML_KR:O_ML_KR + MLA_ROPE_DIM]
    cols = [
        w[:, :O_GQ_K],
        w[:, O_ML_CQ:O_ML_CKV],
        w[:, O_GQ_K:O_ML_CQ],
        w[:, O_ML_CKV:O_ML_KR],
        jnp.zeros((D_MODEL, MLA_NOPE_DIM), w.dtype), k_rope, k_rope[:, src32] * sign32,
    ]
    return jnp.concatenate(cols, axis=1).astype(BF16)


def _layout_mla_weights(w_uq, w_ukv):
    wq = w_uq.reshape(GROUP_WIDTH, N_HEADS, MLA_NOPE_DIM + MLA_ROPE_DIM)
    tail = 128 - MLA_NOPE_DIM - MLA_ROPE_DIM
    wuq_t = jnp.pad(wq, ((0, 0), (0, 0), (0, tail))).reshape(GROUP_WIDTH, -1).T
    wkv = w_ukv.reshape(MLA_KV_LORA, N_HEADS, MLA_NOPE_DIM + HEAD_DIM)
    w_k = jnp.pad(wkv[:, :, :MLA_NOPE_DIM], ((0, 0), (0, 0), (0, 128 - MLA_NOPE_DIM))).reshape(MLA_KV_LORA, -1)
    w_v_t = wkv[:, :, MLA_NOPE_DIM:].reshape(MLA_KV_LORA, -1).T
    return wuq_t.astype(BF16), w_k.astype(BF16), w_v_t.astype(BF16)


def _rotary_tables():
    pos = jnp.arange(SEQ)
    half = HEAD_DIM // 2
    inv = ROPE_THETA ** (-jnp.arange(0, half, 2, dtype=F32) / half)

    def angles(p):
        ang = p.astype(F32)[:, None] * inv[None, :]
        return jnp.concatenate([ang, ang], axis=-1)

    axial = jnp.concatenate([angles(pos // GRID_W), angles(pos % GRID_W)], axis=-1)
    seq = angles(pos)
    tail = 128 - MLA_NOPE_DIM - MLA_ROPE_DIM
    mla_cos = jnp.concatenate([jnp.ones((SEQ, MLA_NOPE_DIM), F32), jnp.cos(seq), jnp.zeros((SEQ, tail), F32)], -1)
    mla_sin = jnp.pad(jnp.sin(seq), ((0, 0), (MLA_NOPE_DIM, tail)))
    pair = lambda t: jnp.tile(t, (1, 2))
    return pair(jnp.cos(axial)), pair(jnp.sin(axial)), mla_cos, mla_sin


def kernel(x, pre_mix_norm, w_in, na_rel_bias, diff_lambda_q1, diff_lambda_k1, diff_lambda_q2, diff_lambda_k2,
           diff_subln, gqa_q_norm, gqa_k_norm, mla_q_norm, mla_kv_norm, mla_w_uq, mla_w_ukv, w_o, post_mix_norm,
           pre_ffn_norm, ffn_w_gate_up, ffn_w_down, post_ffn_norm):
    b, s, d = x.shape
    assert (s, d) == (SEQ, D_MODEL)
    depth = w_in.shape[0]
    src64, _ = _rot_perm(HEAD_DIM, HEAD_DIM // 2)
    ax_cos, ax_sin, mla_cos, mla_sin = _rotary_tables()
    row = lambda v: v.reshape(1, -1).astype(F32)
    n_chunks = D_FF // FF_CHUNK

    x2d = x.reshape(b * s, d)
    for l in range(depth):
        lambda_init = 0.8 - 0.6 * math.exp(-0.3 * l)
        proj = _inproj(x2d, row(pre_mix_norm[l]), _layout_w_in(w_in[l])).reshape(b, s, PROJ_WIDTH)

        a_out = _na_attention(proj, _na_tables(na_rel_bias[l]))
        b_out = _diff_attention(proj, row(diff_lambda_q1[l]), row(diff_lambda_k1[l]), row(diff_lambda_q2[l]),
                                row(diff_lambda_k2[l]), row(diff_subln[l]), lambda_init)
        pair_row = lambda v: row(jnp.tile(v, 2))
        c_out = _gqa_attention(proj, pair_row(gqa_q_norm[l]), pair_row(gqa_q_norm[l][src64]),
                               pair_row(gqa_k_norm[l]), pair_row(gqa_k_norm[l][src64]), ax_cos, ax_sin)
        mla_weights = _layout_mla_weights(mla_w_uq[l], mla_w_ukv[l])
        d_out = _mla_attention(proj, row(mla_q_norm[l]), row(mla_kv_norm[l]), mla_weights, mla_cos, mla_sin)

        gate_up = ffn_w_gate_up[l].astype(BF16)
        wg = gate_up[:, :D_FF].reshape(d, n_chunks, FF_CHUNK).transpose(1, 0, 2)
        wu = gate_up[:, D_FF:].reshape(d, n_chunks, FF_CHUNK).transpose(1, 0, 2)
        wd = ffn_w_down[l].astype(BF16).reshape(n_chunks, FF_CHUNK, d)
        wo = w_o[l].astype(BF16).reshape(N_HEADS, GROUP_WIDTH, d)
        flat = lambda t: t.reshape(b * s, GROUP_WIDTH)
        x2d = _post(x2d, flat(a_out), flat(b_out), flat(c_out), flat(d_out), wo, row(post_mix_norm[l]),
                    row(pre_ffn_norm[l]), wg, wu, wd, row(post_ffn_norm[l]))
    return x2d.reshape(b, s, d)
```

```python
import functools
import math

import numpy as np
import jax
import jax.numpy as jnp
from jax import lax
from jax.experimental import pallas as pl
from jax.experimental.pallas import tpu as pltpu

F32 = jnp.float32
BF16 = jnp.bfloat16

D_MODEL = 1024
SEQ = 2048
GRID_W = 64
GRID_ROWS = SEQ // GRID_W
HEAD_DIM = 64
N_HEADS = 4
GROUP_WIDTH = 256
EPS = 1e-6
ROPE_THETA = 10000.0

NA_WIN_ROWS = 8
NA_WIN_COLS = 16
DIFF_QK_DIM = 32
POS_SPLIT = 64
MLA_NOPE_DIM = 64
MLA_ROPE_DIM = 32
MLA_KV_LORA = 128
D_FF = 2816

O_GQ_Q, O_GQ_K, O_GQ_V, O_ML_CQ, O_ML_CKV, O_ML_KR = 1536, 1792, 1920, 2048, 2304, 2432

PROJ_WIDTH = 2560
B256_NA_Q, B256_NA_K, B256_NA_V, B256_DF_Q, B256_DF_K, B256_DF_V, B256_GQ_Q, B256_ML_CQ = range(8)
B128_GQ_K, B128_GQ_V, B128_ML_CKV, B128_ML_KR = range(16, 20)

VMEM_LIMIT_BYTES = 56 * 1024 * 1024

TOKEN_TILE = 1024
FF_CHUNK = 256
Q_TILE = 256
NA_GROUP_ROWS = 4
NA_BAND_ROWS = 12
NEG_BIG = -1e30
LOG2E = math.log2(math.e)


def _rot_perm(width, group):
    j = np.arange(width)
    jj = j % group
    half = group // 2
    src = (j // group) * group + (jj + half) % group
    sign = np.where(jj < half, -1.0, 1.0).astype(np.float32)
    return src, sign


def _compiler_params():
    return pltpu.CompilerParams(dimension_semantics=("arbitrary",), vmem_limit_bytes=VMEM_LIMIT_BYTES)


def _const_spec(shape):
    zeros = (0,) * len(shape)
    return pl.BlockSpec(shape, lambda i: zeros, pipeline_mode=pl.Buffered(1))


def _rms(x):
    return lax.rsqrt(jnp.mean(x * x, axis=-1, keepdims=True) + EPS)


def _dot(a, b):
    return jnp.dot(a, b, preferred_element_type=F32)


def _inproj_kernel(x_ref, g_ref, w_ref, o_ref):
    x = x_ref[...]
    h = (x * _rms(x) * g_ref[...]).astype(BF16)
    o_ref[...] = _dot(h, w_ref[...])


def _inproj(x2d, gain, w):
    t = x2d.shape[0]
    return pl.pallas_call(
        _inproj_kernel,
        out_shape=jax.ShapeDtypeStruct((t, PROJ_WIDTH), F32),
        grid=(t // TOKEN_TILE,),
        in_specs=[
            pl.BlockSpec((TOKEN_TILE, D_MODEL), lambda i: (i, 0)),
            _const_spec((1, D_MODEL)),
            _const_spec((D_MODEL, PROJ_WIDTH)),
        ],
        out_specs=pl.BlockSpec((TOKEN_TILE, PROJ_WIDTH), lambda i: (i, 0)),
        compiler_params=_compiler_params(),
        name="inproj",
    )(x2d, gain, w)


def _segment_mean_square(x, seg):
    w = x.shape[-1]
    same = (lax.broadcasted_iota(jnp.int32, (w, w), 0) // seg) == (lax.broadcasted_iota(jnp.int32, (w, w), 1) // seg)
    ones = jnp.where(same, 1.0, 0.0).astype(BF16)
    sq = x * x
    hi = sq.astype(BF16)
    lo = (sq - hi.astype(F32)).astype(BF16)
    return (_dot(hi, ones) + _dot(lo, ones)) * (1.0 / seg)


def _ones_row_block(width):
    return jnp.where(lax.broadcasted_iota(jnp.int32, (16, width), 0) == 0, 1.0, 0.0).astype(BF16)


def _normalised_heads(accs):
    return jnp.concatenate([acc[:HEAD_DIM] / acc[HEAD_DIM:HEAD_DIM + 1] for acc in accs], axis=0)


def _pipelined_softmax_pv(n_tiles, n_heads, n_chunks, score_chunks, value_chunk, combine, s_scr, acc_scr, o_ref,
                          exp_fn=jnp.exp2):
    assert n_heads % 2 == 0
    rows = s_scr.shape[1] // n_chunks

    def scores_into(slot, qi, h):
        m8 = None
        for c, st in enumerate(score_chunks(qi, h)):
            s_scr[slot, c * rows:(c + 1) * rows, :] = st
            cm = jnp.max(st.reshape(-1, 8, st.shape[-1]), axis=0)
            m8 = cm if m8 is None else jnp.maximum(m8, cm)
        return jnp.max(m8, axis=0, keepdims=True)

    def write_tile(qi):
        r0 = qi * Q_TILE if isinstance(qi, int) else pl.multiple_of(qi * Q_TILE, Q_TILE)
        o_ref[0, pl.ds(r0, Q_TILE), :] = combine([acc_scr[h] for h in range(n_heads)]).T.astype(o_ref.dtype)

    acc_scr[...] = jnp.ones_like(acc_scr)

    def body(qi, m):
        next_qi = jnp.minimum(qi + 1, n_tiles - 1)
        for h in range(n_heads):
            slot = h % 2
            nq, nh = (qi, h + 1) if h + 1 < n_heads else (next_qi, 0)
            next_m = scores_into(1 - slot, nq, nh)
            if h == 0:
                write_tile(jnp.maximum(qi - 1, 0))
            acc = None
            for c in range(n_chunks):
                p = exp_fn(s_scr[slot, c * rows:(c + 1) * rows, :] - m).astype(BF16)
                part = _dot(value_chunk(qi, h, c), p)
                acc = part if acc is None else acc + part
            acc_scr[h] = acc
            m = next_m
        return m

    lax.fori_loop(0, n_tiles, body, scores_into(0, 0, 0), unroll=2)
    write_tile(n_tiles - 1)


def _proj_spec(width, block):
    return pl.BlockSpec((1, SEQ, width), lambda b: (b, 0, block))


def _na_kernel(q_ref, k_ref, v_ref, tab_ref, o_ref, qt_scr, k_scr, vt_scr, s_scr, acc_scr):
    assert NA_GROUP_ROWS * GRID_W == Q_TILE and NA_BAND_ROWS % NA_GROUP_ROWS == 0
    n_tiles = SEQ // Q_TILE
    band_chunks = NA_BAND_ROWS // NA_GROUP_ROWS
    pair = 2 * HEAD_DIM
    q_scale = HEAD_DIM ** -0.5 * LOG2E

    def prep(ci, carry):
        r0 = pl.multiple_of(ci * Q_TILE, Q_TILE)
        rows = pl.ds(r0, Q_TILE)
        qt = (q_ref[0, rows, :] * q_scale).T
        row = lax.broadcasted_iota(jnp.int32, (pair, Q_TILE), 0)
        for h in range(N_HEADS):
            own = (row >= (h % 2) * HEAD_DIM) & (row < (h % 2 + 1) * HEAD_DIM)
            qt_scr[ci, h] = jnp.where(own, qt[(h // 2) * pair:(h // 2 + 1) * pair], 0.0).astype(BF16)
        for p in range(N_HEADS // 2):
            k_scr[p, rows, :] = k_ref[0, rows, p * pair:(p + 1) * pair].astype(BF16)
        vt = v_ref[0, rows, :].T.astype(BF16)
        for h in range(N_HEADS):
            vt_scr[h, ci, 0:HEAD_DIM, :] = vt[h * HEAD_DIM:(h + 1) * HEAD_DIM]
            vt_scr[h, ci, HEAD_DIM:, :] = _ones_row_block(Q_TILE)
        return carry

    lax.fori_loop(0, n_tiles, prep, 0, unroll=True)

    def key_chunk(gi, c):
        return jnp.clip(gi - (NA_WIN_ROWS // 2) // NA_GROUP_ROWS, 0, n_tiles - band_chunks) + c

    def score_chunks(gi, h):
        kind = jnp.where(gi == 0, 0, jnp.where(gi == n_tiles - 1, 2, 1))
        k0 = pl.multiple_of(key_chunk(gi, 0) * Q_TILE, Q_TILE)
        return [_dot(k_scr[h // 2, pl.ds(k0 + c * Q_TILE, Q_TILE), :], qt_scr[gi, h])
                + tab_ref[kind, h, c * Q_TILE:(c + 1) * Q_TILE, :] for c in range(band_chunks)]

    def value_chunk(gi, h, c):
        return vt_scr[h, key_chunk(gi, c)]

    _pipelined_softmax_pv(n_tiles, N_HEADS, band_chunks, score_chunks, value_chunk, _normalised_heads, s_scr, acc_scr,
                          o_ref)


def _na_tables(rel_bias):
    a = np.arange(NA_GROUP_ROWS)[:, None, None, None]
    c = np.arange(GRID_W)[None, :, None, None]
    i = np.arange(NA_BAND_ROWS)[None, None, :, None]
    kc = np.arange(GRID_W)[None, None, None, :]
    cs = np.clip(c - NA_WIN_COLS // 2, 0, GRID_W - NA_WIN_COLS)
    col_ok = (kc >= cs) & (kc < cs + NA_WIN_COLS)
    dc = kc - c + NA_WIN_COLS - 1
    col_sel = (dc[..., None] == np.arange(2 * NA_WIN_COLS - 1)) & col_ok[..., None]
    row_sels = []
    last_r0 = GRID_ROWS - NA_GROUP_ROWS
    for r0, band_row in ((0, 0), (NA_GROUP_ROWS, 0), (last_r0, GRID_ROWS - NA_BAND_ROWS)):
        r = r0 + a
        rs = np.clip(r - NA_WIN_ROWS // 2, 0, GRID_ROWS - NA_WIN_ROWS)
        key_row = band_row + i
        row_ok = (key_row >= rs) & (key_row < rs + NA_WIN_ROWS)
        dr = key_row - r + NA_WIN_ROWS - 1
        row_sels.append((dr[..., None] == np.arange(2 * NA_WIN_ROWS - 1)) & row_ok[..., None])
    row_sel = np.stack(row_sels)[:, :, 0, :, 0, :].astype(np.float32)
    col_sel = col_sel[0, :, 0, :, :].astype(np.float32)
    vals = jnp.einsum("taiu,huv,ckv->thikac", row_sel, rel_bias.astype(F32), col_sel,
                      precision=lax.Precision.HIGHEST)
    inside = np.einsum("taiu,ckv->tikac", row_sel, col_sel) > 0
    tab = jnp.where(inside[:, None], vals * LOG2E, NEG_BIG)
    q_rows, band = NA_GROUP_ROWS * GRID_W, NA_BAND_ROWS * GRID_W
    return tab.reshape(3, N_HEADS, band, q_rows)


def _na_attention(proj, tables):
    b = proj.shape[0]
    return pl.pallas_call(
        _na_kernel,
        out_shape=jax.ShapeDtypeStruct((b, SEQ, GROUP_WIDTH), BF16),
        grid=(b,),
        in_specs=[
            _proj_spec(256, B256_NA_Q), _proj_spec(256, B256_NA_K), _proj_spec(256, B256_NA_V),
            _const_spec(tables.shape),
        ],
        out_specs=pl.BlockSpec((1, SEQ, GROUP_WIDTH), lambda i: (i, 0, 0)),
        scratch_shapes=[pltpu.VMEM((SEQ // Q_TILE, N_HEADS, 2 * HEAD_DIM, Q_TILE), BF16),
                        pltpu.VMEM((N_HEADS // 2, SEQ, 2 * HEAD_DIM), BF16),
                        pltpu.VMEM((N_HEADS, SEQ // Q_TILE, HEAD_DIM + 16, Q_TILE), BF16),
                        pltpu.VMEM((2, NA_BAND_ROWS * GRID_W, Q_TILE), F32),
                        pltpu.VMEM((N_HEADS, HEAD_DIM + 16, Q_TILE), F32)],
        compiler_params=_compiler_params(),
        name="na_attention",
    )(proj, proj, proj, tables)


def _alibi_slope(h):
    return 2.0 ** (-8.0 * (h + 1) / N_HEADS)


def _diff_kernel(lambda_init, q_ref, k_ref, v_ref, lq1_ref, lk1_ref, lq2_ref, lk2_ref, subln_ref, diag_ref, o_ref,
                 qt_scr, qf_scr, k_scr, vt_scr, s_scr, acc_scr):
    scale = DIFF_QK_DIM ** -0.5
    n_tiles = SEQ // Q_TILE
    pair = 2 * HEAD_DIM
    n_feat = 16
    lam = (jnp.exp(jnp.sum(lq1_ref[...] * lk1_ref[...], axis=-1, keepdims=True))
           - jnp.exp(jnp.sum(lq2_ref[...] * lk2_ref[...], axis=-1, keepdims=True)) + lambda_init)

    def prep(ci, carry):
        r0 = pl.multiple_of(ci * Q_TILE, Q_TILE)
        rows = pl.ds(r0, Q_TILE)
        qt = (q_ref[0, rows, :] * scale).T
        row = lax.broadcasted_iota(jnp.int32, (pair, Q_TILE), 0)
        feat_row = lax.broadcasted_iota(jnp.int32, (n_feat, Q_TILE), 0)
        i = r0 + lax.broadcasted_iota(jnp.int32, (n_feat, Q_TILE), 1)
        i_lo = i % POS_SPLIT
        i_hi = i - i_lo
        for h in range(N_HEADS):
            group = qt[(h // 2) * pair:(h // 2 + 1) * pair]
            for mp in range(2):
                lo = (h % 2) * HEAD_DIM + mp * DIFF_QK_DIM
                keep = (row >= lo) & (row < lo + DIFF_QK_DIM)
                qt_scr[ci, 2 * h + mp] = jnp.where(keep, group, 0.0).astype(BF16)
            slope = _alibi_slope(h)
            feat = jnp.where(feat_row < 2, slope,
                             jnp.where(feat_row == 2, -slope * i_hi.astype(F32),
                                       jnp.where(feat_row == 3, -slope * i_lo.astype(F32), 0.0)))
            qf_scr[ci, h, 0:n_feat, :] = feat.astype(BF16)
            qf_scr[ci, h, n_feat:, :] = (-feat).astype(BF16)
        lane = lax.broadcasted_iota(jnp.int32, (Q_TILE, pair), 1)
        j = r0 + lax.broadcasted_iota(jnp.int32, (Q_TILE, pair), 0)
        j_lo = j % POS_SPLIT
        k_feat = jnp.where(lane == 0, (j - j_lo).astype(F32),
                           jnp.where(lane == 1, j_lo.astype(F32), jnp.where(lane < 4, 1.0, 0.0))).astype(BF16)
        for p in range(N_HEADS // 2):
            k_scr[p, rows, 0:pair] = k_ref[0, rows, p * pair:(p + 1) * pair].astype(BF16)
            k_scr[p, rows, pair:] = k_feat
        vt = v_ref[0, rows, :].T.astype(BF16)
        for h in range(N_HEADS):
            vt_scr[h, ci, 0:HEAD_DIM, :] = vt[h * HEAD_DIM:(h + 1) * HEAD_DIM]
            vt_scr[h, ci, HEAD_DIM:, :] = _ones_row_block(Q_TILE)
        return carry

    lax.fori_loop(0, n_tiles, prep, 0, unroll=True)

    def key_chunk(qi, c):
        wrapped = qi + c >= n_tiles
        return jnp.where(wrapped, qi + c - n_tiles, qi + c), wrapped

    def score_chunk(qi, item, c):
        h = item // 2
        kc, left = key_chunk(qi, c)
        form = 0 if c == 0 else pl.multiple_of(jnp.where(left, 0, n_feat), n_feat)
        feat = qf_scr[qi, h, pl.ds(form, n_feat), :]
        rhs = jnp.concatenate([qt_scr[qi, item], feat, jnp.zeros((pair - n_feat, Q_TILE), BF16)], axis=0)
        st = _dot(k_scr[h // 2, pl.ds(pl.multiple_of(kc * Q_TILE, Q_TILE), Q_TILE), :], rhs)
        return st + diag_ref[h] if c == 0 else st

    def score_chunks(qi, item):
        return [score_chunk(qi, item, c) for c in range(n_tiles)]

    def value_chunk(qi, item, c):
        return vt_scr[item // 2, key_chunk(qi, c)[0]]

    def combine(accs):
        outs = []
        for h in range(N_HEADS):
            a1, a2 = accs[2 * h], accs[2 * h + 1]
            o = a1[:HEAD_DIM] / a1[HEAD_DIM:HEAD_DIM + 1] - lam * (a2[:HEAD_DIM] / a2[HEAD_DIM:HEAD_DIM + 1])
            r = lax.rsqrt(jnp.mean(o * o, axis=0, keepdims=True) + EPS)
            outs.append(o * r * subln_ref[...] * (1.0 - lambda_init))
        return jnp.concatenate(outs, axis=0)

    _pipelined_softmax_pv(n_tiles, 2 * N_HEADS, n_tiles, score_chunks, value_chunk, combine, s_scr, acc_scr, o_ref,
                          exp_fn=jnp.exp)


def _diff_attention(proj, lq1, lk1, lq2, lk2, subln, lambda_init):
    b = proj.shape[0]
    vec = _const_spec((1, DIFF_QK_DIM))
    n_tiles = SEQ // Q_TILE
    rel = np.arange(Q_TILE)
    over = -2.0 * np.maximum(rel[:, None] - rel[None, :], 0).astype(np.float32)
    diag = jnp.asarray(np.stack([_alibi_slope(h) * over for h in range(N_HEADS)]))
    subln_cols = jnp.broadcast_to(subln.reshape(HEAD_DIM, 1), (HEAD_DIM, Q_TILE))
    return pl.pallas_call(
        functools.partial(_diff_kernel, lambda_init),
        out_shape=jax.ShapeDtypeStruct((b, SEQ, GROUP_WIDTH), BF16),
        grid=(b,),
        in_specs=[
            _proj_spec(256, B256_DF_Q), _proj_spec(256, B256_DF_K), _proj_spec(256, B256_DF_V),
            vec, vec, vec, vec, _const_spec((HEAD_DIM, Q_TILE)), _const_spec(diag.shape),
        ],
        out_specs=pl.BlockSpec((1, SEQ, GROUP_WIDTH), lambda i: (i, 0, 0)),
        scratch_shapes=[pltpu.VMEM((n_tiles, 2 * N_HEADS, 2 * HEAD_DIM, Q_TILE), BF16),
                        pltpu.VMEM((n_tiles, N_HEADS, 32, Q_TILE), BF16),
                        pltpu.VMEM((N_HEADS // 2, SEQ, 4 * HEAD_DIM), BF16),
                        pltpu.VMEM((N_HEADS, n_tiles, HEAD_DIM + 16, Q_TILE), BF16),
                        pltpu.VMEM((2, SEQ, Q_TILE), F32),
                        pltpu.VMEM((2 * N_HEADS, HEAD_DIM + 16, Q_TILE), F32)],
        compiler_params=_compiler_params(),
        name="diff_attention",
    )(proj, proj, proj, lq1, lk1, lq2, lk2, subln_cols, diag)


def _swap_rotary_halves(x):
    quarter = HEAD_DIM // 4
    first = (lax.broadcasted_iota(jnp.int32, x.shape, 1) % (2 * quarter)) < quarter
    lanes = x.shape[1]
    return jnp.where(first, pltpu.roll(x, lanes - quarter, 1), pltpu.roll(x, quarter, 1))


def _gqa_kernel(q_ref, k_ref, v_ref, qc_ref, qs_ref, kc_ref, ks_ref, o_ref,
                qt_scr, k_scr, vt_scr, s_scr, acc_scr):
    kv_heads = N_HEADS // 2
    n_tiles = SEQ // Q_TILE
    pair = 2 * HEAD_DIM
    quarter = HEAD_DIM // 4

    def normed_rotary(x, gain_cos, gain_sin):
        rotated = x * gain_cos + _swap_rotary_halves(x) * gain_sin
        return rotated * lax.rsqrt(_segment_mean_square(x, HEAD_DIM) + EPS)

    def normed_rotary_transposed(xt, gain_cos, gain_sin):
        swapped = jnp.concatenate([xt[(b ^ 1) * quarter:((b ^ 1) + 1) * quarter] for b in range(4)], axis=0)
        rotated = xt * gain_cos + swapped * gain_sin
        return rotated * lax.rsqrt(jnp.mean(xt * xt, axis=0, keepdims=True) + EPS)

    def prep(ci, carry):
        r0 = pl.multiple_of(ci * Q_TILE, Q_TILE)
        rows = pl.ds(r0, Q_TILE)
        zeros = jnp.zeros((HEAD_DIM, Q_TILE), BF16)
        for half in range(2):
            xt = q_ref[0, rows, half * pair:(half + 1) * pair].T
            for j in range(2):
                h = 2 * half + j
                kv = h // (N_HEADS // kv_heads)
                head_rows = slice(j * HEAD_DIM, (j + 1) * HEAD_DIM)
                yt = normed_rotary_transposed(xt[head_rows], qc_ref[ci, head_rows, :],
                                              qs_ref[ci, head_rows, :]).astype(BF16)
                for part in range(kv_heads):
                    block = yt if part == kv else zeros
                    qt_scr[ci, h * pair + part * HEAD_DIM:h * pair + (part + 1) * HEAD_DIM, :] = block
        k_scr[rows, :] = normed_rotary(k_ref[0, rows, :], kc_ref[rows, :], ks_ref[rows, :]).astype(BF16)
        vt = v_ref[0, rows, :].T.astype(BF16)
        for kv in range(kv_heads):
            vt_scr[kv, ci, 0:HEAD_DIM, :] = vt[kv * HEAD_DIM:(kv + 1) * HEAD_DIM]
            vt_scr[kv, ci, HEAD_DIM:, :] = _ones_row_block(Q_TILE)
        return carry

    lax.fori_loop(0, n_tiles, prep, 0, unroll=True)

    def score_chunks(qi, h):
        qt = qt_scr[qi, h * pair:(h + 1) * pair, :]
        return [_dot(k_scr[c * Q_TILE:(c + 1) * Q_TILE, :], qt) for c in range(n_tiles)]

    def value_chunk(qi, h, c):
        return vt_scr[h // (N_HEADS // kv_heads), c]

    _pipelined_softmax_pv(n_tiles, N_HEADS, n_tiles, score_chunks, value_chunk, _normalised_heads, s_scr, acc_scr,
                          o_ref)


def _gqa_attention(proj, gq, gqp, gk, gkp, cos, sin):
    b = proj.shape[0]
    tab = _const_spec((SEQ, 2 * HEAD_DIM))
    q_scale = HEAD_DIM ** -0.5 * LOG2E
    _, sign = _rot_perm(2 * HEAD_DIM, HEAD_DIM // 2)
    per_tile = lambda t: t.T.reshape(2 * HEAD_DIM, SEQ // Q_TILE, Q_TILE).transpose(1, 0, 2)
    tables = (per_tile(gq * cos * q_scale), per_tile(gqp * (sign * q_scale) * sin), gk * cos, gkp * sign * sin)
    tab_t = _const_spec((SEQ // Q_TILE, 2 * HEAD_DIM, Q_TILE))
    return pl.pallas_call(
        _gqa_kernel,
        out_shape=jax.ShapeDtypeStruct((b, SEQ, GROUP_WIDTH), BF16),
        grid=(b,),
        in_specs=[
            _proj_spec(256, B256_GQ_Q), _proj_spec(128, B128_GQ_K), _proj_spec(128, B128_GQ_V),
            tab_t, tab_t, tab, tab,
        ],
        out_specs=pl.BlockSpec((1, SEQ, GROUP_WIDTH), lambda i: (i, 0, 0)),
        scratch_shapes=[pltpu.VMEM((SEQ // Q_TILE, N_HEADS * 2 * HEAD_DIM, Q_TILE), BF16),
                        pltpu.VMEM((SEQ, 2 * HEAD_DIM), BF16),
                        pltpu.VMEM((N_HEADS // 2, SEQ // Q_TILE, HEAD_DIM + 16, Q_TILE), BF16),
                        pltpu.VMEM((2, SEQ, Q_TILE), F32),
                        pltpu.VMEM((N_HEADS, HEAD_DIM + 16, Q_TILE), F32)],
        compiler_params=_compiler_params(),
        name="gqa_attention",
    )(proj, proj, proj, *tables)


def _mla_kernel(cq_ref, ckv_ref, kr_ref, gq_ref, gkv_ref, wuq_ref, wukv_ref, wv_ref, cos_ref, sin_ref, qc_ref, qs_ref,
                o_ref, qt_scr, k_scr, vt_scr, s_scr, acc_scr):
    q_scale = (MLA_NOPE_DIM + MLA_ROPE_DIM) ** -0.5 * LOG2E
    n_tiles = SEQ // Q_TILE
    half_rope = MLA_ROPE_DIM // 2
    rope_end = MLA_NOPE_DIM + MLA_ROPE_DIM

    def prep(ci, carry):
        r0 = pl.multiple_of(ci * Q_TILE, Q_TILE)
        rows = pl.ds(r0, Q_TILE)
        cos, sin = cos_ref[rows, :], sin_ref[rows, :]
        cq = cq_ref[0, rows, :]
        cqn_t = (cq * _rms(cq) * gq_ref[...]).T.astype(BF16)
        ckv = ckv_ref[0, rows, :]
        ckvn_f32 = ckv * _rms(ckv) * gkv_ref[...]
        ckvn, ckvn_t = ckvn_f32.astype(BF16), ckvn_f32.T.astype(BF16)
        kr = kr_ref[0, rows, :]
        k_rope = kr * cos + pltpu.roll(kr, 128 - MLA_ROPE_DIM, 1) * sin
        k_rope2 = jnp.concatenate([k_rope, k_rope], axis=1)
        q_cos, q_sin = qc_ref[ci], qs_ref[ci]
        for p in range(N_HEADS // 2):
            cols = slice(p * 256, (p + 1) * 256)
            qt = _dot(wuq_ref[cols, :], cqn_t)
            for j in range(2):
                base = (2 * p + j) * 128
                rope = qt[j * 128 + MLA_NOPE_DIM:j * 128 + rope_end]
                swapped = jnp.concatenate([rope[half_rope:], rope[:half_rope]], axis=0)
                qt_scr[ci, base:base + MLA_NOPE_DIM, :] = (qt[j * 128:j * 128 + MLA_NOPE_DIM] * q_scale).astype(BF16)
                qt_scr[ci, base + MLA_NOPE_DIM:base + rope_end, :] = (rope * q_cos + swapped * q_sin).astype(BF16)
                qt_scr[ci, base + rope_end:base + 128, :] = jnp.zeros((128 - rope_end, Q_TILE), BF16)
            k2 = (_dot(ckvn, wukv_ref[:, cols]) + k_rope2).astype(BF16)
            k_scr[2 * p, rows, :] = k2[:, :128]
            k_scr[2 * p + 1, rows, :] = k2[:, 128:]
        vt = _dot(wv_ref[...], ckvn_t).astype(BF16)
        for h in range(N_HEADS):
            vt_scr[h, ci, 0:HEAD_DIM, :] = vt[h * HEAD_DIM:(h + 1) * HEAD_DIM]
            vt_scr[h, ci, HEAD_DIM:, :] = _ones_row_block(Q_TILE)
        return carry

    lax.fori_loop(0, n_tiles, prep, 0, unroll=True)

    def score_chunks(qi, h):
        qt = qt_scr[qi, h * 128:(h + 1) * 128, :]
        return [_dot(k_scr[h, c * Q_TILE:(c + 1) * Q_TILE, :], qt) for c in range(n_tiles)]

    def value_chunk(qi, h, c):
        return vt_scr[h, c]

    _pipelined_softmax_pv(n_tiles, N_HEADS, n_tiles, score_chunks, value_chunk, _normalised_heads, s_scr, acc_scr,
                          o_ref)


def _mla_attention(proj, gq, gkv, weights, cos, sin):
    b = proj.shape[0]
    tab = _const_spec((SEQ, 128))
    n_tiles = SEQ // Q_TILE
    q_scale = (MLA_NOPE_DIM + MLA_ROPE_DIM) ** -0.5 * LOG2E
    _, sign = _rot_perm(MLA_ROPE_DIM, MLA_ROPE_DIM)
    rope_lanes = slice(MLA_NOPE_DIM, MLA_NOPE_DIM + MLA_ROPE_DIM)
    per_tile = lambda t: t.T.reshape(MLA_ROPE_DIM, n_tiles, Q_TILE).transpose(1, 0, 2)
    q_cos = per_tile(cos[:, rope_lanes] * q_scale)
    q_sin = per_tile(sin[:, rope_lanes] * (sign * q_scale))
    tab_t = _const_spec((n_tiles, MLA_ROPE_DIM, Q_TILE))
    return pl.pallas_call(
        _mla_kernel,
        out_shape=jax.ShapeDtypeStruct((b, SEQ, GROUP_WIDTH), BF16),
        grid=(b,),
        in_specs=[
            _proj_spec(256, B256_ML_CQ), _proj_spec(128, B128_ML_CKV), _proj_spec(128, B128_ML_KR),
            _const_spec((1, GROUP_WIDTH)), _const_spec((1, MLA_KV_LORA)),
            *[_const_spec(w.shape) for w in weights], tab, tab, tab_t, tab_t,
        ],
        out_specs=pl.BlockSpec((1, SEQ, GROUP_WIDTH), lambda i: (i, 0, 0)),
        scratch_shapes=[pltpu.VMEM((SEQ // Q_TILE, N_HEADS * 128, Q_TILE), BF16),
                        pltpu.VMEM((N_HEADS, SEQ, 128), BF16),
                        pltpu.VMEM((N_HEADS, SEQ // Q_TILE, HEAD_DIM + 16, Q_TILE), BF16),
                        pltpu.VMEM((2, SEQ, Q_TILE), F32),
                        pltpu.VMEM((N_HEADS, HEAD_DIM + 16, Q_TILE), F32)],
        compiler_params=_compiler_params(),
        name="mla_attention",
    )(proj, proj, proj, gq, gkv, *weights, cos, sin, q_cos, q_sin)


def _post_kernel(x_ref, a_ref, b_ref, c_ref, d_ref, wo_ref, g_mix_ref, g_pre_ref, wg_ref, wu_ref, wd_ref,
                 g_ffn_ref, o_ref, acc_ref):
    mix = (_dot(a_ref[...], wo_ref[0]) + _dot(b_ref[...], wo_ref[1])
           + _dot(c_ref[...], wo_ref[2]) + _dot(d_ref[...], wo_ref[3]))
    x = x_ref[...] + mix * _rms(mix) * g_mix_ref[...]
    h = (x * _rms(x) * g_pre_ref[...]).astype(BF16)
    acc_ref[...] = jnp.zeros_like(acc_ref)

    def body(ci, carry):
        gate = _dot(h, wg_ref[ci])
        up = _dot(h, wu_ref[ci])
        act = (gate * jax.nn.sigmoid(gate) * up).astype(BF16)
        acc_ref[...] += _dot(act, wd_ref[ci])
        return carry

    lax.fori_loop(0, D_FF // FF_CHUNK, body, 0, unroll=True)
    f = acc_ref[...]
    o_ref[...] = x + f * _rms(f) * g_ffn_ref[...]


def _post(x2d, a, b, c, d, wo, g_mix, g_pre, wg, wu, wd, g_ffn):
    t = x2d.shape[0]
    tok = lambda w: pl.BlockSpec((TOKEN_TILE, w), lambda i: (i, 0))
    vec = _const_spec((1, D_MODEL))
    return pl.pallas_call(
        _post_kernel,
        out_shape=jax.ShapeDtypeStruct((t, D_MODEL), F32),
        grid=(t // TOKEN_TILE,),
        in_specs=[
            tok(D_MODEL), tok(GROUP_WIDTH), tok(GROUP_WIDTH), tok(GROUP_WIDTH), tok(GROUP_WIDTH),
            _const_spec(wo.shape), vec, vec, _const_spec(wg.shape), _const_spec(wu.shape), _const_spec(wd.shape),
            vec,
        ],
        out_specs=tok(D_MODEL),
        scratch_shapes=[pltpu.VMEM((TOKEN_TILE, D_MODEL), F32)],
        compiler_params=_compiler_params(),
        name="outproj_swiglu",
    )(x2d, a, b, c, d, wo, g_mix, g_pre, wg, wu, wd, g_ffn)


def _layout_w_in(w):
    src32, sign32 = _rot_perm(MLA_ROPE_DIM, MLA_ROPE_DIM)
    k_rope = w[:, O_ML_KR:O_ML_KR + MLA_ROPE_DIM]
    cols = [
        w[:, :O_GQ_K],
        w[:, O_ML_CQ:O_ML_CKV],
        w[:, O_GQ_K:O_ML_CQ],
        w[:, O_ML_CKV:O_ML_KR],
        jnp.zeros((D_MODEL, MLA_NOPE_DIM), w.dtype), k_rope, k_rope[:, src32] * sign32,
    ]
    return jnp.concatenate(cols, axis=1).astype(BF16)


def _layout_mla_weights(w_uq, w_ukv):
    wq = w_uq.reshape(GROUP_WIDTH, N_HEADS, MLA_NOPE_DIM + MLA_ROPE_DIM)
    tail = 128 - MLA_NOPE_DIM - MLA_ROPE_DIM
    wuq_t = jnp.pad(wq, ((0, 0), (0, 0), (0, tail))).reshape(GROUP_WIDTH, -1).T
    wkv = w_ukv.reshape(MLA_KV_LORA, N_HEADS, MLA_NOPE_DIM + HEAD_DIM)
    w_k = jnp.pad(wkv[:, :, :MLA_NOPE_DIM], ((0, 0), (0, 0), (0, 128 - MLA_NOPE_DIM))).reshape(MLA_KV_LORA, -1)
    w_v_t = wkv[:, :, MLA_NOPE_DIM:].reshape(MLA_KV_LORA, -1).T
    return wuq_t.astype(BF16), w_k.astype(BF16), w_v_t.astype(BF16)


def _rotary_tables():
    pos = jnp.arange(SEQ)
    half = HEAD_DIM // 2
    inv = ROPE_THETA ** (-jnp.arange(0, half, 2, dtype=F32) / half)

    def angles(p):
        ang = p.astype(F32)[:, None] * inv[None, :]
        return jnp.concatenate([ang, ang], axis=-1)

    axial = jnp.concatenate([angles(pos // GRID_W), angles(pos % GRID_W)], axis=-1)
    seq = angles(pos)
    tail = 128 - MLA_NOPE_DIM - MLA_ROPE_DIM
    mla_cos = jnp.concatenate([jnp.ones((SEQ, MLA_NOPE_DIM), F32), jnp.cos(seq), jnp.zeros((SEQ, tail), F32)], -1)
    mla_sin = jnp.pad(jnp.sin(seq), ((0, 0), (MLA_NOPE_DIM, tail)))
    pair = lambda t: jnp.tile(t, (1, 2))
    return pair(jnp.cos(axial)), pair(jnp.sin(axial)), mla_cos, mla_sin


def kernel(x, pre_mix_norm, w_in, na_rel_bias, diff_lambda_q1, diff_lambda_k1, diff_lambda_q2, diff_lambda_k2,
           diff_subln, gqa_q_norm, gqa_k_norm, mla_q_norm, mla_kv_norm, mla_w_uq, mla_w_ukv, w_o, post_mix_norm,
           pre_ffn_norm, ffn_w_gate_up, ffn_w_down, post_ffn_norm):
    b, s, d = x.shape
    assert (s, d) == (SEQ, D_MODEL)
    depth = w_in.shape[0]
    src64, _ = _rot_perm(HEAD_DIM, HEAD_DIM // 2)
    ax_cos, ax_sin, mla_cos, mla_sin = _rotary_tables()
    row = lambda v: v.reshape(1, -1).astype(F32)
    n_chunks = D_FF // FF_CHUNK

    x2d = x.reshape(b * s, d)
    for l in range(depth):
        lambda_init = 0.8 - 0.6 * math.exp(-0.3 * l)
        proj = _inproj(x2d, row(pre_mix_norm[l]), _layout_w_in(w_in[l])).reshape(b, s, PROJ_WIDTH)

        a_out = _na_attention(proj, _na_tables(na_rel_bias[l]))
        b_out = _diff_attention(proj, row(diff_lambda_q1[l]), row(diff_lambda_k1[l]), row(diff_lambda_q2[l]),
                                row(diff_lambda_k2[l]), row(diff_subln[l]), lambda_init)
        pair_row = lambda v: row(jnp.tile(v, 2))
        c_out = _gqa_attention(proj, pair_row(gqa_q_norm[l]), pair_row(gqa_q_norm[l][src64]),
                               pair_row(gqa_k_norm[l]), pair_row(gqa_k_norm[l][src64]), ax_cos, ax_sin)
        mla_weights = _layout_mla_weights(mla_w_uq[l], mla_w_ukv[l])
        d_out = _mla_attention(proj, row(mla_q_norm[l]), row(mla_kv_norm[l]), mla_weights, mla_cos, mla_sin)

        gate_up = ffn_w_gate_up[l].astype(BF16)
        wg = gate_up[:, :D_FF].reshape(d, n_chunks, FF_CHUNK).transpose(1, 0, 2)
        wu = gate_up[:, D_FF:].reshape(d, n_chunks, FF_CHUNK).transpose(1, 0, 2)
        wd = ffn_w_down[l].astype(BF16).reshape(n_chunks, FF_CHUNK, d)
        wo = w_o[l].astype(BF16).reshape(N_HEADS, GROUP_WIDTH, d)
        flat = lambda t: t.reshape(b * s, GROUP_WIDTH)
        x2d = _post(x2d, flat(a_out), flat(b_out), flat(c_out), flat(d_out), wo, row(post_mix_norm[l]),
                    row(pre_ffn_norm[l]), wg, wu, wd, row(post_ffn_norm[l]))
    return x2d.reshape(b, s, d)
```
